```python
import math
import jax, jax.numpy as jnp
from jax import lax
import numpy as np

D_MODEL = 1024
BATCH = 8
SEQ = 2048
DEPTH = 4
DEC_BATCH = 32
DEC_SEQ = 1
PAST_LEN = 8192
PAGE_SIZE = 128

N_A_LAYERS = DEPTH // 2
N_B_LAYERS = DEPTH - N_A_LAYERS
SSM_GROUP = 16
N_SSM_GROUPS = D_MODEL // SSM_GROUP
SSM_STATE = 64
DT_MIN = 1e-3
DT_MAX = 1e-1
DIL_WINDOWS = (128, 512, 2048)
DIL_RATES = (1, 4, 16)
N_DIL = 3
HEADS_PER_GROUP = 8
HEAD_DIM = 64
ATTN_WIDTH = HEADS_PER_GROUP * HEAD_DIM
D_FF = 2816
EPS = 1e-6
NEG = -1e30

kernel_name = 'yoco_s5_dilated_alibi_macaron'


def _rmsnorm(x, g):
    xf = x.astype(jnp.float32)
    y = xf * lax.rsqrt(jnp.mean(xf * xf, axis=-1, keepdims=True) + EPS)
    return (y * g.astype(jnp.float32)).astype(x.dtype)


def _swiglu(x, w_in, w_out):
    hcat = x @ w_in
    return (jax.nn.silu(hcat[..., :D_FF]) * hcat[..., D_FF:]) @ w_out


def _alibi_slopes():
    n = N_DIL * HEADS_PER_GROUP
    s = jnp.exp2(-8.0 * jnp.arange(1, n + 1, dtype=jnp.float32) / n)
    return s.reshape(N_DIL, HEADS_PER_GROUP)


def _cmul_combine(e1, e2):
    a1r, a1i, b1r, b1i = e1
    a2r, a2i, b2r, b2i = e2
    return (a2r * a1r - a2i * a1i,
            a2r * a1i + a2i * a1r,
            a2r * b1r - a2i * b1i + b2r,
            a2r * b1i + a2i * b1r + b2i)


def _s5_mixer(u, s0_re, s0_im, log_dt, lam_re, lam_im, b_re, b_im, c_re, c_im, d_skip, glu_w, glu_b):
    f32 = jnp.float32
    bn, t_len, _ = u.shape
    ug = u.astype(f32).reshape(bn, t_len, N_SSM_GROUPS, SSM_GROUP)
    dt = jnp.exp(log_dt.astype(f32))[:, None]
    lr = lam_re.astype(f32)
    li = lam_im.astype(f32)
    mag = jnp.exp(lr * dt)
    ar = mag * jnp.cos(li * dt)
    ai = mag * jnp.sin(li * dt)
    den = lr * lr + li * li
    cr = ((ar - 1.0) * lr + ai * li) / den
    ci = (ai * lr - (ar - 1.0) * li) / den
    br = b_re.astype(f32)
    bi = b_im.astype(f32)
    bbr = cr[..., None] * br - ci[..., None] * bi
    bbi = cr[..., None] * bi + ci[..., None] * br
    bu_r = jnp.einsum('btgh,gph->tbgp', ug, bbr)
    bu_i = jnp.einsum('btgh,gph->tbgp', ug, bbi)
    if s0_re is not None:
        s0r = s0_re.astype(f32)
        s0i = s0_im.astype(f32)
        bu_r = bu_r.at[0].add(ar * s0r - ai * s0i)
        bu_i = bu_i.at[0].add(ar * s0i + ai * s0r)
    a_r = jnp.broadcast_to(ar, (t_len, 1) + ar.shape)
    a_i = jnp.broadcast_to(ai, (t_len, 1) + ai.shape)
    _, _, xr, xi = lax.associative_scan(_cmul_combine, (a_r, a_i, bu_r, bu_i), axis=0)
    y = (jnp.einsum('tbgp,ghp->btgh', xr, c_re.astype(f32))
         - jnp.einsum('tbgp,ghp->btgh', xi, c_im.astype(f32))
         + d_skip.astype(f32) * ug)
    act = jax.nn.gelu(y.reshape(bn, t_len, D_MODEL))
    z = act @ glu_w + glu_b
    out = z[..., :D_MODEL] * jax.nn.sigmoid(z[..., D_MODEL:])
    return out, xr[-1], xi[-1]


def _to_strided(t, n_res, nb, r_len, dil):
    bn, s_len = t.shape[:2]
    rest = t.shape[2:]
    pad_rest = [(0, 0)] * len(rest)
    t = jnp.pad(t, [(0, 0), (0, n_res * dil - s_len)] + pad_rest)
    t = jnp.moveaxis(t.reshape((bn, n_res, dil) + rest), 2, 1)
    t = jnp.pad(t, [(0, 0), (0, 0), (0, nb * r_len - n_res)] + pad_rest)
    return t.reshape((bn, dil, nb, r_len) + rest)


def _from_strided(t, s_len, n_res):
    bn, dil, nb, r_len = t.shape[:4]
    rest = t.shape[4:]
    t = t.reshape((bn, dil, nb * r_len) + rest)[:, :, :n_res]
    t = jnp.moveaxis(t, 1, 2)
    return t.reshape((bn, n_res * dil) + rest)[:, :s_len]


def _dilated_attn_prompt(q, k, v, dil, window, slopes):
    f32 = jnp.float32
    bn, s_len, n_h, e = q.shape
    r_len = window // dil
    n_res = -(-s_len // dil)
    nb = -(-n_res // r_len)
    qs = _to_strided(q.astype(f32), n_res, nb, r_len, dil)
    ks = _to_strided(k.astype(f32), n_res, nb, r_len, dil)
    vs = _to_strided(v.astype(f32), n_res, nb, r_len, dil)

    def with_prev(t):
        prev = jnp.pad(t, [(0, 0), (0, 0), (1, 0)] + [(0, 0)] * 3)[:, :, :-1]
        return jnp.concatenate([prev, t], axis=3)

    kb = with_prev(ks)
    vb = with_prev(vs)
    s = jnp.einsum('bdnqhe,bdnkhe->bdnqhk', qs, kb) * (e ** -0.5)
    qi = jnp.arange(r_len)[:, None]
    kj = jnp.arange(2 * r_len)[None, :]
    diff = r_len + qi - kj
    kidx = (jnp.arange(nb)[:, None, None] - 1) * r_len + kj[None]
    valid = (diff >= 0) & (diff <= r_len) & (kidx >= 0)
    bias = -slopes[None, :, None] * (dil * diff).astype(f32)[:, None, :]
    s = jnp.where(valid[:, :, None, :], s + bias, NEG)
    m = jnp.max(s, axis=-1)
    p = jnp.exp(s - m[..., None])
    l = jnp.sum(p, axis=-1)
    acc = jnp.einsum('bdnqhk,bdnkhe->bdnqhe', p, vb)
    return (_from_strided(acc, s_len, n_res), _from_strided(m, s_len, n_res),
            _from_strided(l, s_len, n_res))


def _dilated_attn_sample(q, k_new, v_new, k_buf, v_buf, dil, window, slopes):
    f32 = jnp.float32
    e = q.shape[-1]
    t_len = q.shape[1]
    lb = k_buf.shape[1]
    r_len = window // dil
    kc = jnp.concatenate([k_buf.astype(f32), k_new.astype(f32)], axis=1)
    vc = jnp.concatenate([v_buf.astype(f32), v_new.astype(f32)], axis=1)
    ti = jnp.arange(t_len)[:, None]
    kk = jnp.arange(r_len + 1)[None, :]
    idx = lb + ti - kk * dil
    valid = idx >= 0
    idxc = jnp.maximum(idx, 0)
    kg = kc[:, idxc]
    vg = vc[:, idxc]
    s = jnp.einsum('bthe,btkhe->bthk', q.astype(f32), kg) * (e ** -0.5)
    bias = -slopes[:, None] * (dil * kk).astype(f32)
    s = jnp.where(valid[:, None, :], s + bias, NEG)
    m = jnp.max(s, axis=-1)
    p = jnp.exp(s - m[..., None])
    l = jnp.sum(p, axis=-1)
    acc = jnp.einsum('bthk,btkhe->bthe', p, vg)
    return acc, m, l


def _dilated_mixer(u, kv_groups, kv_bufs, w_q, w_o):
    bn, t_len, _ = u.shape
    q = (u @ w_q).reshape(bn, t_len, N_DIL, HEADS_PER_GROUP, HEAD_DIM)
    slopes = _alibi_slopes()
    accs, ms, ls = [], [], []
    for g in range(N_DIL):
        k = kv_groups[g][:, :, 0]
        v = kv_groups[g][:, :, 1]
        if kv_bufs is None:
            acc, m, l = _dilated_attn_prompt(q[:, :, g], k, v, DIL_RATES[g], DIL_WINDOWS[g], slopes[g])
        else:
            acc, m, l = _dilated_attn_sample(q[:, :, g], k, v, kv_bufs[g][:, :, 0], kv_bufs[g][:, :, 1],
                                             DIL_RATES[g], DIL_WINDOWS[g], slopes[g])
        accs.append(acc)
        ms.append(m)
        ls.append(l)
    m_all = jnp.stack(ms)
    l_all = jnp.stack(ls)
    acc_all = jnp.stack(accs)
    w = jnp.exp(m_all - jnp.max(m_all, axis=0, keepdims=True))
    den = jnp.sum(w * l_all, axis=0)
    merged = jnp.sum(w[..., None] * acc_all, axis=0) / den[..., None]
    return merged.reshape(bn, t_len, ATTN_WIDTH) @ w_o


def _trunk(x, ssm_re0, ssm_im0, kv_bufs, weights):
    (norm_ffn1, ffn1_w_in, ffn1_w_out, norm_mix, norm_ffn2, ffn2_w_in, ffn2_w_out, norm_kv, norm_final,
     ssm_log_dt, ssm_lambda_re, ssm_lambda_im, ssm_b_re, ssm_b_im, ssm_c_re, ssm_c_im, ssm_d, glu_w, glu_b,
     attn_w_q, attn_w_kv, attn_w_o) = weights
    bn, t_len, _ = x.shape
    h = x
    ssm_re, ssm_im = [], []
    kv_groups = None
    for layer in range(DEPTH):
        if layer == N_A_LAYERS:
            kv = (_rmsnorm(h, norm_kv) @ attn_w_kv).reshape(bn, t_len, N_DIL, 2, HEADS_PER_GROUP, HEAD_DIM)
            kv_groups = [kv[:, :, g] for g in range(N_DIL)]
        h = h + (0.5 * _swiglu(_rmsnorm(h, norm_ffn1[layer]), ffn1_w_in[layer], ffn1_w_out[layer])).astype(h.dtype)
        u = _rmsnorm(h, norm_mix[layer])
        if layer < N_A_LAYERS:
            s0r = None if ssm_re0 is None else ssm_re0[layer]
            s0i = None if ssm_im0 is None else ssm_im0[layer]
            mix, sr, si = _s5_mixer(u, s0r, s0i, ssm_log_dt[layer], ssm_lambda_re[layer], ssm_lambda_im[layer],
                                    ssm_b_re[layer], ssm_b_im[layer], ssm_c_re[layer], ssm_c_im[layer],
                                    ssm_d[layer], glu_w[layer], glu_b[layer])
            ssm_re.append(sr)
            ssm_im.append(si)
        else:
            b = layer - N_A_LAYERS
            mix = _dilated_mixer(u, kv_groups, kv_bufs, attn_w_q[b], attn_w_o[b])
        h = h + mix.astype(h.dtype)
        h = h + (0.5 * _swiglu(_rmsnorm(h, norm_ffn2[layer]), ffn2_w_in[layer], ffn2_w_out[layer])).astype(h.dtype)
    return _rmsnorm(h, norm_final), jnp.stack(ssm_re), jnp.stack(ssm_im), kv_groups


def setup_inputs(seed: int = 0) -> dict:
    key = jax.random.key(seed)
    ks = jax.random.split(key, 32)
    f32 = jnp.float32
    nrm = lambda k, shape, s: jax.random.normal(k, shape, f32) * s
    out = {}
    out['x_prompt'] = nrm(ks[0], (BATCH, SEQ, D_MODEL), 1.0)
    out['x_sample'] = nrm(ks[1], (DEC_BATCH, DEC_SEQ, D_MODEL), 1.0)
    out['state_ssm_re'] = nrm(ks[2], (N_A_LAYERS, DEC_BATCH, N_SSM_GROUPS, SSM_STATE), 0.3)
    out['state_ssm_im'] = nrm(ks[3], (N_A_LAYERS, DEC_BATCH, N_SSM_GROUPS, SSM_STATE), 0.3)
    out['cache_kv_w128'] = nrm(ks[4], (DEC_BATCH, min(DIL_WINDOWS[0], PAST_LEN), 2, HEADS_PER_GROUP, HEAD_DIM), 1.0)
    out['cache_kv_w512'] = nrm(ks[5], (DEC_BATCH, min(DIL_WINDOWS[1], PAST_LEN), 2, HEADS_PER_GROUP, HEAD_DIM), 1.0)
    out['cache_kv_w2048'] = nrm(ks[6], (DEC_BATCH, min(DIL_WINDOWS[2], PAST_LEN), 2, HEADS_PER_GROUP, HEAD_DIM), 1.0)
    out['norm_ffn1'] = 1.0 + nrm(ks[7], (DEPTH, D_MODEL), 0.02)
    out['ffn1_w_in'] = nrm(ks[8], (DEPTH, D_MODEL, 2 * D_FF), D_MODEL ** -0.5)
    out['ffn1_w_out'] = nrm(ks[9], (DEPTH, D_FF, D_MODEL), D_FF ** -0.5)
    out['norm_mix'] = 1.0 + nrm(ks[10], (DEPTH, D_MODEL), 0.02)
    out['norm_ffn2'] = 1.0 + nrm(ks[11], (DEPTH, D_MODEL), 0.02)
    out['ffn2_w_in'] = nrm(ks[12], (DEPTH, D_MODEL, 2 * D_FF), D_MODEL ** -0.5)
    out['ffn2_w_out'] = nrm(ks[13], (DEPTH, D_FF, D_MODEL), D_FF ** -0.5)
    out['norm_kv'] = 1.0 + nrm(ks[14], (D_MODEL,), 0.02)
    out['norm_final'] = 1.0 + nrm(ks[15], (D_MODEL,), 0.02)
    out['ssm_log_dt'] = jax.random.uniform(ks[16], (N_A_LAYERS, N_SSM_GROUPS), f32,
                                           math.log(DT_MIN), math.log(DT_MAX))
    out['ssm_lambda_re'] = -0.5 + nrm(ks[17], (N_A_LAYERS, N_SSM_GROUPS, SSM_STATE), 0.01)
    out['ssm_lambda_im'] = (math.pi * jnp.arange(SSM_STATE, dtype=f32))[None, None, :] + nrm(
        ks[18], (N_A_LAYERS, N_SSM_GROUPS, SSM_STATE), 0.01)
    b_scale = (2.0 * SSM_GROUP) ** -0.5
    out['ssm_b_re'] = nrm(ks[19], (N_A_LAYERS, N_SSM_GROUPS, SSM_STATE, SSM_GROUP), b_scale)
    out['ssm_b_im'] = nrm(ks[20], (N_A_LAYERS, N_SSM_GROUPS, SSM_STATE, SSM_GROUP), b_scale)
    out['ssm_c_re'] = nrm(ks[21], (N_A_LAYERS, N_SSM_GROUPS, SSM_GROUP, SSM_STATE), 0.5)
    out['ssm_c_im'] = nrm(ks[22], (N_A_LAYERS, N_SSM_GROUPS, SSM_GROUP, SSM_STATE), 0.5)
    out['ssm_d'] = nrm(ks[23], (N_A_LAYERS, N_SSM_GROUPS, SSM_GROUP), 1.0)
    out['glu_w'] = nrm(ks[24], (N_A_LAYERS, D_MODEL, 2 * D_MODEL), D_MODEL ** -0.5)
    out['glu_b'] = nrm(ks[25], (N_A_LAYERS, 2 * D_MODEL), 0.01)
    out['attn_w_q'] = nrm(ks[26], (N_B_LAYERS, D_MODEL, N_DIL * ATTN_WIDTH), D_MODEL ** -0.5)
    out['attn_w_kv'] = nrm(ks[27], (D_MODEL, N_DIL * 2 * ATTN_WIDTH), D_MODEL ** -0.5)
    out['attn_w_o'] = nrm(ks[28], (N_B_LAYERS, ATTN_WIDTH, D_MODEL), ATTN_WIDTH ** -0.5)
    return out


def reference(x_prompt, x_sample, state_ssm_re, state_ssm_im, cache_kv_w128, cache_kv_w512, cache_kv_w2048,
              norm_ffn1, ffn1_w_in, ffn1_w_out, norm_mix, norm_ffn2, ffn2_w_in, ffn2_w_out, norm_kv, norm_final,
              ssm_log_dt, ssm_lambda_re, ssm_lambda_im, ssm_b_re, ssm_b_im, ssm_c_re, ssm_c_im, ssm_d,
              glu_w, glu_b, attn_w_q, attn_w_kv, attn_w_o):
    weights = (norm_ffn1, ffn1_w_in, ffn1_w_out, norm_mix, norm_ffn2, ffn2_w_in, ffn2_w_out, norm_kv, norm_final,
               ssm_log_dt, ssm_lambda_re, ssm_lambda_im, ssm_b_re, ssm_b_im, ssm_c_re, ssm_c_im, ssm_d,
               glu_w, glu_b, attn_w_q, attn_w_kv, attn_w_o)
    y_prompt, ssm_re_p, ssm_im_p, kv_p = _trunk(x_prompt, None, None, None, weights)
    y_sample, ssm_re_s, ssm_im_s, kv_s = _trunk(x_sample, state_ssm_re, state_ssm_im,
                                                 (cache_kv_w128, cache_kv_w512, cache_kv_w2048), weights)
    s_len = x_prompt.shape[1]
    kv128_p = kv_p[0][:, s_len - min(DIL_WINDOWS[0], s_len):]
    kv512_p = kv_p[1][:, s_len - min(DIL_WINDOWS[1], s_len):]
    kv2048_p = kv_p[2][:, s_len - min(DIL_WINDOWS[2], s_len):]
    return (y_prompt, y_sample, ssm_re_p, ssm_im_p, kv128_p, kv512_p, kv2048_p,
            ssm_re_s, ssm_im_s, kv_s[0], kv_s[1], kv_s[2])
```

```python
import functools

import jax
import jax.numpy as jnp
from jax import lax
from jax.experimental import pallas as pl
from jax.experimental.pallas import tpu as pltpu

F32 = jnp.float32
BF16 = jnp.bfloat16

D_MODEL = 1024
DEPTH = 4
N_A_LAYERS = DEPTH // 2
SSM_GROUP = 16
N_SSM_GROUPS = D_MODEL // SSM_GROUP
SSM_STATE = 64
STATE_W = N_SSM_GROUPS * SSM_STATE
DIL_WINDOWS = (128, 512, 2048)
DIL_RATES = (1, 4, 16)
N_DIL = 3
HEADS_PER_GROUP = 8
HEAD_DIM = 64
ATTN_WIDTH = HEADS_PER_GROUP * HEAD_DIM
R_LEN = DIL_WINDOWS[0] // DIL_RATES[0]
D_FF = 2816
EPS = 1e-6
NEG = -1e30

LANES = 128
CH_BLOCKS = D_MODEL // LANES
GROUPS_PER_BLOCK = LANES // SSM_GROUP
STATE_BLOCK = GROUPS_PER_BLOCK * SSM_STATE
VMEM_LIMIT_BYTES = 56 * 1024 * 1024

assert all(w // d == R_LEN for w, d in zip(DIL_WINDOWS, DIL_RATES))


def _params(*semantics):
    return pltpu.CompilerParams(dimension_semantics=semantics, vmem_limit_bytes=VMEM_LIMIT_BYTES)


def _const_spec(shape):
    zeros = (0,) * len(shape)
    return pl.BlockSpec(shape, lambda *_: zeros, pipeline_mode=pl.Buffered(1))


def _rms(x, g):
    return x * lax.rsqrt(jnp.mean(x * x, axis=-1, keepdims=True) + EPS) * g


def _ffn_kernel(x_ref, g_ref, win_ref, wout_ref, *rest, n_chunks, final):
    if final:
        gf_ref, o_ref = rest
    else:
        (o_ref,) = rest
    x = x_ref[...]
    xn = _rms(x, g_ref[...]).astype(BF16)
    fc = D_FF // n_chunks
    acc = None
    for c in range(n_chunks):
        a = jnp.dot(xn, win_ref[:, c * fc:(c + 1) * fc], preferred_element_type=F32)
        b = jnp.dot(xn, win_ref[:, D_FF + c * fc:D_FF + (c + 1) * fc], preferred_element_type=F32)
        act = (a * jax.nn.sigmoid(a) * b).astype(BF16)
        d = jnp.dot(act, wout_ref[c * fc:(c + 1) * fc, :], preferred_element_type=F32)
        acc = d if acc is None else acc + d
    y = x + 0.5 * acc
    if final:
        y = _rms(y, gf_ref[...])
    o_ref[...] = y


def _ffn(x, g, w_in, w_out, final_g=None, tm=512, n_chunks=2):
    n = x.shape[0]
    tm = min(tm, n)
    assert n % tm == 0 and D_FF % (n_chunks * LANES) == 0
    final = final_g is not None
    in_specs = [pl.BlockSpec((tm, D_MODEL), lambda i: (i, 0)),
                _const_spec((1, D_MODEL)),
                _const_spec((D_MODEL, 2 * D_FF)),
                _const_spec((D_FF, D_MODEL))]
    args = [x, g.reshape(1, D_MODEL), w_in, w_out]
    if final:
        in_specs.append(_const_spec((1, D_MODEL)))
        args.append(final_g.reshape(1, D_MODEL))
    return pl.pallas_call(
        functools.partial(_ffn_kernel, n_chunks=n_chunks, final=final),
        grid=(n // tm,),
        in_specs=in_specs,
        out_specs=pl.BlockSpec((tm, D_MODEL), lambda i: (i, 0)),
        out_shape=jax.ShapeDtypeStruct((n, D_MODEL), F32),
        compiler_params=_params("parallel"),
        name="ffn",
    )(*args)


def _disc_kernel(ldt_ref, lr_ref, li_ref, br_ref, bi_ref, ar_ref, ai_ref, bbr_ref, bbi_ref):
    dt = jnp.exp(ldt_ref[...])
    lr = lr_ref[...]
    li = li_ref[...]
    mag = jnp.exp(lr * dt)
    ar = mag * jnp.cos(li * dt)
    ai = mag * jnp.sin(li * dt)
    den = lr * lr + li * li
    cr = ((ar - 1.0) * lr + ai * li) / den
    ci = (ai * lr - (ar - 1.0) * li) / den
    ar_ref[...] = ar
    ai_ref[...] = ai
    br = br_ref[...]
    bi = bi_ref[...]
    crb = cr[:, None, :]
    cib = ci[:, None, :]
    bbr_ref[...] = crb * br - cib * bi
    bbi_ref[...] = crb * bi + cib * br


def _discretise(log_dt, lam_re, lam_im, b_re, b_im):
    g, p, hg = N_SSM_GROUPS, SSM_STATE, SSM_GROUP
    gp = jax.ShapeDtypeStruct((g, p), F32)
    ghp = jax.ShapeDtypeStruct((g, hg, p), F32)
    return pl.pallas_call(
        _disc_kernel,
        out_shape=(gp, gp, ghp, ghp),
        name="s5_disc",
    )(log_dt.reshape(g, 1), lam_re, lam_im, jnp.swapaxes(b_re, 1, 2), jnp.swapaxes(b_im, 1, 2))


def _block_diag_in(bb):
    x = bb.reshape(CH_BLOCKS, GROUPS_PER_BLOCK, SSM_GROUP, SSM_STATE)
    eye = jnp.eye(GROUPS_PER_BLOCK, dtype=bb.dtype)
    return jnp.einsum('kghp,gj->kghjp', x, eye).reshape(CH_BLOCKS, LANES, STATE_BLOCK)


def _block_diag_out(c):
    x = c.reshape(CH_BLOCKS, GROUPS_PER_BLOCK, SSM_GROUP, SSM_STATE)
    eye = jnp.eye(GROUPS_PER_BLOCK, dtype=c.dtype)
    return jnp.einsum('kghp,gj->kgpjh', x, eye).reshape(CH_BLOCKS, STATE_BLOCK, LANES)


SCAN_COLS = 512


def _s5_kernel(h_ref, g_ref, s0r_ref, s0i_ref, ar_ref, ai_ref, wb_ref, wcr_ref, wci_ref, d_ref,
               gw_ref, gb_ref, o_ref, sr_ref, si_ref, xr_s, xi_s, *, rb, tt):
    @pl.when(pl.program_id(0) == 0)
    def _():
        sr_ref[...] = s0r_ref[...]
        si_ref[...] = s0i_ref[...]

    h = h_ref[...]
    u = _rms(h, g_ref[...])
    ub = u.astype(BF16)
    for k in range(CH_BLOCKS):
        bu = jnp.dot(ub[:, k * LANES:(k + 1) * LANES], wb_ref[k], preferred_element_type=F32)
        xr_s[:, k * STATE_BLOCK:(k + 1) * STATE_BLOCK] = bu[:, :STATE_BLOCK]
        xi_s[:, k * STATE_BLOCK:(k + 1) * STATE_BLOCK] = bu[:, STATE_BLOCK:]

    for c in range(STATE_W // SCAN_COLS):
        cs = slice(c * SCAN_COLS, (c + 1) * SCAN_COLS)
        a_r = jnp.broadcast_to(ar_ref[:, cs], (rb, SCAN_COLS))
        a_i = jnp.broadcast_to(ai_ref[:, cs], (rb, SCAN_COLS))

        def step(t, carry, cs=cs, a_r=a_r, a_i=a_i):
            s_r, s_i = carry
            rows = pl.ds(pl.multiple_of(t * rb, rb), rb)
            n_r = a_r * s_r - a_i * s_i + xr_s[rows, cs]
            n_i = a_r * s_i + a_i * s_r + xi_s[rows, cs]
            xr_s[rows, cs] = n_r
            xi_s[rows, cs] = n_i
            return n_r, n_i

        s_r, s_i = lax.fori_loop(0, tt, step, (sr_ref[:, cs], si_ref[:, cs]), unroll=min(tt, 8))
        sr_ref[:, cs] = s_r
        si_ref[:, cs] = s_i

    ys = []
    for k in range(CH_BLOCKS):
        ks = slice(k * STATE_BLOCK, (k + 1) * STATE_BLOCK)
        ys.append(jnp.dot(xr_s[:, ks].astype(BF16), wcr_ref[k], preferred_element_type=F32)
                  + jnp.dot(xi_s[:, ks].astype(BF16), wci_ref[k], preferred_element_type=F32))
    y = jnp.concatenate(ys, axis=1) + d_ref[...] * u
    act = jax.nn.gelu(y).astype(BF16)
    z = jnp.dot(act, gw_ref[...], preferred_element_type=F32) + gb_ref[...]
    o_ref[...] = h + z[:, :D_MODEL] * jax.nn.sigmoid(z[:, D_MODEL:])


def _s5(h, g, s0r, s0i, ar, ai, wb, wcr, wci, d_skip, glu_w, glu_b, *, rb, tt):
    n = h.shape[0]
    rows = rb * tt
    assert n % rows == 0 and rb % 8 == 0
    state = jax.ShapeDtypeStruct((rb, STATE_W), F32)
    return pl.pallas_call(
        functools.partial(_s5_kernel, rb=rb, tt=tt),
        grid=(n // rows,),
        in_specs=[pl.BlockSpec((rows, D_MODEL), lambda i: (i, 0)),
                  _const_spec((1, D_MODEL)),
                  _const_spec((rb, STATE_W)), _const_spec((rb, STATE_W)),
                  _const_spec((1, STATE_W)), _const_spec((1, STATE_W)),
                  _const_spec((CH_BLOCKS, LANES, 2 * STATE_BLOCK)),
                  _const_spec((CH_BLOCKS, STATE_BLOCK, LANES)),
                  _const_spec((CH_BLOCKS, STATE_BLOCK, LANES)),
                  _const_spec((1, D_MODEL)),
                  _const_spec((D_MODEL, 2 * D_MODEL)),
                  _const_spec((1, 2 * D_MODEL))],
        out_specs=(pl.BlockSpec((rows, D_MODEL), lambda i: (i, 0)),
                   pl.BlockSpec((rb, STATE_W), lambda i: (0, 0)),
                   pl.BlockSpec((rb, STATE_W), lambda i: (0, 0))),
        out_shape=(jax.ShapeDtypeStruct((n, D_MODEL), F32), state, state),
        scratch_shapes=[pltpu.VMEM((rows, STATE_W), F32), pltpu.VMEM((rows, STATE_W), F32)],
        compiler_params=_params("arbitrary"),
        name="s5",
    )(h, g.reshape(1, D_MODEL), s0r, s0i, ar, ai, wb, wcr, wci, d_skip.reshape(1, D_MODEL),
      glu_w, glu_b.reshape(1, 2 * D_MODEL))


def _proj_kernel(x_ref, g_ref, w_ref, o_ref, *, scale):
    xn = _rms(x_ref[...], g_ref[...]).astype(BF16)
    y = jnp.dot(xn, w_ref[...], preferred_element_type=F32)
    o_ref[...] = y if scale == 1.0 else y * scale


def _proj(x, g, w, scale=1.0, tm=512):
    n = x.shape[0]
    tm = min(tm, n)
    nout = w.shape[1]
    return pl.pallas_call(
        functools.partial(_proj_kernel, scale=scale),
        grid=(n // tm,),
        in_specs=[pl.BlockSpec((tm, D_MODEL), lambda i: (i, 0)),
                  _const_spec((1, D_MODEL)),
                  _const_spec((D_MODEL, nout))],
        out_specs=pl.BlockSpec((tm, nout), lambda i: (i, 0)),
        out_shape=jax.ShapeDtypeStruct((n, nout), F32),
        compiler_params=_params("parallel"),
        name="proj",
    )(x, g.reshape(1, D_MODEL), w)


def _oproj_kernel(a_ref, w_ref, h_ref, o_ref):
    o_ref[...] = h_ref[...] + jnp.dot(a_ref[...].astype(BF16), w_ref[...], preferred_element_type=F32)


def _oproj(a, w, h, tm=512):
    n = h.shape[0]
    tm = min(tm, n)
    return pl.pallas_call(
        _oproj_kernel,
        grid=(n // tm,),
        in_specs=[pl.BlockSpec((tm, ATTN_WIDTH), lambda i: (i, 0)),
                  _const_spec((ATTN_WIDTH, D_MODEL)),
                  pl.BlockSpec((tm, D_MODEL), lambda i: (i, 0))],
        out_specs=pl.BlockSpec((tm, D_MODEL), lambda i: (i, 0)),
        out_shape=jax.ShapeDtypeStruct((n, D_MODEL), F32),
        compiler_params=_params("parallel"),
        name="oproj",
    )(a, w, h)


def _alibi_slopes():
    n = N_DIL * HEADS_PER_GROUP
    s = jnp.exp2(-8.0 * jnp.arange(1, n + 1, dtype=F32) / n)
    return s.reshape(N_DIL, HEADS_PER_GROUP)


def _prompt_bias():
    qi = jnp.arange(R_LEN)[:, None]
    kj = jnp.arange(2 * R_LEN)[None, :]
    diff = R_LEN + qi - kj
    valid = (diff >= 0) & (diff <= R_LEN)
    dil = jnp.asarray(DIL_RATES, jnp.int32)[:, None, None]
    dist = (dil * diff[None]).astype(F32)
    bias = -_alibi_slopes()[:, :, None, None] * dist[:, None]
    return jnp.where(valid[None, None], bias, NEG)


def _sample_bias():
    kk = R_LEN - jnp.arange(R_LEN)
    dil = jnp.asarray(DIL_RATES, jnp.int32)[:, None]
    dist = (dil * kk[None]).astype(F32)
    return -_alibi_slopes()[:, :, None] * dist[:, None]


MERGE_ROWS = 256


def _attn_kernel(q0, q1, q2, k0, k1, k2, v0, v1, v2, bias_ref, o_ref, acc_s, m_s, l_s, *, t_len):
    qs, ks, vs = (q0, q1, q2), (k0, k1, k2), (v0, v1, v2)
    lo = lax.broadcasted_iota(jnp.int32, (R_LEN, LANES), 1) < HEAD_DIM

    def block(g, q_idx, k_idx, first):
        q = qs[g][q_idx, :]
        qb = jnp.concatenate([jnp.where(lo, q, 0.0), jnp.where(lo, 0.0, q)], axis=0).astype(BF16)
        kb = ks[g][k_idx, :].astype(BF16)
        vb = vs[g][k_idx, :].astype(BF16)
        s = lax.dot_general(qb, kb, (((1,), (1,)), ((), ())), preferred_element_type=F32)
        bias = bias_ref[g, :, :, R_LEN:] if first else bias_ref[g]
        s = s + bias.reshape(2 * R_LEN, bias.shape[-1])
        m = jnp.max(s, axis=-1, keepdims=True)
        p = jnp.exp(s - m)
        l = jnp.sum(p, axis=-1, keepdims=True)
        a = jnp.dot(p.astype(BF16), vb, preferred_element_type=F32)
        acc_s[g, q_idx, :] = jnp.where(lo, a[:R_LEN], a[R_LEN:])
        m_s[g, q_idx, :] = jnp.where(lo, m[:R_LEN], m[R_LEN:])
        l_s[g, q_idx, :] = jnp.where(lo, l[:R_LEN], l[R_LEN:])

    for g in range(N_DIL):
        dil = DIL_RATES[g]
        span = R_LEN * dil
        n_blocks = t_len // span

        def rows(start, n, dil=dil):
            if dil > 1:
                return pl.ds(start, n, stride=dil)
            return pl.ds(start if isinstance(start, int) else pl.multiple_of(start, R_LEN), n)

        def first_block(r, c, g=g, rows=rows):
            block(g, rows(r, R_LEN), rows(r, R_LEN), True)
            return c

        if n_blocks == 1:
            lax.fori_loop(0, dil, first_block, 0)
            continue
        for r in range(dil):
            first_block(r, 0)

            def later_block(nb, c, g=g, r=r, span=span, rows=rows):
                start = nb * span + r
                block(g, rows(start, R_LEN), rows(start - span, 2 * R_LEN), False)
                return c

            lax.fori_loop(1, n_blocks, later_block, 0)

    def merge(i, c):
        rows = pl.ds(pl.multiple_of(i * MERGE_ROWS, MERGE_ROWS), MERGE_ROWS)
        ms = [m_s[g, rows, :] for g in range(N_DIL)]
        mx = jnp.maximum(jnp.maximum(ms[0], ms[1]), ms[2])
        ws = [jnp.exp(m - mx) for m in ms]
        den = ws[0] * l_s[0, rows, :] + ws[1] * l_s[1, rows, :] + ws[2] * l_s[2, rows, :]
        num = ws[0] * acc_s[0, rows, :] + ws[1] * acc_s[1, rows, :] + ws[2] * acc_s[2, rows, :]
        o_ref[rows, :] = num / den
        return c

    lax.fori_loop(0, t_len // MERGE_ROWS, merge, 0)


def _attn_prompt(q, kv, bias):
    bn, t_len, _ = q.shape
    pairs = ATTN_WIDTH // LANES
    per_group = 2 * ATTN_WIDTH // LANES

    def col(off):
        return pl.BlockSpec((None, t_len, LANES), lambda b, hp: (b, 0, off + hp))

    in_specs = ([col(g * pairs) for g in range(N_DIL)]
                + [col(g * per_group) for g in range(N_DIL)]
                + [col(g * per_group + pairs) for g in range(N_DIL)]
                + [pl.BlockSpec((N_DIL, 2, R_LEN, 2 * R_LEN), lambda b, hp: (0, hp, 0, 0))])
    scratch = pltpu.VMEM((N_DIL, t_len, LANES), F32)
    return pl.pallas_call(
        functools.partial(_attn_kernel, t_len=t_len),
        grid=(bn, pairs),
        in_specs=in_specs,
        out_specs=pl.BlockSpec((None, t_len, LANES), lambda b, hp: (b, 0, hp)),
        out_shape=jax.ShapeDtypeStruct((bn, t_len, ATTN_WIDTH), F32),
        scratch_shapes=[scratch, scratch, scratch],
        compiler_params=_params("parallel", "parallel"),
        name="attn_prompt",
    )(q, q, q, kv, kv, kv, kv, kv, kv, bias)


SAMPLE_BLOCK = 8


def _attn_sample_kernel(q_ref, kvn_ref, c0, c1, c2, bias_ref, o_ref):
    caches = (c0, c1, c2)
    sub = lax.broadcasted_iota(jnp.int32, (HEADS_PER_GROUP, ATTN_WIDTH), 0)
    lane = lax.broadcasted_iota(jnp.int32, (HEADS_PER_GROUP, ATTN_WIDTH), 1)
    own = (lane // HEAD_DIM) == sub
    for b in range(SAMPLE_BLOCK):
        os_, ms, ls = [], [], []
        for g in range(N_DIL):
            base = g * 2 * ATTN_WIDTH
            qbd = jnp.where(own, q_ref[b:b + 1, g * ATTN_WIDTH:(g + 1) * ATTN_WIDTH], 0.0)
            kc = caches[g][b, :, :ATTN_WIDTH].astype(BF16)
            vc = caches[g][b, :, ATTN_WIDTH:].astype(BF16)
            kn = kvn_ref[b:b + 1, base:base + ATTN_WIDTH]
            vn = kvn_ref[b:b + 1, base + ATTN_WIDTH:base + 2 * ATTN_WIDTH]
            s = lax.dot_general(qbd.astype(BF16), kc, (((1,), (1,)), ((), ())),
                                preferred_element_type=F32) + bias_ref[g]
            s_new = jnp.sum(qbd * kn, axis=-1, keepdims=True)
            m = jnp.maximum(jnp.max(s, axis=-1, keepdims=True), s_new)
            p = jnp.exp(s - m)
            p_new = jnp.exp(s_new - m)
            ls.append(jnp.sum(p, axis=-1, keepdims=True) + p_new)
            os_.append(jnp.dot(p.astype(BF16), vc, preferred_element_type=F32) + p_new * vn)
            ms.append(m)
        mx = jnp.maximum(jnp.maximum(ms[0], ms[1]), ms[2])
        ws = [jnp.exp(m - mx) for m in ms]
        den = ws[0] * ls[0] + ws[1] * ls[1] + ws[2] * ls[2]
        o = (ws[0] * os_[0] + ws[1] * os_[1] + ws[2] * os_[2]) / den
        o_ref[b:b + 1, :] = jnp.sum(jnp.where(own, o, 0.0), axis=0, keepdims=True)


def _attn_sample(q, kv_new, caches, bias):
    bd = q.shape[0]
    assert bd % SAMPLE_BLOCK == 0
    strided, specs = [], []
    for g in range(N_DIL):
        dil = DIL_RATES[g]
        assert caches[g].shape[1] == R_LEN * dil
        strided.append(caches[g].reshape(bd, R_LEN, dil * 2 * ATTN_WIDTH))
        specs.append(pl.BlockSpec((SAMPLE_BLOCK, R_LEN, 2 * ATTN_WIDTH), lambda i: (i, 0, 0)))
    return pl.pallas_call(
        _attn_sample_kernel,
        grid=(bd // SAMPLE_BLOCK,),
        in_specs=[pl.BlockSpec((SAMPLE_BLOCK, N_DIL * ATTN_WIDTH), lambda i: (i, 0)),
                  pl.BlockSpec((SAMPLE_BLOCK, N_DIL * 2 * ATTN_WIDTH), lambda i: (i, 0))]
                 + specs + [_const_spec((N_DIL, HEADS_PER_GROUP, R_LEN))],
        out_specs=pl.BlockSpec((SAMPLE_BLOCK, ATTN_WIDTH), lambda i: (i, 0)),
        out_shape=jax.ShapeDtypeStruct((bd, ATTN_WIDTH), F32),
        compiler_params=_params("parallel"),
        name="attn_sample",
    )(q, kv_new, *strided, bias)


def kernel(x_prompt, x_sample, state_ssm_re, state_ssm_im, cache_kv_w128, cache_kv_w512, cache_kv_w2048, norm_ffn1, ffn1_w_in, ffn1_w_out, norm_mix, norm_ffn2, ffn2_w_in, ffn2_w_out, norm_kv, norm_final, ssm_log_dt, ssm_lambda_re, ssm_lambda_im, ssm_b_re, ssm_b_im, ssm_c_re, ssm_c_im, ssm_d, glu_w, glu_b, attn_w_q, attn_w_kv, attn_w_o):
    bn, t_len, _ = x_prompt.shape
    bd = x_sample.shape[0]
    assert x_sample.shape[1] == 1

    w1_in, w1_out = ffn1_w_in.astype(BF16), ffn1_w_out.astype(BF16)
    w2_in, w2_out = ffn2_w_in.astype(BF16), ffn2_w_out.astype(BF16)
    glu_wb = glu_w.astype(BF16)
    w_q, w_kv, w_o = attn_w_q.astype(BF16), attn_w_kv.astype(BF16), attn_w_o.astype(BF16)

    ssm = []
    for layer in range(N_A_LAYERS):
        ar, ai, bbr, bbi = _discretise(ssm_log_dt[layer], ssm_lambda_re[layer], ssm_lambda_im[layer],
                                       ssm_b_re[layer], ssm_b_im[layer])
        wb = jnp.concatenate([_block_diag_in(bbr), _block_diag_in(bbi)], axis=-1).astype(BF16)
        wcr = _block_diag_out(ssm_c_re[layer]).astype(BF16)
        wci = _block_diag_out(-ssm_c_im[layer]).astype(BF16)
        ssm.append((ar.reshape(1, STATE_W), ai.reshape(1, STATE_W), wb, wcr, wci,
                    ssm_d[layer].reshape(D_MODEL), glu_wb[layer], glu_b[layer]))

    def s5_layers(h, s0r, s0i, rb, tt):
        finals_r, finals_i = [], []
        for layer in range(N_A_LAYERS):
            h = _ffn(h, norm_ffn1[layer], w1_in[layer], w1_out[layer])
            h, sr, si = _s5(h, norm_mix[layer], s0r[layer], s0i[layer], *ssm[layer], rb=rb, tt=tt)
            finals_r.append(sr.reshape(rb, N_SSM_GROUPS, SSM_STATE))
            finals_i.append(si.reshape(rb, N_SSM_GROUPS, SSM_STATE))
            h = _ffn(h, norm_ffn2[layer], w2_in[layer], w2_out[layer])
        return h, jnp.stack(finals_r), jnp.stack(finals_i)

    def attn_layers(h, kv, attend):
        for layer in range(N_A_LAYERS, DEPTH):
            b = layer - N_A_LAYERS
            h = _ffn(h, norm_ffn1[layer], w1_in[layer], w1_out[layer])
            q = _proj(h, norm_mix[layer], w_q[b], scale=HEAD_DIM ** -0.5)
            h = _oproj(attend(q, kv), w_o[b], h)
            h = _ffn(h, norm_ffn2[layer], w2_in[layer], w2_out[layer],
                     final_g=norm_final if layer == DEPTH - 1 else None)
        return h

    zeros = jnp.zeros((N_A_LAYERS, bn, STATE_W), F32)
    h = jnp.swapaxes(x_prompt, 0, 1).reshape(t_len * bn, D_MODEL)
    h, ssm_re_p, ssm_im_p = s5_layers(h, zeros, zeros, rb=bn, tt=64)
    h = jnp.swapaxes(h.reshape(t_len, bn, D_MODEL), 0, 1).reshape(bn * t_len, D_MODEL)
    kv_p = _proj(h, norm_kv, w_kv)
    prompt_bias = _prompt_bias()
    h = attn_layers(h, kv_p, lambda q, kv: _attn_prompt(
        q.reshape(bn, t_len, -1), kv.reshape(bn, t_len, -1), prompt_bias).reshape(bn * t_len, ATTN_WIDTH))
    y_prompt = h.reshape(bn, t_len, D_MODEL)
    kv_p = kv_p.reshape(bn, t_len, N_DIL, 2, HEADS_PER_GROUP, HEAD_DIM)
    kv_out_p = [kv_p[:, t_len - min(w, t_len):, g] for g, w in enumerate(DIL_WINDOWS)]

    hs = x_sample.reshape(bd, D_MODEL)
    hs, ssm_re_s, ssm_im_s = s5_layers(hs, state_ssm_re.reshape(N_A_LAYERS, bd, STATE_W),
                                       state_ssm_im.reshape(N_A_LAYERS, bd, STATE_W), rb=bd, tt=1)
    kv_s = _proj(hs, norm_kv, w_kv)
    caches = (cache_kv_w128, cache_kv_w512, cache_kv_w2048)
    sample_bias = _sample_bias()
    hs = attn_layers(hs, kv_s, lambda q, kv: _attn_sample(q, kv, caches, sample_bias))
    y_sample = hs.reshape(bd, 1, D_MODEL)
    kv_s = kv_s.reshape(bd, 1, N_DIL, 2, HEADS_PER_GROUP, HEAD_DIM)

    return (y_prompt, y_sample, ssm_re_p, ssm_im_p, kv_out_p[0], kv_out_p[1], kv_out_p[2],
            ssm_re_s, ssm_im_s, kv_s[:, :, 0], kv_s[:, :, 1], kv_s[:, :, 2])
```

```python
import functools

import jax
import jax.numpy as jnp
from jax import lax
from jax.experimental import pallas as pl
from jax.experimental.pallas import tpu as pltpu

F32 = jnp.float32
BF16 = jnp.bfloat16

D_MODEL = 1024
DEPTH = 4
N_A_LAYERS = DEPTH // 2
SSM_GROUP = 16
N_SSM_GROUPS = D_MODEL // SSM_GROUP
SSM_STATE = 64
STATE_W = N_SSM_GROUPS * SSM_STATE
DIL_WINDOWS = (128, 512, 2048)
DIL_RATES = (1, 4, 16)
N_DIL = 3
HEADS_PER_GROUP = 8
HEAD_DIM = 64
ATTN_WIDTH = HEADS_PER_GROUP * HEAD_DIM
R_LEN = DIL_WINDOWS[0] // DIL_RATES[0]
D_FF = 2816
EPS = 1e-6
NEG = -1e30

LANES = 128
MXU_TILE = 256
CH_BLOCKS = D_MODEL // LANES
GROUPS_PER_BLOCK = LANES // SSM_GROUP
STATE_BLOCK = GROUPS_PER_BLOCK * SSM_STATE
VMEM_LIMIT_BYTES = 56 * 1024 * 1024

assert all(w // d == R_LEN for w, d in zip(DIL_WINDOWS, DIL_RATES))


def _params(*semantics):
    return pltpu.CompilerParams(dimension_semantics=semantics, vmem_limit_bytes=VMEM_LIMIT_BYTES)


def _const_spec(shape):
    zeros = (0,) * len(shape)
    return pl.BlockSpec(shape, lambda *_: zeros, pipeline_mode=pl.Buffered(1))


def _layer_spec(shape, layer):
    if layer is None:
        return _const_spec(shape)
    idx = (layer,) + (0,) * len(shape)
    return pl.BlockSpec((None,) + tuple(shape), lambda *_: idx, pipeline_mode=pl.Buffered(1))


def _rms(x, g):
    return x * lax.rsqrt(jnp.mean(x * x, axis=-1, keepdims=True) + EPS) * g


def _ff_chunks(n_chunks):
    tiles = D_FF // MXU_TILE
    assert tiles * MXU_TILE == D_FF and n_chunks <= tiles
    cuts = [MXU_TILE * ((tiles * c + n_chunks - 1) // n_chunks) for c in range(n_chunks + 1)]
    return list(zip(cuts[:-1], cuts[1:]))


def _ffn_kernel(x_ref, g_ref, win_ref, wout_ref, *rest, n_chunks, final):
    if final:
        gf_ref, o_ref = rest
    else:
        (o_ref,) = rest
    x = x_ref[...]
    xn = _rms(x, g_ref[...]).astype(BF16)
    acc = None
    for lo, hi in _ff_chunks(n_chunks):
        a = jnp.dot(xn, win_ref[:, lo:hi], preferred_element_type=F32)
        b = jnp.dot(xn, win_ref[:, D_FF + lo:D_FF + hi], preferred_element_type=F32)
        act = (a * jax.nn.sigmoid(a) * b).astype(BF16)
        d = jnp.dot(act, wout_ref[lo:hi, :], preferred_element_type=F32)
        acc = d if acc is None else acc + d
    y = x + 0.5 * acc
    if final:
        y = _rms(y, gf_ref[...])
    o_ref[...] = y


def _ffn(x, g, w_in, w_out, layer, final_g=None, tm=512, n_chunks=2):
    n = x.shape[0]
    tm = min(tm, n)
    assert n % tm == 0
    final = final_g is not None
    in_specs = [pl.BlockSpec((tm, D_MODEL), lambda i: (i, 0)),
                _const_spec((1, D_MODEL)),
                _layer_spec((D_MODEL, 2 * D_FF), layer),
                _layer_spec((D_FF, D_MODEL), layer)]
    args = [x, g.reshape(1, D_MODEL), w_in, w_out]
    if final:
        in_specs.append(_const_spec((1, D_MODEL)))
        args.append(final_g.reshape(1, D_MODEL))
    return pl.pallas_call(
        functools.partial(_ffn_kernel, n_chunks=n_chunks, final=final),
        grid=(n // tm,),
        in_specs=in_specs,
        out_specs=pl.BlockSpec((tm, D_MODEL), lambda i: (i, 0)),
        out_shape=jax.ShapeDtypeStruct((n, D_MODEL), F32),
        compiler_params=_params("parallel"),
        name="ffn",
    )(*args)


def _disc_kernel(ldt_ref, lr_ref, li_ref, br_ref, bi_ref, ar_ref, ai_ref, bbr_ref, bbi_ref):
    dt = jnp.exp(ldt_ref[...])
    lr = lr_ref[...]
    li = li_ref[...]
    mag = jnp.exp(lr * dt)
    ar = mag * jnp.cos(li * dt)
    ai = mag * jnp.sin(li * dt)
    den = lr * lr + li * li
    cr = ((ar - 1.0) * lr + ai * li) / den
    ci = (ai * lr - (ar - 1.0) * li) / den
    ar_ref[...] = ar
    ai_ref[...] = ai
    br = br_ref[...]
    bi = bi_ref[...]
    crb = cr[:, None, :]
    cib = ci[:, None, :]
    bbr_ref[...] = crb * br - cib * bi
    bbi_ref[...] = crb * bi + cib * br


def _discretise(log_dt, lam_re, lam_im, b_re, b_im):
    g, p, hg = N_SSM_GROUPS, SSM_STATE, SSM_GROUP
    gp = jax.ShapeDtypeStruct((g, p), F32)
    ghp = jax.ShapeDtypeStruct((g, hg, p), F32)
    return pl.pallas_call(
        _disc_kernel,
        out_shape=(gp, gp, ghp, ghp),
        name="s5_disc",
    )(log_dt.reshape(g, 1), lam_re, lam_im, jnp.swapaxes(b_re, 1, 2), jnp.swapaxes(b_im, 1, 2))


def _block_diag_in(bb):
    x = bb.reshape(CH_BLOCKS, GROUPS_PER_BLOCK, SSM_GROUP, SSM_STATE)
    eye = jnp.eye(GROUPS_PER_BLOCK, dtype=bb.dtype)
    return jnp.einsum('kghp,gj->kghjp', x, eye).reshape(CH_BLOCKS, LANES, STATE_BLOCK)


def _block_diag_out(c):
    x = c.reshape(CH_BLOCKS, GROUPS_PER_BLOCK, SSM_GROUP, SSM_STATE)
    eye = jnp.eye(GROUPS_PER_BLOCK, dtype=c.dtype)
    return jnp.einsum('kghp,gj->kgpjh', x, eye).reshape(CH_BLOCKS, STATE_BLOCK, LANES)


SCAN_COLS = 512


def _s5_kernel(h_ref, g_ref, s0r_ref, s0i_ref, ar_ref, ai_ref, wb_ref, wcr_ref, wci_ref, d_ref,
               gw_ref, gb_ref, o_ref, sr_ref, si_ref, xr_s, xi_s, *, rb, tt):
    @pl.when(pl.program_id(0) == 0)
    def _():
        sr_ref[...] = s0r_ref[...]
        si_ref[...] = s0i_ref[...]

    h = h_ref[...]
    u = _rms(h, g_ref[...])
    ub = u.astype(BF16)
    for k in range(CH_BLOCKS):
        bu = jnp.dot(ub[:, k * LANES:(k + 1) * LANES], wb_ref[k], preferred_element_type=F32)
        xr_s[:, k * STATE_BLOCK:(k + 1) * STATE_BLOCK] = bu[:, :STATE_BLOCK]
        xi_s[:, k * STATE_BLOCK:(k + 1) * STATE_BLOCK] = bu[:, STATE_BLOCK:]

    for c in range(STATE_W // SCAN_COLS):
        cs = slice(c * SCAN_COLS, (c + 1) * SCAN_COLS)
        a_r = jnp.broadcast_to(ar_ref[:, cs], (rb, SCAN_COLS))
        a_i = jnp.broadcast_to(ai_ref[:, cs], (rb, SCAN_COLS))

        def step(t, carry, cs=cs, a_r=a_r, a_i=a_i):
            s_r, s_i = carry
            rows = pl.ds(pl.multiple_of(t * rb, rb), rb)
            n_r = a_r * s_r - a_i * s_i + xr_s[rows, cs]
            n_i = a_r * s_i + a_i * s_r + xi_s[rows, cs]
            xr_s[rows, cs] = n_r
            xi_s[rows, cs] = n_i
            return n_r, n_i

        s_r, s_i = lax.fori_loop(0, tt, step, (sr_ref[:, cs], si_ref[:, cs]), unroll=min(tt, 8))
        sr_ref[:, cs] = s_r
        si_ref[:, cs] = s_i

    ys = []
    for k in range(CH_BLOCKS):
        ks = slice(k * STATE_BLOCK, (k + 1) * STATE_BLOCK)
        ys.append(jnp.dot(xr_s[:, ks].astype(BF16), wcr_ref[k], preferred_element_type=F32)
                  + jnp.dot(xi_s[:, ks].astype(BF16), wci_ref[k], preferred_element_type=F32))
    y = jnp.concatenate(ys, axis=1) + d_ref[...] * u
    act = jax.nn.gelu(y).astype(BF16)
    z = jnp.dot(act, gw_ref[...], preferred_element_type=F32) + gb_ref[...]
    o_ref[...] = h + z[:, :D_MODEL] * jax.nn.sigmoid(z[:, D_MODEL:])


def _s5(h, g, s0r, s0i, ar, ai, wb, wcr, wci, d_skip, glu_w, glu_b, *, layer, rb, tt):
    n = h.shape[0]
    rows = rb * tt
    assert n % rows == 0 and rb % 8 == 0
    state = jax.ShapeDtypeStruct((rb, STATE_W), F32)
    return pl.pallas_call(
        functools.partial(_s5_kernel, rb=rb, tt=tt),
        grid=(n // rows,),
        in_specs=[pl.BlockSpec((rows, D_MODEL), lambda i: (i, 0)),
                  _const_spec((1, D_MODEL)),
                  _const_spec((rb, STATE_W)), _const_spec((rb, STATE_W)),
                  _const_spec((1, STATE_W)), _const_spec((1, STATE_W)),
                  _const_spec((CH_BLOCKS, LANES, 2 * STATE_BLOCK)),
                  _const_spec((CH_BLOCKS, STATE_BLOCK, LANES)),
                  _const_spec((CH_BLOCKS, STATE_BLOCK, LANES)),
                  _const_spec((1, D_MODEL)),
                  _layer_spec((D_MODEL, 2 * D_MODEL), layer),
                  _const_spec((1, 2 * D_MODEL))],
        out_specs=(pl.BlockSpec((rows, D_MODEL), lambda i: (i, 0)),
                   pl.BlockSpec((rb, STATE_W), lambda i: (0, 0)),
                   pl.BlockSpec((rb, STATE_W), lambda i: (0, 0))),
        out_shape=(jax.ShapeDtypeStruct((n, D_MODEL), F32), state, state),
        scratch_shapes=[pltpu.VMEM((rows, STATE_W), F32), pltpu.VMEM((rows, STATE_W), F32)],
        compiler_params=_params("arbitrary"),
        name="s5",
    )(h, g.reshape(1, D_MODEL), s0r, s0i, ar, ai, wb, wcr, wci, d_skip.reshape(1, D_MODEL),
      glu_w, glu_b.reshape(1, 2 * D_MODEL))


def _proj_kernel(x_ref, g_ref, w_ref, o_ref, *, scale):
    xn = _rms(x_ref[...], g_ref[...]).astype(BF16)
    y = jnp.dot(xn, w_ref[...], preferred_element_type=F32)
    o_ref[...] = y if scale == 1.0 else y * scale


def _proj(x, g, w, layer=None, scale=1.0, tm=512):
    n = x.shape[0]
    tm = min(tm, n)
    nout = w.shape[-1]
    return pl.pallas_call(
        functools.partial(_proj_kernel, scale=scale),
        grid=(n // tm,),
        in_specs=[pl.BlockSpec((tm, D_MODEL), lambda i: (i, 0)),
                  _const_spec((1, D_MODEL)),
                  _layer_spec((D_MODEL, nout), layer)],
        out_specs=pl.BlockSpec((tm, nout), lambda i: (i, 0)),
        out_shape=jax.ShapeDtypeStruct((n, nout), F32),
        compiler_params=_params("parallel"),
        name="proj",
    )(x, g.reshape(1, D_MODEL), w)


def _oproj_kernel(a_ref, w_ref, h_ref, o_ref):
    o_ref[...] = h_ref[...] + jnp.dot(a_ref[...].astype(BF16), w_ref[...], preferred_element_type=F32)


def _oproj(a, w, layer, h, tm=512):
    n = h.shape[0]
    tm = min(tm, n)
    return pl.pallas_call(
        _oproj_kernel,
        grid=(n // tm,),
        in_specs=[pl.BlockSpec((tm, ATTN_WIDTH), lambda i: (i, 0)),
                  _layer_spec((ATTN_WIDTH, D_MODEL), layer),
                  pl.BlockSpec((tm, D_MODEL), lambda i: (i, 0))],
        out_specs=pl.BlockSpec((tm, D_MODEL), lambda i: (i, 0)),
        out_shape=jax.ShapeDtypeStruct((n, D_MODEL), F32),
        compiler_params=_params("parallel"),
        name="oproj",
    )(a, w, h)


def _alibi_slopes():
    n = N_DIL * HEADS_PER_GROUP
    s = jnp.exp2(-8.0 * jnp.arange(1, n + 1, dtype=F32) / n)
    return s.reshape(N_DIL, HEADS_PER_GROUP)


def _prompt_bias():
    qi = jnp.arange(R_LEN)[:, None]
    kj = jnp.arange(2 * R_LEN)[None, :]
    diff = R_LEN + qi - kj
    valid = (diff >= 0) & (diff <= R_LEN)
    dil = jnp.asarray(DIL_RATES, jnp.int32)[:, None, None]
    dist = (dil * diff[None]).astype(F32)
    bias = -_alibi_slopes()[:, :, None, None] * dist[:, None]
    return jnp.where(valid[None, None], bias, NEG)


def _sample_bias():
    kk = R_LEN - jnp.arange(R_LEN)
    dil = jnp.asarray(DIL_RATES, jnp.int32)[:, None]
    dist = (dil * kk[None]).astype(F32)
    return -_alibi_slopes()[:, :, None] * dist[:, None]


MERGE_ROWS = 256
INTERLEAVE = 8


def _attn_kernel(q0, q1, q2, k0, k1, k2, v0, v1, v2, bias_ref, o_ref, acc_s, m_s, l_s, *, t_len):
    qs, ks, vs = (q0, q1, q2), (k0, k1, k2), (v0, v1, v2)
    lo = lax.broadcasted_iota(jnp.int32, (R_LEN, LANES), 1) < HEAD_DIM

    def block(g, q_idx, k_idx, first):
        q = qs[g][q_idx, :]
        qb = jnp.concatenate([jnp.where(lo, q, 0.0), jnp.where(lo, 0.0, q)], axis=0).astype(BF16)
        kb = ks[g][k_idx, :].astype(BF16)
        vb = vs[g][k_idx, :].astype(BF16)
        s = lax.dot_general(qb, kb, (((1,), (1,)), ((), ())), preferred_element_type=F32)
        bias = bias_ref[g, :, :, R_LEN:] if first else bias_ref[g]
        s = s + bias.reshape(2 * R_LEN, bias.shape[-1])
        m = jnp.max(s, axis=-1, keepdims=True)
        p = jnp.exp(s - m)
        l = jnp.sum(p, axis=-1, keepdims=True)
        a = jnp.dot(p.astype(BF16), vb, preferred_element_type=F32)
        acc_s[g, q_idx, :] = jnp.where(lo, a[:R_LEN], a[R_LEN:])
        m_s[g, q_idx, :] = jnp.where(lo, m[:R_LEN], m[R_LEN:])
        l_s[g, q_idx, :] = jnp.where(lo, l[:R_LEN], l[R_LEN:])

    for g in range(N_DIL):
        dil = DIL_RATES[g]
        span = R_LEN * dil
        n_blocks = t_len // span

        def rows(start, n, dil=dil):
            if dil > 1:
                return pl.ds(start, n, stride=dil)
            return pl.ds(start if isinstance(start, int) else pl.multiple_of(start, R_LEN), n)

        if dil <= INTERLEAVE:
            for r in range(dil):
                block(g, rows(r, R_LEN), rows(r, R_LEN), True)
        else:
            def first_blocks(i, c, g=g, rows=rows):
                for j in range(INTERLEAVE):
                    r = i * INTERLEAVE + j
                    block(g, rows(r, R_LEN), rows(r, R_LEN), True)
                return c

            lax.fori_loop(0, dil // INTERLEAVE, first_blocks, 0)

        later = n_blocks - 1
        if later == 0:
            continue
        per_trip = max(w for w in range(1, INTERLEAVE + 1) if later % w == 0) if dil == 1 else 1

        def later_blocks(i, c, g=g, dil=dil, span=span, rows=rows, per_trip=per_trip):
            for j in range(per_trip):
                nb = 1 + i * per_trip + j
                for r in range(dil):
                    start = nb * span + r
                    block(g, rows(start, R_LEN), rows(start - span, 2 * R_LEN), False)
            return c

        lax.fori_loop(0, later // per_trip, later_blocks, 0)

    def merge(i, c):
        rows = pl.ds(pl.multiple_of(i * MERGE_ROWS, MERGE_ROWS), MERGE_ROWS)
        ms = [m_s[g, rows, :] for g in range(N_DIL)]
        mx = jnp.maximum(jnp.maximum(ms[0], ms[1]), ms[2])
        ws = [jnp.exp(m - mx) for m in ms]
        den = ws[0] * l_s[0, rows, :] + ws[1] * l_s[1, rows, :] + ws[2] * l_s[2, rows, :]
        num = ws[0] * acc_s[0, rows, :] + ws[1] * acc_s[1, rows, :] + ws[2] * acc_s[2, rows, :]
        o_ref[rows, :] = num / den
        return c

    lax.fori_loop(0, t_len // MERGE_ROWS, merge, 0)


def _attn_prompt(q, kv, bias):
    bn, t_len, _ = q.shape
    pairs = ATTN_WIDTH // LANES
    per_group = 2 * ATTN_WIDTH // LANES

    def col(off):
        return pl.BlockSpec((None, t_len, LANES), lambda b, hp: (b, 0, off + hp))

    in_specs = ([col(g * pairs) for g in range(N_DIL)]
                + [col(g * per_group) for g in range(N_DIL)]
                + [col(g * per_group + pairs) for g in range(N_DIL)]
                + [pl.BlockSpec((N_DIL, 2, R_LEN, 2 * R_LEN), lambda b, hp: (0, hp, 0, 0))])
    scratch = pltpu.VMEM((N_DIL, t_len, LANES), F32)
    return pl.pallas_call(
        functools.partial(_attn_kernel, t_len=t_len),
        grid=(bn, pairs),
        in_specs=in_specs,
        out_specs=pl.BlockSpec((None, t_len, LANES), lambda b, hp: (b, 0, hp)),
        out_shape=jax.ShapeDtypeStruct((bn, t_len, ATTN_WIDTH), F32),
        scratch_shapes=[scratch, scratch, scratch],
        compiler_params=_params("parallel", "parallel"),
        name="attn_prompt",
    )(q, q, q, kv, kv, kv, kv, kv, kv, bias)


SAMPLE_BLOCK = 8


def _attn_sample_kernel(q_ref, kvn_ref, c0, c1, c2, bias_ref, o_ref):
    caches = (c0, c1, c2)
    sub = lax.broadcasted_iota(jnp.int32, (HEADS_PER_GROUP, ATTN_WIDTH), 0)
    lane = lax.broadcasted_iota(jnp.int32, (HEADS_PER_GROUP, ATTN_WIDTH), 1)
    own = (lane // HEAD_DIM) == sub
    for b in range(SAMPLE_BLOCK):
        os_, ms, ls = [], [], []
        for g in range(N_DIL):
            base = g * 2 * ATTN_WIDTH
            qbd = jnp.where(own, q_ref[b:b + 1, g * ATTN_WIDTH:(g + 1) * ATTN_WIDTH], 0.0)
            kc = caches[g][b, :, :ATTN_WIDTH].astype(BF16)
            vc = caches[g][b, :, ATTN_WIDTH:].astype(BF16)
            kn = kvn_ref[b:b + 1, base:base + ATTN_WIDTH]
            vn = kvn_ref[b:b + 1, base + ATTN_WIDTH:base + 2 * ATTN_WIDTH]
            s = lax.dot_general(qbd.astype(BF16), kc, (((1,), (1,)), ((), ())),
                                preferred_element_type=F32) + bias_ref[g]
            s_new = jnp.sum(qbd * kn, axis=-1, keepdims=True)
            m = jnp.maximum(jnp.max(s, axis=-1, keepdims=True), s_new)
            p = jnp.exp(s - m)
            p_new = jnp.exp(s_new - m)
            ls.append(jnp.sum(p, axis=-1, keepdims=True) + p_new)
            os_.append(jnp.dot(p.astype(BF16), vc, preferred_element_type=F32) + p_new * vn)
            ms.append(m)
        mx = jnp.maximum(jnp.maximum(ms[0], ms[1]), ms[2])
        ws = [jnp.exp(m - mx) for m in ms]
        den = ws[0] * ls[0] + ws[1] * ls[1] + ws[2] * ls[2]
        o = (ws[0] * os_[0] + ws[1] * os_[1] + ws[2] * os_[2]) / den
        o_ref[b:b + 1, :] = jnp.sum(jnp.where(own, o, 0.0), axis=0, keepdims=True)


def _attn_sample(q, kv_new, caches, bias):
    bd = q.shape[0]
    assert bd % SAMPLE_BLOCK == 0
    strided, specs = [], []
    for g in range(N_DIL):
        dil = DIL_RATES[g]
        assert caches[g].shape[1] == R_LEN * dil
        strided.append(caches[g].reshape(bd, R_LEN, dil * 2 * ATTN_WIDTH))
        specs.append(pl.BlockSpec((SAMPLE_BLOCK, R_LEN, 2 * ATTN_WIDTH), lambda i: (i, 0, 0)))
    return pl.pallas_call(
        _attn_sample_kernel,
        grid=(bd // SAMPLE_BLOCK,),
        in_specs=[pl.BlockSpec((SAMPLE_BLOCK, N_DIL * ATTN_WIDTH), lambda i: (i, 0)),
                  pl.BlockSpec((SAMPLE_BLOCK, N_DIL * 2 * ATTN_WIDTH), lambda i: (i, 0))]
                 + specs + [_const_spec((N_DIL, HEADS_PER_GROUP, R_LEN))],
        out_specs=pl.BlockSpec((SAMPLE_BLOCK, ATTN_WIDTH), lambda i: (i, 0)),
        out_shape=jax.ShapeDtypeStruct((bd, ATTN_WIDTH), F32),
        compiler_params=_params("parallel"),
        name="attn_sample",
    )(q, kv_new, *strided, bias)


def kernel(x_prompt, x_sample, state_ssm_re, state_ssm_im, cache_kv_w128, cache_kv_w512, cache_kv_w2048, norm_ffn1, ffn1_w_in, ffn1_w_out, norm_mix, norm_ffn2, ffn2_w_in, ffn2_w_out, norm_kv, norm_final, ssm_log_dt, ssm_lambda_re, ssm_lambda_im, ssm_b_re, ssm_b_im, ssm_c_re, ssm_c_im, ssm_d, glu_w, glu_b, attn_w_q, attn_w_kv, attn_w_o):
    bn, t_len, _ = x_prompt.shape
    bd = x_sample.shape[0]
    assert x_sample.shape[1] == 1

    w1_in, w1_out = ffn1_w_in.astype(BF16), ffn1_w_out.astype(BF16)
    w2_in, w2_out = ffn2_w_in.astype(BF16), ffn2_w_out.astype(BF16)
    glu_wb = glu_w.astype(BF16)
    w_q, w_kv, w_o = attn_w_q.astype(BF16), attn_w_kv.astype(BF16), attn_w_o.astype(BF16)

    ssm = []
    for layer in range(N_A_LAYERS):
        ar, ai, bbr, bbi = _discretise(ssm_log_dt[layer], ssm_lambda_re[layer], ssm_lambda_im[layer],
                                       ssm_b_re[layer], ssm_b_im[layer])
        wb = jnp.concatenate([_block_diag_in(bbr), _block_diag_in(bbi)], axis=-1).astype(BF16)
        wcr = _block_diag_out(ssm_c_re[layer]).astype(BF16)
        wci = _block_diag_out(-ssm_c_im[layer]).astype(BF16)
        ssm.append((ar.reshape(1, STATE_W), ai.reshape(1, STATE_W), wb, wcr, wci,
                    ssm_d[layer].reshape(D_MODEL), glu_wb, glu_b[layer]))

    def s5_layers(h, s0r, s0i, rb, tt):
        finals_r, finals_i = [], []
        for layer in range(N_A_LAYERS):
            h = _ffn(h, norm_ffn1[layer], w1_in, w1_out, layer)
            h, sr, si = _s5(h, norm_mix[layer], s0r[layer], s0i[layer], *ssm[layer], layer=layer, rb=rb, tt=tt)
            finals_r.append(sr.reshape(rb, N_SSM_GROUPS, SSM_STATE))
            finals_i.append(si.reshape(rb, N_SSM_GROUPS, SSM_STATE))
            h = _ffn(h, norm_ffn2[layer], w2_in, w2_out, layer)
        return h, jnp.stack(finals_r), jnp.stack(finals_i)

    def attn_layers(h, kv, attend):
        for layer in range(N_A_LAYERS, DEPTH):
            b = layer - N_A_LAYERS
            h = _ffn(h, norm_ffn1[layer], w1_in, w1_out, layer)
            q = _proj(h, norm_mix[layer], w_q, layer=b, scale=HEAD_DIM ** -0.5)
            h = _oproj(attend(q, kv), w_o, b, h)
            h = _ffn(h, norm_ffn2[layer], w2_in, w2_out, layer,
                     final_g=norm_final if layer == DEPTH - 1 else None)
        return h

    zeros = jnp.zeros((N_A_LAYERS, bn, STATE_W), F32)
    h = jnp.swapaxes(x_prompt, 0, 1).reshape(t_len * bn, D_MODEL)
    h, ssm_re_p, ssm_im_p = s5_layers(h, zeros, zeros, rb=bn, tt=64)
    h = jnp.swapaxes(h.reshape(t_len, bn, D_MODEL), 0, 1).reshape(bn * t_len, D_MODEL)
    kv_p = _proj(h, norm_kv, w_kv)
    prompt_bias = _prompt_bias()
    h = attn_layers(h, kv_p, lambda q, kv: _attn_prompt(
        q.reshape(bn, t_len, -1), kv.reshape(bn, t_len, -1), prompt_bias).reshape(bn * t_len, ATTN_WIDTH))
    y_prompt = h.reshape(bn, t_len, D_MODEL)
    kv_p = kv_p.reshape(bn, t_len, N_DIL, 2, HEADS_PER_GROUP, HEAD_DIM)
    kv_out_p = [kv_p[:, t_len - min(w, t_len):, g] for g, w in enumerate(DIL_WINDOWS)]

    hs = x_sample.reshape(bd, D_MODEL)
    hs, ssm_re_s, ssm_im_s = s5_layers(hs, state_ssm_re.reshape(N_A_LAYERS, bd, STATE_W),
                                       state_ssm_im.reshape(N_A_LAYERS, bd, STATE_W), rb=bd, tt=1)
    kv_s = _proj(hs, norm_kv, w_kv)
    caches = (cache_kv_w128, cache_kv_w512, cache_kv_w2048)
    sample_bias = _sample_bias()
    hs = attn_layers(hs, kv_s, lambda q, kv: _attn_sample(q, kv, caches, sample_bias))
    y_sample = hs.reshape(bd, 1, D_MODEL)
    kv_s = kv_s.reshape(bd, 1, N_DIL, 2, HEADS_PER_GROUP, HEAD_DIM)

    return (y_prompt, y_sample, ssm_re_p, ssm_im_p, kv_out_p[0], kv_out_p[1], kv_out_p[2],
            ssm_re_s, ssm_im_s, kv_s[:, :, 0], kv_s[:, :, 1], kv_s[:, :, 2])
```

```python
import functools

import jax
import jax.numpy as jnp
from jax import lax
from jax.experimental import pallas as pl
from jax.experimental.pallas import tpu as pltpu

F32 = jnp.float32
BF16 = jnp.bfloat16

D_MODEL = 1024
DEPTH = 4
N_A_LAYERS = DEPTH // 2
SSM_GROUP = 16
N_SSM_GROUPS = D_MODEL // SSM_GROUP
SSM_STATE = 64
STATE_W = N_SSM_GROUPS * SSM_STATE
DIL_WINDOWS = (128, 512, 2048)
DIL_RATES = (1, 4, 16)
N_DIL = 3
HEADS_PER_GROUP = 8
HEAD_DIM = 64
ATTN_WIDTH = HEADS_PER_GROUP * HEAD_DIM
R_LEN = DIL_WINDOWS[0] // DIL_RATES[0]
D_FF = 2816
EPS = 1e-6
NEG = -1e30

LANES = 128
MXU_TILE = 256
CH_BLOCKS = D_MODEL // LANES
GROUPS_PER_BLOCK = LANES // SSM_GROUP
STATE_BLOCK = GROUPS_PER_BLOCK * SSM_STATE
VMEM_LIMIT_BYTES = 56 * 1024 * 1024

assert all(w // d == R_LEN for w, d in zip(DIL_WINDOWS, DIL_RATES))


def _params(*semantics):
    return pltpu.CompilerParams(dimension_semantics=semantics, vmem_limit_bytes=VMEM_LIMIT_BYTES)


def _const_spec(shape):
    zeros = (0,) * len(shape)
    return pl.BlockSpec(shape, lambda *_: zeros, pipeline_mode=pl.Buffered(1))


def _layer_spec(shape, layer):
    if layer is None:
        return _const_spec(shape)
    idx = (layer,) + (0,) * len(shape)
    return pl.BlockSpec((None,) + tuple(shape), lambda *_: idx, pipeline_mode=pl.Buffered(1))


def _rms(x, g):
    return x * lax.rsqrt(jnp.mean(x * x, axis=-1, keepdims=True) + EPS) * g


def _ff_chunks(n_chunks):
    tiles = D_FF // MXU_TILE
    assert tiles * MXU_TILE == D_FF and n_chunks <= tiles
    cuts = [MXU_TILE * ((tiles * c + n_chunks - 1) // n_chunks) for c in range(n_chunks + 1)]
    return list(zip(cuts[:-1], cuts[1:]))


def _ffn_kernel(x_ref, g_ref, win_ref, wout_ref, *rest, n_chunks, final):
    if final:
        gf_ref, o_ref = rest
    else:
        (o_ref,) = rest
    x = x_ref[...]
    xn = _rms(x, g_ref[...]).astype(BF16)
    acc = None
    for lo, hi in _ff_chunks(n_chunks):
        a = jnp.dot(xn, win_ref[:, lo:hi], preferred_element_type=F32)
        b = jnp.dot(xn, win_ref[:, D_FF + lo:D_FF + hi], preferred_element_type=F32)
        act = (a * jax.nn.sigmoid(a) * b).astype(BF16)
        d = jnp.dot(act, wout_ref[lo:hi, :], preferred_element_type=F32)
        acc = d if acc is None else acc + d
    y = x + 0.5 * acc
    if final:
        y = _rms(y, gf_ref[...])
    o_ref[...] = y


def _ffn(x, g, w_in, w_out, layer, final_g=None, tm=512, n_chunks=2):
    n = x.shape[0]
    tm = min(tm, n)
    assert n % tm == 0
    final = final_g is not None
    in_specs = [pl.BlockSpec((tm, D_MODEL), lambda i: (i, 0)),
                _const_spec((1, D_MODEL)),
                _layer_spec((D_MODEL, 2 * D_FF), layer),
                _layer_spec((D_FF, D_MODEL), layer)]
    args = [x, g.reshape(1, D_MODEL), w_in, w_out]
    if final:
        in_specs.append(_const_spec((1, D_MODEL)))
        args.append(final_g.reshape(1, D_MODEL))
    return pl.pallas_call(
        functools.partial(_ffn_kernel, n_chunks=n_chunks, final=final),
        grid=(n // tm,),
        in_specs=in_specs,
        out_specs=pl.BlockSpec((tm, D_MODEL), lambda i: (i, 0)),
        out_shape=jax.ShapeDtypeStruct((n, D_MODEL), F32),
        compiler_params=_params("parallel"),
        name="ffn",
    )(*args)


def _disc_kernel(ldt_ref, lr_ref, li_ref, br_ref, bi_ref, ar_ref, ai_ref, bbr_ref, bbi_ref):
    dt = jnp.exp(ldt_ref[...])
    lr = lr_ref[...]
    li = li_ref[...]
    mag = jnp.exp(lr * dt)
    ar = mag * jnp.cos(li * dt)
    ai = mag * jnp.sin(li * dt)
    den = lr * lr + li * li
    cr = ((ar - 1.0) * lr + ai * li) / den
    ci = (ai * lr - (ar - 1.0) * li) / den
    ar_ref[...] = ar
    ai_ref[...] = ai
    br = br_ref[...]
    bi = bi_ref[...]
    crb = cr[:, None, :]
    cib = ci[:, None, :]
    bbr_ref[...] = crb * br - cib * bi
    bbi_ref[...] = crb * bi + cib * br


def _discretise(log_dt, lam_re, lam_im, b_re, b_im):
    g, p, hg = N_SSM_GROUPS, SSM_STATE, SSM_GROUP
    gp = jax.ShapeDtypeStruct((g, p), F32)
    ghp = jax.ShapeDtypeStruct((g, hg, p), F32)
    return pl.pallas_call(
        _disc_kernel,
        out_shape=(gp, gp, ghp, ghp),
        name="s5_disc",
    )(log_dt.reshape(g, 1), lam_re, lam_im, jnp.swapaxes(b_re, 1, 2), jnp.swapaxes(b_im, 1, 2))


def _block_diag_in(bb):
    x = bb.reshape(CH_BLOCKS, GROUPS_PER_BLOCK, SSM_GROUP, SSM_STATE)
    eye = jnp.eye(GROUPS_PER_BLOCK, dtype=bb.dtype)
    return jnp.einsum('kghp,gj->kghjp', x, eye).reshape(CH_BLOCKS, LANES, STATE_BLOCK)


def _block_diag_out(c):
    x = c.reshape(CH_BLOCKS, GROUPS_PER_BLOCK, SSM_GROUP, SSM_STATE)
    eye = jnp.eye(GROUPS_PER_BLOCK, dtype=c.dtype)
    return jnp.einsum('kghp,gj->kgpjh', x, eye).reshape(CH_BLOCKS, STATE_BLOCK, LANES)


SCAN_COLS = 512


def _s5_kernel(h_ref, g_ref, s0r_ref, s0i_ref, ar_ref, ai_ref, wb_ref, wcr_ref, wci_ref, d_ref,
               gw_ref, gb_ref, o_ref, sr_ref, si_ref, u_s, xr_s, xi_s, *, rb, tt):
    n_slab, per_slab = h_ref.shape[:2]
    assert n_slab * per_slab == rb * tt and (n_slab == rb or tt == 1)

    def seq_rows(s):
        return pl.ds(s, per_slab, stride=n_slab) if n_slab > 1 else pl.ds(0, per_slab)

    @pl.when(pl.program_id(0) == 0)
    def _():
        sr_ref[...] = s0r_ref[...]
        si_ref[...] = s0i_ref[...]

    for s in range(n_slab):
        un = _rms(h_ref[s], g_ref[...])
        for k in range(CH_BLOCKS):
            u_s[k, seq_rows(s), :] = un[:, k * LANES:(k + 1) * LANES]
    for k in range(CH_BLOCKS):
        bu = jnp.dot(u_s[k].astype(BF16), wb_ref[k], preferred_element_type=F32)
        xr_s[:, k * STATE_BLOCK:(k + 1) * STATE_BLOCK] = bu[:, :STATE_BLOCK]
        xi_s[:, k * STATE_BLOCK:(k + 1) * STATE_BLOCK] = bu[:, STATE_BLOCK:]

    for c in range(STATE_W // SCAN_COLS):
        cs = slice(c * SCAN_COLS, (c + 1) * SCAN_COLS)
        a_r = jnp.broadcast_to(ar_ref[:, cs], (rb, SCAN_COLS))
        a_i = jnp.broadcast_to(ai_ref[:, cs], (rb, SCAN_COLS))

        def step(t, carry, cs=cs, a_r=a_r, a_i=a_i):
            s_r, s_i = carry
            rows = pl.ds(pl.multiple_of(t * rb, rb), rb)
            n_r = a_r * s_r - a_i * s_i + xr_s[rows, cs]
            n_i = a_r * s_i + a_i * s_r + xi_s[rows, cs]
            xr_s[rows, cs] = n_r
            xi_s[rows, cs] = n_i
            return n_r, n_i

        s_r, s_i = lax.fori_loop(0, tt, step, (sr_ref[:, cs], si_ref[:, cs]), unroll=min(tt, 8))
        sr_ref[:, cs] = s_r
        si_ref[:, cs] = s_i

    ys = []
    for k in range(CH_BLOCKS):
        ks = slice(k * STATE_BLOCK, (k + 1) * STATE_BLOCK)
        ys.append(jnp.dot(xr_s[:, ks].astype(BF16), wcr_ref[k], preferred_element_type=F32)
                  + jnp.dot(xi_s[:, ks].astype(BF16), wci_ref[k], preferred_element_type=F32)
                  + d_ref[:, k * LANES:(k + 1) * LANES] * u_s[k])
    act = jax.nn.gelu(jnp.concatenate(ys, axis=1)).astype(BF16)
    z = jnp.dot(act, gw_ref[...], preferred_element_type=F32) + gb_ref[...]
    mix = z[:, :D_MODEL] * jax.nn.sigmoid(z[:, D_MODEL:])
    for k in range(CH_BLOCKS):
        u_s[k] = mix[:, k * LANES:(k + 1) * LANES]
    for s in range(n_slab):
        o_ref[s] = h_ref[s] + jnp.concatenate([u_s[k, seq_rows(s), :] for k in range(CH_BLOCKS)], axis=1)


def _s5(h, g, s0r, s0i, ar, ai, wb, wcr, wci, d_skip, glu_w, glu_b, *, layer, tt):
    rb, t_len, _ = h.shape
    assert t_len % tt == 0 and rb % 8 == 0
    rows = rb * tt
    block = (1, rb, D_MODEL) if t_len == 1 else (rb, tt, D_MODEL)
    h_in = h.reshape(1, rb, D_MODEL) if t_len == 1 else h
    state = jax.ShapeDtypeStruct((rb, STATE_W), F32)
    out, sr, si = pl.pallas_call(
        functools.partial(_s5_kernel, rb=rb, tt=tt),
        grid=(t_len // tt,),
        in_specs=[pl.BlockSpec(block, lambda i: (0, i, 0)),
                  _const_spec((1, D_MODEL)),
                  _const_spec((rb, STATE_W)), _const_spec((rb, STATE_W)),
                  _const_spec((1, STATE_W)), _const_spec((1, STATE_W)),
                  _const_spec((CH_BLOCKS, LANES, 2 * STATE_BLOCK)),
                  _const_spec((CH_BLOCKS, STATE_BLOCK, LANES)),
                  _const_spec((CH_BLOCKS, STATE_BLOCK, LANES)),
                  _const_spec((1, D_MODEL)),
                  _layer_spec((D_MODEL, 2 * D_MODEL), layer),
                  _const_spec((1, 2 * D_MODEL))],
        out_specs=(pl.BlockSpec(block, lambda i: (0, i, 0)),
                   pl.BlockSpec((rb, STATE_W), lambda i: (0, 0)),
                   pl.BlockSpec((rb, STATE_W), lambda i: (0, 0))),
        out_shape=(jax.ShapeDtypeStruct(h_in.shape, F32), state, state),
        scratch_shapes=[pltpu.VMEM((CH_BLOCKS, rows, LANES), F32),
                        pltpu.VMEM((rows, STATE_W), F32), pltpu.VMEM((rows, STATE_W), F32)],
        compiler_params=_params("arbitrary"),
        name="s5",
    )(h_in, g.reshape(1, D_MODEL), s0r, s0i, ar, ai, wb, wcr, wci, d_skip.reshape(1, D_MODEL),
      glu_w, glu_b.reshape(1, 2 * D_MODEL))
    return out.reshape(h.shape), sr, si


def _proj_kernel(x_ref, g_ref, w_ref, o_ref, *, scale):
    xn = _rms(x_ref[...], g_ref[...]).astype(BF16)
    y = jnp.dot(xn, w_ref[...], preferred_element_type=F32)
    o_ref[...] = y if scale == 1.0 else y * scale


def _proj(x, g, w, layer=None, scale=1.0, tm=512):
    n = x.shape[0]
    tm = min(tm, n)
    nout = w.shape[-1]
    return pl.pallas_call(
        functools.partial(_proj_kernel, scale=scale),
        grid=(n // tm,),
        in_specs=[pl.BlockSpec((tm, D_MODEL), lambda i: (i, 0)),
                  _const_spec((1, D_MODEL)),
                  _layer_spec((D_MODEL, nout), layer)],
        out_specs=pl.BlockSpec((tm, nout), lambda i: (i, 0)),
        out_shape=jax.ShapeDtypeStruct((n, nout), F32),
        compiler_params=_params("parallel"),
        name="proj",
    )(x, g.reshape(1, D_MODEL), w)


def _kv_tails(t_len, tm):
    tails = []
    for w in DIL_WINDOWS:
        n_tail = min(w, t_len)
        if n_tail >= tm:
            assert n_tail % tm == 0
            tails.append(((t_len - n_tail) // tm, 0, tm))
        else:
            tails.append((t_len // tm - 1, tm - n_tail, n_tail))
    return tails


def _kv_kernel(x_ref, g_ref, w_ref, wt_ref, kv_ref, *tail_refs, tails):
    j = pl.program_id(1)
    xn = _rms(x_ref[...], g_ref[...]).astype(BF16)
    kv_ref[...] = jnp.dot(xn, w_ref[...], preferred_element_type=F32)
    for g, (first_tile, row0, n_tok) in enumerate(tails):
        def emit(g=g, row0=row0, n_tok=n_tok):
            wt = wt_ref[g * 2 * ATTN_WIDTH:(g + 1) * 2 * ATTN_WIDTH, :]
            tail_refs[g][...] = lax.dot_general(wt, xn[row0:row0 + n_tok], (((1,), (1,)), ((), ())),
                                                preferred_element_type=F32)

        if first_tile == 0:
            emit()
        else:
            pl.when(j >= first_tile)(emit)


def _kv_proj(x, g, w, w_t, tm=512):
    bn, t_len, _ = x.shape
    tm = min(tm, t_len)
    assert t_len % tm == 0
    nout = w.shape[-1]
    tails = _kv_tails(t_len, tm)
    tail_specs, tail_shapes = [], []
    for (first_tile, _, n_tok), win in zip(tails, DIL_WINDOWS):
        tail_specs.append(pl.BlockSpec((None, 2 * ATTN_WIDTH, n_tok),
                                       lambda b, j, f=first_tile: (b, 0, jnp.maximum(j - f, 0))))
        tail_shapes.append(jax.ShapeDtypeStruct((bn, 2 * ATTN_WIDTH, min(win, t_len)), F32))
    return pl.pallas_call(
        functools.partial(_kv_kernel, tails=tails),
        grid=(bn, t_len // tm),
        in_specs=[pl.BlockSpec((None, tm, D_MODEL), lambda b, j: (b, j, 0)),
                  _const_spec((1, D_MODEL)),
                  _const_spec((D_MODEL, nout)),
                  _const_spec((nout, D_MODEL))],
        out_specs=[pl.BlockSpec((None, tm, nout), lambda b, j: (b, j, 0))] + tail_specs,
        out_shape=[jax.ShapeDtypeStruct((bn, t_len, nout), F32)] + tail_shapes,
        compiler_params=_params("parallel", "arbitrary"),
        name="kv_proj",
    )(x, g.reshape(1, D_MODEL), w, w_t)


def _oproj_kernel(a_ref, w_ref, h_ref, o_ref):
    o_ref[...] = h_ref[...] + jnp.dot(a_ref[...].astype(BF16), w_ref[...], preferred_element_type=F32)


def _oproj(a, w, layer, h, tm=512):
    n = h.shape[0]
    tm = min(tm, n)
    return pl.pallas_call(
        _oproj_kernel,
        grid=(n // tm,),
        in_specs=[pl.BlockSpec((tm, ATTN_WIDTH), lambda i: (i, 0)),
                  _layer_spec((ATTN_WIDTH, D_MODEL), layer),
                  pl.BlockSpec((tm, D_MODEL), lambda i: (i, 0))],
        out_specs=pl.BlockSpec((tm, D_MODEL), lambda i: (i, 0)),
        out_shape=jax.ShapeDtypeStruct((n, D_MODEL), F32),
        compiler_params=_params("parallel"),
        name="oproj",
    )(a, w, h)


def _alibi_slopes():
    n = N_DIL * HEADS_PER_GROUP
    s = jnp.exp2(-8.0 * jnp.arange(1, n + 1, dtype=F32) / n)
    return s.reshape(N_DIL, HEADS_PER_GROUP)


def _prompt_bias():
    qi = jnp.arange(R_LEN)[:, None]
    kj = jnp.arange(2 * R_LEN)[None, :]
    diff = R_LEN + qi - kj
    valid = (diff >= 0) & (diff <= R_LEN)
    dil = jnp.asarray(DIL_RATES, jnp.int32)[:, None, None]
    dist = (dil * diff[None]).astype(F32)
    bias = -_alibi_slopes()[:, :, None, None] * dist[:, None]
    return jnp.where(valid[None, None], bias, NEG)


def _sample_bias():
    kk = R_LEN - jnp.arange(R_LEN)
    dil = jnp.asarray(DIL_RATES, jnp.int32)[:, None]
    dist = (dil * kk[None]).astype(F32)
    return -_alibi_slopes()[:, :, None] * dist[:, None]


MERGE_ROWS = 256
INTERLEAVE = 8


def _attn_kernel(q0, q1, q2, k0, k1, k2, v0, v1, v2, bias_ref, o_ref, acc_s, m_s, l_s, *, t_len):
    qs, ks, vs = (q0, q1, q2), (k0, k1, k2), (v0, v1, v2)
    lo = lax.broadcasted_iota(jnp.int32, (R_LEN, LANES), 1) < HEAD_DIM

    def block(g, q_idx, k_idx, first):
        q = qs[g][q_idx, :]
        qb = jnp.concatenate([jnp.where(lo, q, 0.0), jnp.where(lo, 0.0, q)], axis=0).astype(BF16)
        kb = ks[g][k_idx, :].astype(BF16)
        vb = vs[g][k_idx, :].astype(BF16)
        s = lax.dot_general(qb, kb, (((1,), (1,)), ((), ())), preferred_element_type=F32)
        bias = bias_ref[g, :, :, R_LEN:] if first else bias_ref[g]
        s = s + bias.reshape(2 * R_LEN, bias.shape[-1])
        m = jnp.max(s, axis=-1, keepdims=True)
        p = jnp.exp(s - m)
        l = jnp.sum(p, axis=-1, keepdims=True)
        a = jnp.dot(p.astype(BF16), vb, preferred_element_type=F32)
        acc_s[g, q_idx, :] = jnp.where(lo, a[:R_LEN], a[R_LEN:])
        m_s[g, q_idx, :] = jnp.where(lo, m[:R_LEN], m[R_LEN:])
        l_s[g, q_idx, :] = jnp.where(lo, l[:R_LEN], l[R_LEN:])

    for g in range(N_DIL):
        dil = DIL_RATES[g]
        span = R_LEN * dil
        n_blocks = t_len // span

        def rows(start, n, dil=dil):
            if dil > 1:
                return pl.ds(start, n, stride=dil)
            return pl.ds(start if isinstance(start, int) else pl.multiple_of(start, R_LEN), n)

        if dil <= INTERLEAVE:
            for r in range(dil):
                block(g, rows(r, R_LEN), rows(r, R_LEN), True)
        else:
            def first_blocks(i, c, g=g, rows=rows):
                for j in range(INTERLEAVE):
                    r = i * INTERLEAVE + j
                    block(g, rows(r, R_LEN), rows(r, R_LEN), True)
                return c

            lax.fori_loop(0, dil // INTERLEAVE, first_blocks, 0)

        later = n_blocks - 1
        if later == 0:
            continue
        per_trip = max(w for w in range(1, INTERLEAVE + 1) if later % w == 0) if dil == 1 else 1

        def later_blocks(i, c, g=g, dil=dil, span=span, rows=rows, per_trip=per_trip):
            for j in range(per_trip):
                nb = 1 + i * per_trip + j
                for r in range(dil):
                    start = nb * span + r
                    block(g, rows(start, R_LEN), rows(start - span, 2 * R_LEN), False)
            return c

        lax.fori_loop(0, later // per_trip, later_blocks, 0)

    def merge(i, c):
        rows = pl.ds(pl.multiple_of(i * MERGE_ROWS, MERGE_ROWS), MERGE_ROWS)
        ms = [m_s[g, rows, :] for g in range(N_DIL)]
        mx = jnp.maximum(jnp.maximum(ms[0], ms[1]), ms[2])
        ws = [jnp.exp(m - mx) for m in ms]
        den = ws[0] * l_s[0, rows, :] + ws[1] * l_s[1, rows, :] + ws[2] * l_s[2, rows, :]
        num = ws[0] * acc_s[0, rows, :] + ws[1] * acc_s[1, rows, :] + ws[2] * acc_s[2, rows, :]
        o_ref[rows, :] = num / den
        return c

    lax.fori_loop(0, t_len // MERGE_ROWS, merge, 0)


def _attn_prompt(q, kv, bias):
    bn, t_len, _ = q.shape
    pairs = ATTN_WIDTH // LANES
    per_group = 2 * ATTN_WIDTH // LANES

    def col(off):
        return pl.BlockSpec((None, t_len, LANES), lambda b, hp: (b, 0, off + hp))

    in_specs = ([col(g * pairs) for g in range(N_DIL)]
                + [col(g * per_group) for g in range(N_DIL)]
                + [col(g * per_group + pairs) for g in range(N_DIL)]
                + [pl.BlockSpec((N_DIL, 2, R_LEN, 2 * R_LEN), lambda b, hp: (0, hp, 0, 0))])
    scratch = pltpu.VMEM((N_DIL, t_len, LANES), F32)
    return pl.pallas_call(
        functools.partial(_attn_kernel, t_len=t_len),
        grid=(bn, pairs),
        in_specs=in_specs,
        out_specs=pl.BlockSpec((None, t_len, LANES), lambda b, hp: (b, 0, hp)),
        out_shape=jax.ShapeDtypeStruct((bn, t_len, ATTN_WIDTH), F32),
        scratch_shapes=[scratch, scratch, scratch],
        compiler_params=_params("parallel", "parallel"),
        name="attn_prompt",
    )(q, q, q, kv, kv, kv, kv, kv, kv, bias)


SAMPLE_BLOCK = 8


def _window_kernel(c_ref, o_ref, *, dil):
    w = c_ref.shape[-1]
    pos = lax.broadcasted_iota(jnp.int32, (w, R_LEN), 0)
    col = lax.broadcasted_iota(jnp.int32, (w, R_LEN), 1)
    pick = jnp.where(pos == col * dil, 1.0, 0.0).astype(BF16)
    o_ref[...] = jnp.dot(c_ref[...].astype(BF16), pick, preferred_element_type=F32).astype(BF16)


def _strided_window(cache, dil):
    bd, w = cache.shape[:2]
    assert w == R_LEN * dil
    slab = jnp.transpose(cache, (0, 2, 3, 4, 1)).reshape(bd, 2 * ATTN_WIDTH, w)
    if dil == 1:
        return slab
    return pl.pallas_call(
        functools.partial(_window_kernel, dil=dil),
        grid=(bd,),
        in_specs=[pl.BlockSpec((None, 2 * ATTN_WIDTH, w), lambda i: (i, 0, 0))],
        out_specs=pl.BlockSpec((None, 2 * ATTN_WIDTH, R_LEN), lambda i: (i, 0, 0)),
        out_shape=jax.ShapeDtypeStruct((bd, 2 * ATTN_WIDTH, R_LEN), BF16),
        compiler_params=_params("parallel"),
        name="cache_window",
    )(slab)


def _attn_sample_kernel(q_ref, kvn_ref, c0, c1, c2, bias_ref, o_ref):
    windows = (c0, c1, c2)
    sub = lax.broadcasted_iota(jnp.int32, (HEADS_PER_GROUP, ATTN_WIDTH), 0)
    lane = lax.broadcasted_iota(jnp.int32, (HEADS_PER_GROUP, ATTN_WIDTH), 1)
    own = (lane // HEAD_DIM) == sub
    for b in range(SAMPLE_BLOCK):
        os_, ms, ls = [], [], []
        for g in range(N_DIL):
            base = g * 2 * ATTN_WIDTH
            qbd = jnp.where(own, q_ref[b:b + 1, g * ATTN_WIDTH:(g + 1) * ATTN_WIDTH], 0.0)
            kt = windows[g][b, :ATTN_WIDTH, :].astype(BF16)
            vt = windows[g][b, ATTN_WIDTH:, :].astype(BF16)
            kn = kvn_ref[b:b + 1, base:base + ATTN_WIDTH]
            vn = kvn_ref[b:b + 1, base + ATTN_WIDTH:base + 2 * ATTN_WIDTH]
            s = jnp.dot(qbd.astype(BF16), kt, preferred_element_type=F32) + bias_ref[g]
            s_new = jnp.sum(qbd * kn, axis=-1, keepdims=True)
            m = jnp.maximum(jnp.max(s, axis=-1, keepdims=True), s_new)
            p = jnp.exp(s - m)
            p_new = jnp.exp(s_new - m)
            ls.append(jnp.sum(p, axis=-1, keepdims=True) + p_new)
            os_.append(lax.dot_general(p.astype(BF16), vt, (((1,), (1,)), ((), ())),
                                       preferred_element_type=F32) + p_new * vn)
            ms.append(m)
        mx = jnp.maximum(jnp.maximum(ms[0], ms[1]), ms[2])
        ws = [jnp.exp(m - mx) for m in ms]
        den = ws[0] * ls[0] + ws[1] * ls[1] + ws[2] * ls[2]
        o = (ws[0] * os_[0] + ws[1] * os_[1] + ws[2] * os_[2]) / den
        o_ref[b:b + 1, :] = jnp.sum(jnp.where(own, o, 0.0), axis=0, keepdims=True)


def _attn_sample(q, kv_new, windows, bias):
    bd = q.shape[0]
    assert bd % SAMPLE_BLOCK == 0
    window_spec = pl.BlockSpec((SAMPLE_BLOCK, 2 * ATTN_WIDTH, R_LEN), lambda i: (i, 0, 0))
    return pl.pallas_call(
        _attn_sample_kernel,
        grid=(bd // SAMPLE_BLOCK,),
        in_specs=[pl.BlockSpec((SAMPLE_BLOCK, N_DIL * ATTN_WIDTH), lambda i: (i, 0)),
                  pl.BlockSpec((SAMPLE_BLOCK, N_DIL * 2 * ATTN_WIDTH), lambda i: (i, 0))]
                 + [window_spec] * N_DIL + [_const_spec((N_DIL, HEADS_PER_GROUP, R_LEN))],
        out_specs=pl.BlockSpec((SAMPLE_BLOCK, ATTN_WIDTH), lambda i: (i, 0)),
        out_shape=jax.ShapeDtypeStruct((bd, ATTN_WIDTH), F32),
        compiler_params=_params("parallel"),
        name="attn_sample",
    )(q, kv_new, *windows, bias)


def kernel(x_prompt, x_sample, state_ssm_re, state_ssm_im, cache_kv_w128, cache_kv_w512, cache_kv_w2048, norm_ffn1, ffn1_w_in, ffn1_w_out, norm_mix, norm_ffn2, ffn2_w_in, ffn2_w_out, norm_kv, norm_final, ssm_log_dt, ssm_lambda_re, ssm_lambda_im, ssm_b_re, ssm_b_im, ssm_c_re, ssm_c_im, ssm_d, glu_w, glu_b, attn_w_q, attn_w_kv, attn_w_o):
    bn, t_len, _ = x_prompt.shape
    bd = x_sample.shape[0]
    assert x_sample.shape[1] == 1

    w1_in, w1_out = ffn1_w_in.astype(BF16), ffn1_w_out.astype(BF16)
    w2_in, w2_out = ffn2_w_in.astype(BF16), ffn2_w_out.astype(BF16)
    glu_wb = glu_w.astype(BF16)
    w_q, w_kv, w_o = attn_w_q.astype(BF16), attn_w_kv.astype(BF16), attn_w_o.astype(BF16)

    ssm = []
    for layer in range(N_A_LAYERS):
        ar, ai, bbr, bbi = _discretise(ssm_log_dt[layer], ssm_lambda_re[layer], ssm_lambda_im[layer],
                                       ssm_b_re[layer], ssm_b_im[layer])
        wb = jnp.concatenate([_block_diag_in(bbr), _block_diag_in(bbi)], axis=-1).astype(BF16)
        wcr = _block_diag_out(ssm_c_re[layer]).astype(BF16)
        wci = _block_diag_out(-ssm_c_im[layer]).astype(BF16)
        ssm.append((ar.reshape(1, STATE_W), ai.reshape(1, STATE_W), wb, wcr, wci,
                    ssm_d[layer].reshape(D_MODEL), glu_wb, glu_b[layer]))

    def s5_layers(h, s0r, s0i, tt):
        n_seq, n_t, _ = h.shape
        finals_r, finals_i = [], []
        for layer in range(N_A_LAYERS):
            h = _ffn(h.reshape(n_seq * n_t, D_MODEL), norm_ffn1[layer], w1_in, w1_out, layer)
            h, sr, si = _s5(h.reshape(n_seq, n_t, D_MODEL), norm_mix[layer], s0r[layer], s0i[layer],
                            *ssm[layer], layer=layer, tt=tt)
            finals_r.append(sr.reshape(n_seq, N_SSM_GROUPS, SSM_STATE))
            finals_i.append(si.reshape(n_seq, N_SSM_GROUPS, SSM_STATE))
            h = _ffn(h.reshape(n_seq * n_t, D_MODEL), norm_ffn2[layer], w2_in, w2_out, layer)
        return h, jnp.stack(finals_r), jnp.stack(finals_i)

    def attn_layers(h, kv, attend):
        for layer in range(N_A_LAYERS, DEPTH):
            b = layer - N_A_LAYERS
            h = _ffn(h, norm_ffn1[layer], w1_in, w1_out, layer)
            q = _proj(h, norm_mix[layer], w_q, layer=b, scale=HEAD_DIM ** -0.5)
            h = _oproj(attend(q, kv), w_o, b, h)
            h = _ffn(h, norm_ffn2[layer], w2_in, w2_out, layer,
                     final_g=norm_final if layer == DEPTH - 1 else None)
        return h

    zeros = jnp.zeros((N_A_LAYERS, bn, STATE_W), F32)
    h, ssm_re_p, ssm_im_p = s5_layers(x_prompt, zeros, zeros, tt=64)
    kv_p, *kv_tails = _kv_proj(h.reshape(bn, t_len, D_MODEL), norm_kv, w_kv, w_kv.T)
    prompt_bias = _prompt_bias()
    h = attn_layers(h, kv_p, lambda q, kv: _attn_prompt(
        q.reshape(bn, t_len, -1), kv, prompt_bias).reshape(bn * t_len, ATTN_WIDTH))
    y_prompt = h.reshape(bn, t_len, D_MODEL)
    kv_out_p = [jnp.transpose(tail.reshape(bn, 2, HEADS_PER_GROUP, HEAD_DIM, -1), (0, 4, 1, 2, 3))
                for tail in kv_tails]

    hs, ssm_re_s, ssm_im_s = s5_layers(x_sample, state_ssm_re.reshape(N_A_LAYERS, bd, STATE_W),
                                       state_ssm_im.reshape(N_A_LAYERS, bd, STATE_W), tt=1)
    kv_s = _proj(hs, norm_kv, w_kv)
    windows = [_strided_window(cache, dil)
               for cache, dil in zip((cache_kv_w128, cache_kv_w512, cache_kv_w2048), DIL_RATES)]
    sample_bias = _sample_bias()
    hs = attn_layers(hs, kv_s, lambda q, kv: _attn_sample(q, kv, windows, sample_bias))
    y_sample = hs.reshape(bd, 1, D_MODEL)
    kv_s = kv_s.reshape(bd, 1, N_DIL, 2, HEADS_PER_GROUP, HEAD_DIM)

    return (y_prompt, y_sample, ssm_re_p, ssm_im_p, kv_out_p[0], kv_out_p[1], kv_out_p[2],
            ssm_re_s, ssm_im_s, kv_s[:, :, 0], kv_s[:, :, 1], kv_s[:, :, 2])
```

```python
import functools

import jax
import jax.numpy as jnp
from jax import lax
from jax.experimental import pallas as pl
from jax.experimental.pallas import tpu as pltpu

F32 = jnp.float32
BF16 = jnp.bfloat16

D_MODEL = 1024
DEPTH = 4
N_A_LAYERS = DEPTH // 2
SSM_GROUP = 16
N_SSM_GROUPS = D_MODEL // SSM_GROUP
SSM_STATE = 64
STATE_W = N_SSM_GROUPS * SSM_STATE
DIL_WINDOWS = (128, 512, 2048)
DIL_RATES = (1, 4, 16)
N_DIL = 3
HEADS_PER_GROUP = 8
HEAD_DIM = 64
ATTN_WIDTH = HEADS_PER_GROUP * HEAD_DIM
R_LEN = DIL_WINDOWS[0] // DIL_RATES[0]
D_FF = 2816
EPS = 1e-6
NEG = -1e30

LANES = 128
MXU_TILE = 256
CH_BLOCKS = D_MODEL // LANES
GROUPS_PER_BLOCK = LANES // SSM_GROUP
STATE_BLOCK = GROUPS_PER_BLOCK * SSM_STATE
VMEM_LIMIT_BYTES = 56 * 1024 * 1024

assert all(w // d == R_LEN for w, d in zip(DIL_WINDOWS, DIL_RATES))


def _params(*semantics):
    return pltpu.CompilerParams(dimension_semantics=semantics, vmem_limit_bytes=VMEM_LIMIT_BYTES)


def _const_spec(shape):
    zeros = (0,) * len(shape)
    return pl.BlockSpec(shape, lambda *_: zeros, pipeline_mode=pl.Buffered(1))


def _layer_spec(shape, layer):
    if layer is None:
        return _const_spec(shape)
    idx = (layer,) + (0,) * len(shape)
    return pl.BlockSpec((None,) + tuple(shape), lambda *_: idx, pipeline_mode=pl.Buffered(1))


def _rms(x, g):
    return x * lax.rsqrt(jnp.mean(x * x, axis=-1, keepdims=True) + EPS) * g


def _ff_chunks(n_chunks):
    tiles = D_FF // MXU_TILE
    assert tiles * MXU_TILE == D_FF and n_chunks <= tiles
    cuts = [MXU_TILE * ((tiles * c + n_chunks - 1) // n_chunks) for c in range(n_chunks + 1)]
    return list(zip(cuts[:-1], cuts[1:]))


def _swiglu_residual(x, g, wa_ref, wb_ref, wout_ref, n_chunks):
    xn = _rms(x, g).astype(BF16)
    acc = None
    for lo, hi in _ff_chunks(n_chunks):
        a = jnp.dot(xn, wa_ref[:, lo:hi], preferred_element_type=F32)
        b = jnp.dot(xn, wb_ref[:, lo:hi], preferred_element_type=F32)
        act = (a * jax.nn.sigmoid(a) * b).astype(BF16)
        d = jnp.dot(act, wout_ref[lo:hi, :], preferred_element_type=F32)
        acc = d if acc is None else acc + d
    return x + 0.5 * acc


def _ffn_kernel(*refs, n_chunks, attn_in, final, q_scale):
    refs = list(refs)
    x = refs.pop(0)[...]
    if attn_in:
        a_ref, wo_ref = refs.pop(0), refs.pop(0)
        x = x + jnp.dot(a_ref[...].astype(BF16), wo_ref[...], preferred_element_type=F32)
    g_ref, wa_ref, wb_ref, wout_ref = (refs.pop(0) for _ in range(4))
    y = _swiglu_residual(x, g_ref[...], wa_ref, wb_ref, wout_ref, n_chunks)
    if final:
        y = _rms(y, refs.pop(0)[...])
    if q_scale is not None:
        gq_ref, wq_ref = refs.pop(0), refs.pop(0)
    refs.pop(0)[...] = y
    if q_scale is not None:
        q = jnp.dot(_rms(y, gq_ref[...]).astype(BF16), wq_ref[...], preferred_element_type=F32)
        refs.pop(0)[...] = q * q_scale


def _ffn(x, g, weights, attn=None, final_g=None, q_proj=None, tm=512, n_chunks=2):
    n = x.shape[0]
    tm = min(tm, n)
    assert n % tm == 0
    rows = lambda width: pl.BlockSpec((tm, width), lambda i: (i, 0))
    in_specs, args = [rows(D_MODEL)], [x]
    if attn is not None:
        merged, w_o, idx = attn
        in_specs += [rows(ATTN_WIDTH), _layer_spec((ATTN_WIDTH, D_MODEL), idx)]
        args += [merged, w_o]
    in_specs += [_const_spec((1, D_MODEL)), _const_spec((D_MODEL, D_FF)), _const_spec((D_MODEL, D_FF)),
                 _const_spec((D_FF, D_MODEL))]
    args += [g.reshape(1, D_MODEL), *weights]
    if final_g is not None:
        in_specs.append(_const_spec((1, D_MODEL)))
        args.append(final_g.reshape(1, D_MODEL))
    out_specs, out_shape = [rows(D_MODEL)], [jax.ShapeDtypeStruct((n, D_MODEL), F32)]
    q_scale = None
    if q_proj is not None:
        gq, w_q, idx, q_scale = q_proj
        nq = w_q.shape[-1]
        in_specs += [_const_spec((1, D_MODEL)), _layer_spec((D_MODEL, nq), idx)]
        args += [gq.reshape(1, D_MODEL), w_q]
        out_specs.append(rows(nq))
        out_shape.append(jax.ShapeDtypeStruct((n, nq), F32))
    out = pl.pallas_call(
        functools.partial(_ffn_kernel, n_chunks=n_chunks, attn_in=attn is not None,
                          final=final_g is not None, q_scale=q_scale),
        grid=(n // tm,),
        in_specs=in_specs,
        out_specs=out_specs,
        out_shape=out_shape,
        compiler_params=_params("parallel"),
        name="ffn",
    )(*args)
    return out if q_proj is not None else out[0]


def _ffn_cast_kernel(x_ref, g_ref, wa_ref, wb_ref, wout_ref, *rest, final):
    if final:
        gf_ref, o_ref, wa_o, wb_o, wout_o, acc_s = rest
    else:
        o_ref, wa_o, wb_o, wout_o, acc_s = rest
    c = pl.program_id(0)
    wa = wa_ref[...].astype(BF16)
    wb = wb_ref[...].astype(BF16)
    wout = wout_ref[...].astype(BF16)
    wa_o[...] = wa
    wb_o[...] = wb
    wout_o[...] = wout
    x = x_ref[...]
    xn = _rms(x, g_ref[...]).astype(BF16)
    a = jnp.dot(xn, wa, preferred_element_type=F32)
    b = jnp.dot(xn, wb, preferred_element_type=F32)
    d = jnp.dot((a * jax.nn.sigmoid(a) * b).astype(BF16), wout, preferred_element_type=F32)

    @pl.when(c == 0)
    def _():
        acc_s[...] = d

    @pl.when(c > 0)
    def _():
        acc_s[...] += d

    @pl.when(c == pl.num_programs(0) - 1)
    def _():
        y = x + 0.5 * acc_s[...]
        o_ref[...] = _rms(y, gf_ref[...]) if final else y


def _ffn_cast(x, g, w_in, w_out, layer, final_g=None):
    n = x.shape[0]
    n_chunks = D_FF // MXU_TILE
    final = final_g is not None
    in_specs = [_const_spec((n, D_MODEL)), _const_spec((1, D_MODEL)),
                pl.BlockSpec((None, D_MODEL, MXU_TILE), lambda c: (layer, 0, c)),
                pl.BlockSpec((None, D_MODEL, MXU_TILE), lambda c: (layer, 0, n_chunks + c)),
                pl.BlockSpec((None, MXU_TILE, D_MODEL), lambda c: (layer, c, 0))]
    args = [x, g.reshape(1, D_MODEL), w_in, w_in, w_out]
    if final:
        in_specs.append(_const_spec((1, D_MODEL)))
        args.append(final_g.reshape(1, D_MODEL))
    y, wa, wb, wo = pl.pallas_call(
        functools.partial(_ffn_cast_kernel, final=final),
        grid=(n_chunks,),
        in_specs=in_specs,
        out_specs=[pl.BlockSpec((n, D_MODEL), lambda c: (0, 0)),
                   pl.BlockSpec((D_MODEL, MXU_TILE), lambda c: (0, c)),
                   pl.BlockSpec((D_MODEL, MXU_TILE), lambda c: (0, c)),
                   pl.BlockSpec((MXU_TILE, D_MODEL), lambda c: (c, 0))],
        out_shape=[jax.ShapeDtypeStruct((n, D_MODEL), F32),
                   jax.ShapeDtypeStruct((D_MODEL, D_FF), BF16),
                   jax.ShapeDtypeStruct((D_MODEL, D_FF), BF16),
                   jax.ShapeDtypeStruct((D_FF, D_MODEL), BF16)],
        scratch_shapes=[pltpu.VMEM((n, D_MODEL), F32)],
        compiler_params=_params("arbitrary"),
        name="ffn_cast",
    )(*args)
    return y, (wa, wb, wo)


def _disc_kernel(ldt_ref, lr_ref, li_ref, br_ref, bi_ref, ar_ref, ai_ref, bbr_ref, bbi_ref):
    dt = jnp.exp(ldt_ref[...])
    lr = lr_ref[...]
    li = li_ref[...]
    mag = jnp.exp(lr * dt)
    ar = mag * jnp.cos(li * dt)
    ai = mag * jnp.sin(li * dt)
    den = lr * lr + li * li
    cr = ((ar - 1.0) * lr + ai * li) / den
    ci = (ai * lr - (ar - 1.0) * li) / den
    ar_ref[...] = ar
    ai_ref[...] = ai
    br = br_ref[...]
    bi = bi_ref[...]
    crb = cr[:, None, :]
    cib = ci[:, None, :]
    bbr_ref[...] = crb * br - cib * bi
    bbi_ref[...] = crb * bi + cib * br


def _discretise(log_dt, lam_re, lam_im, b_re, b_im):
    g, p, hg = N_SSM_GROUPS, SSM_STATE, SSM_GROUP
    gp = jax.ShapeDtypeStruct((g, p), F32)
    ghp = jax.ShapeDtypeStruct((g, hg, p), F32)
    return pl.pallas_call(
        _disc_kernel,
        out_shape=(gp, gp, ghp, ghp),
        name="s5_disc",
    )(log_dt.reshape(g, 1), lam_re, lam_im, jnp.swapaxes(b_re, 1, 2), jnp.swapaxes(b_im, 1, 2))


def _block_diag_in(bb):
    x = bb.reshape(CH_BLOCKS, GROUPS_PER_BLOCK, SSM_GROUP, SSM_STATE)
    eye = jnp.eye(GROUPS_PER_BLOCK, dtype=bb.dtype)
    return jnp.einsum('kghp,gj->kghjp', x, eye).reshape(CH_BLOCKS, LANES, STATE_BLOCK)


def _block_diag_out(c):
    x = c.reshape(CH_BLOCKS, GROUPS_PER_BLOCK, SSM_GROUP, SSM_STATE)
    eye = jnp.eye(GROUPS_PER_BLOCK, dtype=c.dtype)
    return jnp.einsum('kghp,gj->kgpjh', x, eye).reshape(CH_BLOCKS, STATE_BLOCK, LANES)


SCAN_COLS = 512


def _s5_kernel(h_ref, g_ref, s0r_ref, s0i_ref, ar_ref, ai_ref, wb_ref, wcr_ref, wci_ref, d_ref,
               gw_ref, gb_ref, o_ref, sr_ref, si_ref, u_s, xr_s, xi_s, *, rb, tt):
    n_slab, per_slab = h_ref.shape[:2]
    assert n_slab * per_slab == rb * tt and (n_slab == rb or tt == 1)

    def seq_rows(s):
        return pl.ds(s, per_slab, stride=n_slab) if n_slab > 1 else pl.ds(0, per_slab)

    @pl.when(pl.program_id(0) == 0)
    def _():
        sr_ref[...] = s0r_ref[...]
        si_ref[...] = s0i_ref[...]

    for s in range(n_slab):
        un = _rms(h_ref[s], g_ref[...])
        for k in range(CH_BLOCKS):
            u_s[k, seq_rows(s), :] = un[:, k * LANES:(k + 1) * LANES]
    for k in range(CH_BLOCKS):
        bu = jnp.dot(u_s[k].astype(BF16), wb_ref[k], preferred_element_type=F32)
        xr_s[:, k * STATE_BLOCK:(k + 1) * STATE_BLOCK] = bu[:, :STATE_BLOCK]
        xi_s[:, k * STATE_BLOCK:(k + 1) * STATE_BLOCK] = bu[:, STATE_BLOCK:]

    for c in range(STATE_W // SCAN_COLS):
        cs = slice(c * SCAN_COLS, (c + 1) * SCAN_COLS)
        a_r = jnp.broadcast_to(ar_ref[:, cs], (rb, SCAN_COLS))
        a_i = jnp.broadcast_to(ai_ref[:, cs], (rb, SCAN_COLS))

        def step(t, carry, cs=cs, a_r=a_r, a_i=a_i):
            s_r, s_i = carry
            rows = pl.ds(pl.multiple_of(t * rb, rb), rb)
            n_r = a_r * s_r - a_i * s_i + xr_s[rows, cs]
            n_i = a_r * s_i + a_i * s_r + xi_s[rows, cs]
            xr_s[rows, cs] = n_r
            xi_s[rows, cs] = n_i
            return n_r, n_i

        s_r, s_i = lax.fori_loop(0, tt, step, (sr_ref[:, cs], si_ref[:, cs]), unroll=True)
        sr_ref[:, cs] = s_r
        si_ref[:, cs] = s_i

    ys = []
    for k in range(CH_BLOCKS):
        ks = slice(k * STATE_BLOCK, (k + 1) * STATE_BLOCK)
        ys.append(jnp.dot(xr_s[:, ks].astype(BF16), wcr_ref[k], preferred_element_type=F32)
                  + jnp.dot(xi_s[:, ks].astype(BF16), wci_ref[k], preferred_element_type=F32)
                  + d_ref[:, k * LANES:(k + 1) * LANES] * u_s[k])
    act = jax.nn.gelu(jnp.concatenate(ys, axis=1)).astype(BF16)
    z = jnp.dot(act, gw_ref[...], preferred_element_type=F32) + gb_ref[...]
    mix = z[:, :D_MODEL] * jax.nn.sigmoid(z[:, D_MODEL:])
    for k in range(CH_BLOCKS):
        u_s[k] = mix[:, k * LANES:(k + 1) * LANES]
    for s in range(n_slab):
        o_ref[s] = h_ref[s] + jnp.concatenate([u_s[k, seq_rows(s), :] for k in range(CH_BLOCKS)], axis=1)


def _s5(h, g, s0r, s0i, ar, ai, wb, wcr, wci, d_skip, glu_w, glu_b, *, layer, tt):
    rb, t_len, _ = h.shape
    assert t_len % tt == 0 and rb % 8 == 0
    rows = rb * tt
    block = (1, rb, D_MODEL) if t_len == 1 else (rb, tt, D_MODEL)
    h_in = h.reshape(1, rb, D_MODEL) if t_len == 1 else h
    state = jax.ShapeDtypeStruct((rb, STATE_W), F32)
    out, sr, si = pl.pallas_call(
        functools.partial(_s5_kernel, rb=rb, tt=tt),
        grid=(t_len // tt,),
        in_specs=[pl.BlockSpec(block, lambda i: (0, i, 0)),
                  _const_spec((1, D_MODEL)),
                  _const_spec((rb, STATE_W)), _const_spec((rb, STATE_W)),
                  _const_spec((1, STATE_W)), _const_spec((1, STATE_W)),
                  _const_spec((CH_BLOCKS, LANES, 2 * STATE_BLOCK)),
                  _const_spec((CH_BLOCKS, STATE_BLOCK, LANES)),
                  _const_spec((CH_BLOCKS, STATE_BLOCK, LANES)),
                  _const_spec((1, D_MODEL)),
                  _layer_spec((D_MODEL, 2 * D_MODEL), layer),
                  _const_spec((1, 2 * D_MODEL))],
        out_specs=(pl.BlockSpec(block, lambda i: (0, i, 0)),
                   pl.BlockSpec((rb, STATE_W), lambda i: (0, 0)),
                   pl.BlockSpec((rb, STATE_W), lambda i: (0, 0))),
        out_shape=(jax.ShapeDtypeStruct(h_in.shape, F32), state, state),
        scratch_shapes=[pltpu.VMEM((CH_BLOCKS, rows, LANES), F32),
                        pltpu.VMEM((rows, STATE_W), F32), pltpu.VMEM((rows, STATE_W), F32)],
        compiler_params=_params("arbitrary"),
        name="s5",
    )(h_in, g.reshape(1, D_MODEL), s0r, s0i, ar, ai, wb, wcr, wci, d_skip.reshape(1, D_MODEL),
      glu_w, glu_b.reshape(1, 2 * D_MODEL))
    return out.reshape(h.shape), sr, si


def _proj_kernel(x_ref, g_ref, w_ref, o_ref, *, scale):
    xn = _rms(x_ref[...], g_ref[...]).astype(BF16)
    y = jnp.dot(xn, w_ref[...], preferred_element_type=F32)
    o_ref[...] = y if scale == 1.0 else y * scale


def _proj(x, g, w, layer=None, scale=1.0, tm=512):
    n = x.shape[0]
    tm = min(tm, n)
    nout = w.shape[-1]
    return pl.pallas_call(
        functools.partial(_proj_kernel, scale=scale),
        grid=(n // tm,),
        in_specs=[pl.BlockSpec((tm, D_MODEL), lambda i: (i, 0)),
                  _const_spec((1, D_MODEL)),
                  _layer_spec((D_MODEL, nout), layer)],
        out_specs=pl.BlockSpec((tm, nout), lambda i: (i, 0)),
        out_shape=jax.ShapeDtypeStruct((n, nout), F32),
        compiler_params=_params("parallel"),
        name="proj",
    )(x, g.reshape(1, D_MODEL), w)


def _kv_tails(t_len, tm):
    tails = []
    for w in DIL_WINDOWS:
        n_tail = min(w, t_len)
        if n_tail >= tm:
            assert n_tail % tm == 0
            tails.append(((t_len - n_tail) // tm, 0, tm))
        else:
            tails.append((t_len // tm - 1, tm - n_tail, n_tail))
    return tails


def _kv_kernel(x_ref, g_ref, w_ref, wt_ref, kv_ref, *tail_refs, tails):
    j = pl.program_id(1)
    xn = _rms(x_ref[...], g_ref[...]).astype(BF16)
    kv_ref[...] = jnp.dot(xn, w_ref[...], preferred_element_type=F32)
    for g, (first_tile, row0, n_tok) in enumerate(tails):
        def emit(g=g, row0=row0, n_tok=n_tok):
            wt = wt_ref[g * 2 * ATTN_WIDTH:(g + 1) * 2 * ATTN_WIDTH, :]
            tail_refs[g][...] = lax.dot_general(wt, xn[row0:row0 + n_tok], (((1,), (1,)), ((), ())),
                                                preferred_element_type=F32)

        if first_tile == 0:
            emit()
        else:
            pl.when(j >= first_tile)(emit)


def _kv_proj(x, g, w, w_t, tm=512):
    bn, t_len, _ = x.shape
    tm = min(tm, t_len)
    assert t_len % tm == 0
    nout = w.shape[-1]
    tails = _kv_tails(t_len, tm)
    tail_specs, tail_shapes = [], []
    for (first_tile, _, n_tok), win in zip(tails, DIL_WINDOWS):
        tail_specs.append(pl.BlockSpec((None, 2 * ATTN_WIDTH, n_tok),
                                       lambda b, j, f=first_tile: (b, 0, jnp.maximum(j - f, 0))))
        tail_shapes.append(jax.ShapeDtypeStruct((bn, 2 * ATTN_WIDTH, min(win, t_len)), F32))
    return pl.pallas_call(
        functools.partial(_kv_kernel, tails=tails),
        grid=(bn, t_len // tm),
        in_specs=[pl.BlockSpec((None, tm, D_MODEL), lambda b, j: (b, j, 0)),
                  _const_spec((1, D_MODEL)),
                  _const_spec((D_MODEL, nout)),
                  _const_spec((nout, D_MODEL))],
        out_specs=[pl.BlockSpec((None, tm, nout), lambda b, j: (b, j, 0))] + tail_specs,
        out_shape=[jax.ShapeDtypeStruct((bn, t_len, nout), F32)] + tail_shapes,
        compiler_params=_params("parallel", "arbitrary"),
        name="kv_proj",
    )(x, g.reshape(1, D_MODEL), w, w_t)


def _oproj_kernel(a_ref, w_ref, h_ref, o_ref):
    o_ref[...] = h_ref[...] + jnp.dot(a_ref[...].astype(BF16), w_ref[...], preferred_element_type=F32)


def _oproj(a, w, layer, h, tm=512):
    n = h.shape[0]
    tm = min(tm, n)
    return pl.pallas_call(
        _oproj_kernel,
        grid=(n // tm,),
        in_specs=[pl.BlockSpec((tm, ATTN_WIDTH), lambda i: (i, 0)),
                  _layer_spec((ATTN_WIDTH, D_MODEL), layer),
                  pl.BlockSpec((tm, D_MODEL), lambda i: (i, 0))],
        out_specs=pl.BlockSpec((tm, D_MODEL), lambda i: (i, 0)),
        out_shape=jax.ShapeDtypeStruct((n, D_MODEL), F32),
        compiler_params=_params("parallel"),
        name="oproj",
    )(a, w, h)


def _alibi_slopes():
    n = N_DIL * HEADS_PER_GROUP
    s = jnp.exp2(-8.0 * jnp.arange(1, n + 1, dtype=F32) / n)
    return s.reshape(N_DIL, HEADS_PER_GROUP)


def _prompt_bias():
    qi = jnp.arange(R_LEN)[:, None]
    kj = jnp.arange(2 * R_LEN)[None, :]
    diff = R_LEN + qi - kj
    valid = (diff >= 0) & (diff <= R_LEN)
    dil = jnp.asarray(DIL_RATES, jnp.int32)[:, None, None]
    dist = (dil * diff[None]).astype(F32)
    bias = -_alibi_slopes()[:, :, None, None] * dist[:, None]
    return jnp.where(valid[None, None], bias, NEG)


def _sample_bias():
    kk = R_LEN - jnp.arange(R_LEN)
    dil = jnp.asarray(DIL_RATES, jnp.int32)[:, None]
    dist = (dil * kk[None]).astype(F32)
    return -_alibi_slopes()[:, :, None] * dist[:, None]


MERGE_ROWS = 256
INTERLEAVE = 16


def _attn_kernel(q0, q1, q2, k0, k1, k2, v0, v1, v2, bias_ref, o_ref, acc_s, m_s, l_s, *, t_len):
    qs, ks, vs = (q0, q1, q2), (k0, k1, k2), (v0, v1, v2)
    lo = lax.broadcasted_iota(jnp.int32, (R_LEN, LANES), 1) < HEAD_DIM

    def block(g, q_idx, k_idx, first):
        q = qs[g][q_idx, :]
        qb = jnp.concatenate([jnp.where(lo, q, 0.0), jnp.where(lo, 0.0, q)], axis=0).astype(BF16)
        kb = ks[g][k_idx, :].astype(BF16)
        vb = vs[g][k_idx, :].astype(BF16)
        s = lax.dot_general(qb, kb, (((1,), (1,)), ((), ())), preferred_element_type=F32)
        bias = bias_ref[g, :, :, R_LEN:] if first else bias_ref[g]
        s = s + bias.reshape(2 * R_LEN, bias.shape[-1])
        m = jnp.max(s, axis=-1, keepdims=True)
        p = jnp.exp(s - m)
        l = jnp.sum(p, axis=-1, keepdims=True)
        a = jnp.dot(p.astype(BF16), vb, preferred_element_type=F32)
        acc_s[g, q_idx, :] = jnp.where(lo, a[:R_LEN], a[R_LEN:])
        m_s[g, q_idx, :] = jnp.where(lo, m[:R_LEN], m[R_LEN:])
        l_s[g, q_idx, :] = jnp.where(lo, l[:R_LEN], l[R_LEN:])

    for g in range(N_DIL):
        dil = DIL_RATES[g]
        span = R_LEN * dil
        n_blocks = t_len // span

        def rows(start, n, dil=dil):
            if dil > 1:
                return pl.ds(start, n, stride=dil)
            return pl.ds(start if isinstance(start, int) else pl.multiple_of(start, R_LEN), n)

        if dil <= INTERLEAVE:
            for r in range(dil):
                block(g, rows(r, R_LEN), rows(r, R_LEN), True)
        else:
            def first_blocks(i, c, g=g, rows=rows):
                for j in range(INTERLEAVE):
                    r = i * INTERLEAVE + j
                    block(g, rows(r, R_LEN), rows(r, R_LEN), True)
                return c

            lax.fori_loop(0, dil // INTERLEAVE, first_blocks, 0)

        later = n_blocks - 1
        if later == 0:
            continue
        per_trip = max(w for w in range(1, max(INTERLEAVE // dil, 1) + 1) if later % w == 0)

        def later_blocks(i, c, g=g, dil=dil, span=span, rows=rows, per_trip=per_trip):
            for j in range(per_trip):
                nb = 1 + i * per_trip + j
                for r in range(dil):
                    start = nb * span + r
                    block(g, rows(start, R_LEN), rows(start - span, 2 * R_LEN), False)
            return c

        if later == per_trip:
            later_blocks(0, 0)
        else:
            lax.fori_loop(0, later // per_trip, later_blocks, 0)

    def merge(i, c):
        rows = pl.ds(pl.multiple_of(i * MERGE_ROWS, MERGE_ROWS), MERGE_ROWS)
        ms = [m_s[g, rows, :] for g in range(N_DIL)]
        mx = jnp.maximum(jnp.maximum(ms[0], ms[1]), ms[2])
        ws = [jnp.exp(m - mx) for m in ms]
        den = ws[0] * l_s[0, rows, :] + ws[1] * l_s[1, rows, :] + ws[2] * l_s[2, rows, :]
        num = ws[0] * acc_s[0, rows, :] + ws[1] * acc_s[1, rows, :] + ws[2] * acc_s[2, rows, :]
        o_ref[rows, :] = num / den
        return c

    lax.fori_loop(0, t_len // MERGE_ROWS, merge, 0)


def _attn_prompt(q, kv, bias):
    bn, t_len, _ = q.shape
    pairs = ATTN_WIDTH // LANES
    per_group = 2 * ATTN_WIDTH // LANES

    def col(off):
        return pl.BlockSpec((None, t_len, LANES), lambda b, hp: (b, 0, off + hp))

    in_specs = ([col(g * pairs) for g in range(N_DIL)]
                + [col(g * per_group) for g in range(N_DIL)]
                + [col(g * per_group + pairs) for g in range(N_DIL)]
                + [pl.BlockSpec((N_DIL, 2, R_LEN, 2 * R_LEN), lambda b, hp: (0, hp, 0, 0))])
    scratch = pltpu.VMEM((N_DIL, t_len, LANES), F32)
    return pl.pallas_call(
        functools.partial(_attn_kernel, t_len=t_len),
        grid=(bn, pairs),
        in_specs=in_specs,
        out_specs=pl.BlockSpec((None, t_len, LANES), lambda b, hp: (b, 0, hp)),
        out_shape=jax.ShapeDtypeStruct((bn, t_len, ATTN_WIDTH), F32),
        scratch_shapes=[scratch, scratch, scratch],
        compiler_params=_params("parallel", "parallel"),
        name="attn_prompt",
    )(q, q, q, kv, kv, kv, kv, kv, kv, bias)


SAMPLE_BLOCK = 8


def _window_kernel(c_ref, o_ref, *, dil):
    w = c_ref.shape[-1]
    pos = lax.broadcasted_iota(jnp.int32, (w, R_LEN), 0)
    col = lax.broadcasted_iota(jnp.int32, (w, R_LEN), 1)
    pick = jnp.where(pos == col * dil, 1.0, 0.0).astype(BF16)
    o_ref[...] = jnp.dot(c_ref[...].astype(BF16), pick, preferred_element_type=F32).astype(BF16)


def _strided_window(cache, dil):
    bd, w = cache.shape[:2]
    assert w == R_LEN * dil
    slab = jnp.transpose(cache, (0, 2, 3, 4, 1)).reshape(bd, 2 * ATTN_WIDTH, w)
    if dil == 1:
        return slab
    return pl.pallas_call(
        functools.partial(_window_kernel, dil=dil),
        grid=(bd,),
        in_specs=[pl.BlockSpec((None, 2 * ATTN_WIDTH, w), lambda i: (i, 0, 0))],
        out_specs=pl.BlockSpec((None, 2 * ATTN_WIDTH, R_LEN), lambda i: (i, 0, 0)),
        out_shape=jax.ShapeDtypeStruct((bd, 2 * ATTN_WIDTH, R_LEN), BF16),
        compiler_params=_params("parallel"),
        name="cache_window",
    )(slab)


def _attn_sample_kernel(q_ref, kvn_ref, c0, c1, c2, bias_ref, o_ref):
    windows = (c0, c1, c2)
    sub = lax.broadcasted_iota(jnp.int32, (HEADS_PER_GROUP, ATTN_WIDTH), 0)
    lane = lax.broadcasted_iota(jnp.int32, (HEADS_PER_GROUP, ATTN_WIDTH), 1)
    own = (lane // HEAD_DIM) == sub
    for b in range(SAMPLE_BLOCK):
        os_, ms, ls = [], [], []
        for g in range(N_DIL):
            base = g * 2 * ATTN_WIDTH
            qbd = jnp.where(own, q_ref[b:b + 1, g * ATTN_WIDTH:(g + 1) * ATTN_WIDTH], 0.0)
            kt = windows[g][b, :ATTN_WIDTH, :].astype(BF16)
            vt = windows[g][b, ATTN_WIDTH:, :].astype(BF16)
            kn = kvn_ref[b:b + 1, base:base + ATTN_WIDTH]
            vn = kvn_ref[b:b + 1, base + ATTN_WIDTH:base + 2 * ATTN_WIDTH]
            s = jnp.dot(qbd.astype(BF16), kt, preferred_element_type=F32) + bias_ref[g]
            s_new = jnp.sum(qbd * kn, axis=-1, keepdims=True)
            m = jnp.maximum(jnp.max(s, axis=-1, keepdims=True), s_new)
            p = jnp.exp(s - m)
            p_new = jnp.exp(s_new - m)
            ls.append(jnp.sum(p, axis=-1, keepdims=True) + p_new)
            os_.append(lax.dot_general(p.astype(BF16), vt, (((1,), (1,)), ((), ())),
                                       preferred_element_type=F32) + p_new * vn)
            ms.append(m)
        mx = jnp.maximum(jnp.maximum(ms[0], ms[1]), ms[2])
        ws = [jnp.exp(m - mx) for m in ms]
        den = ws[0] * ls[0] + ws[1] * ls[1] + ws[2] * ls[2]
        o = (ws[0] * os_[0] + ws[1] * os_[1] + ws[2] * os_[2]) / den
        o_ref[b:b + 1, :] = jnp.sum(jnp.where(own, o, 0.0), axis=0, keepdims=True)


def _attn_sample(q, kv_new, windows, bias):
    bd = q.shape[0]
    assert bd % SAMPLE_BLOCK == 0
    window_spec = pl.BlockSpec((SAMPLE_BLOCK, 2 * ATTN_WIDTH, R_LEN), lambda i: (i, 0, 0))
    return pl.pallas_call(
        _attn_sample_kernel,
        grid=(bd // SAMPLE_BLOCK,),
        in_specs=[pl.BlockSpec((SAMPLE_BLOCK, N_DIL * ATTN_WIDTH), lambda i: (i, 0)),
                  pl.BlockSpec((SAMPLE_BLOCK, N_DIL * 2 * ATTN_WIDTH), lambda i: (i, 0))]
                 + [window_spec] * N_DIL + [_const_spec((N_DIL, HEADS_PER_GROUP, R_LEN))],
        out_specs=pl.BlockSpec((SAMPLE_BLOCK, ATTN_WIDTH), lambda i: (i, 0)),
        out_shape=jax.ShapeDtypeStruct((bd, ATTN_WIDTH), F32),
        compiler_params=_params("parallel"),
        name="attn_sample",
    )(q, kv_new, *windows, bias)


def kernel(x_prompt, x_sample, state_ssm_re, state_ssm_im, cache_kv_w128, cache_kv_w512, cache_kv_w2048, norm_ffn1, ffn1_w_in, ffn1_w_out, norm_mix, norm_ffn2, ffn2_w_in, ffn2_w_out, norm_kv, norm_final, ssm_log_dt, ssm_lambda_re, ssm_lambda_im, ssm_b_re, ssm_b_im, ssm_c_re, ssm_c_im, ssm_d, glu_w, glu_b, attn_w_q, attn_w_kv, attn_w_o):
    bn, t_len, _ = x_prompt.shape
    bd = x_sample.shape[0]
    assert x_sample.shape[1] == 1

    glu_wb = glu_w.astype(BF16)
    w_q, w_kv, w_o = attn_w_q.astype(BF16), attn_w_kv.astype(BF16), attn_w_o.astype(BF16)
    q_scale = HEAD_DIM ** -0.5

    ssm = []
    for layer in range(N_A_LAYERS):
        ar, ai, bbr, bbi = _discretise(ssm_log_dt[layer], ssm_lambda_re[layer], ssm_lambda_im[layer],
                                       ssm_b_re[layer], ssm_b_im[layer])
        wb = jnp.concatenate([_block_diag_in(bbr), _block_diag_in(bbi)], axis=-1).astype(BF16)
        wcr = _block_diag_out(ssm_c_re[layer]).astype(BF16)
        wci = _block_diag_out(-ssm_c_im[layer]).astype(BF16)
        ssm.append((ar.reshape(1, STATE_W), ai.reshape(1, STATE_W), wb, wcr, wci,
                    ssm_d[layer].reshape(D_MODEL), glu_wb, glu_b[layer]))

    def states(parts, n_seq):
        return jnp.stack([p.reshape(n_seq, N_SSM_GROUPS, SSM_STATE) for p in parts])

    s0r = state_ssm_re.reshape(N_A_LAYERS, bd, STATE_W)
    s0i = state_ssm_im.reshape(N_A_LAYERS, bd, STATE_W)
    windows = [_strided_window(cache, dil)
               for cache, dil in zip((cache_kv_w128, cache_kv_w512, cache_kv_w2048), DIL_RATES)]
    sample_bias = _sample_bias()
    ffn1_w, ffn2_w, fin_r, fin_i = [], [], [], []
    hs = x_sample.reshape(bd, D_MODEL)
    for layer in range(DEPTH):
        if layer == N_A_LAYERS:
            kv_s = _proj(hs, norm_kv, w_kv)
        hs, w = _ffn_cast(hs, norm_ffn1[layer], ffn1_w_in, ffn1_w_out, layer)
        ffn1_w.append(w)
        if layer < N_A_LAYERS:
            hs, sr, si = _s5(hs.reshape(bd, 1, D_MODEL), norm_mix[layer], s0r[layer], s0i[layer],
                             *ssm[layer], layer=layer, tt=1)
            hs = hs.reshape(bd, D_MODEL)
            fin_r.append(sr)
            fin_i.append(si)
        else:
            b = layer - N_A_LAYERS
            q = _proj(hs, norm_mix[layer], w_q, layer=b, scale=q_scale)
            hs = _oproj(_attn_sample(q, kv_s, windows, sample_bias), w_o, b, hs)
        hs, w = _ffn_cast(hs, norm_ffn2[layer], ffn2_w_in, ffn2_w_out, layer,
                          final_g=norm_final if layer == DEPTH - 1 else None)
        ffn2_w.append(w)
    y_sample = hs.reshape(bd, 1, D_MODEL)
    ssm_re_s, ssm_im_s = states(fin_r, bd), states(fin_i, bd)
    kv_s = kv_s.reshape(bd, 1, N_DIL, 2, HEADS_PER_GROUP, HEAD_DIM)

    zeros = jnp.zeros((bn, STATE_W), F32)
    prompt_bias = _prompt_bias()
    fin_r, fin_i = [], []
    h = x_prompt.reshape(bn * t_len, D_MODEL)
    for layer in range(DEPTH):
        if layer == N_A_LAYERS:
            kv_p, *kv_tails = _kv_proj(h.reshape(bn, t_len, D_MODEL), norm_kv, w_kv, w_kv.T)
        if layer < N_A_LAYERS:
            h = _ffn(h, norm_ffn1[layer], ffn1_w[layer])
            h, sr, si = _s5(h.reshape(bn, t_len, D_MODEL), norm_mix[layer], zeros, zeros,
                            *ssm[layer], layer=layer, tt=64)
            h = h.reshape(bn * t_len, D_MODEL)
            fin_r.append(sr)
            fin_i.append(si)
            h = _ffn(h, norm_ffn2[layer], ffn2_w[layer])
        else:
            b = layer - N_A_LAYERS
            h, q = _ffn(h, norm_ffn1[layer], ffn1_w[layer], q_proj=(norm_mix[layer], w_q, b, q_scale))
            merged = _attn_prompt(q.reshape(bn, t_len, -1), kv_p, prompt_bias)
            h = _ffn(h, norm_ffn2[layer], ffn2_w[layer], attn=(merged.reshape(bn * t_len, ATTN_WIDTH), w_o, b),
                     final_g=norm_final if layer == DEPTH - 1 else None)
    y_prompt = h.reshape(bn, t_len, D_MODEL)
    ssm_re_p, ssm_im_p = states(fin_r, bn), states(fin_i, bn)
    kv_out_p = [jnp.transpose(tail.reshape(bn, 2, HEADS_PER_GROUP, HEAD_DIM, -1), (0, 4, 1, 2, 3))
                for tail in kv_tails]

    return (y_prompt, y_sample, ssm_re_p, ssm_im_p, kv_out_p[0], kv_out_p[1], kv_out_p[2],
            ssm_re_s, ssm_im_s, kv_s[:, :, 0], kv_s[:, :, 1], kv_s[:, :, 2])
```

```python
import functools

import jax
import jax.numpy as jnp
from jax import lax
from jax.experimental import pallas as pl
from jax.experimental.pallas import tpu as pltpu

F32 = jnp.float32
BF16 = jnp.bfloat16

D_MODEL = 1024
DEPTH = 4
N_A_LAYERS = DEPTH // 2
SSM_GROUP = 16
N_SSM_GROUPS = D_MODEL // SSM_GROUP
SSM_STATE = 64
STATE_W = N_SSM_GROUPS * SSM_STATE
DIL_WINDOWS = (128, 512, 2048)
DIL_RATES = (1, 4, 16)
N_DIL = 3
HEADS_PER_GROUP = 8
HEAD_DIM = 64
ATTN_WIDTH = HEADS_PER_GROUP * HEAD_DIM
R_LEN = DIL_WINDOWS[0] // DIL_RATES[0]
D_FF = 2816
EPS = 1e-6
NEG = -1e30

LANES = 128
MXU_TILE = 256
CH_BLOCKS = D_MODEL // LANES
GROUPS_PER_BLOCK = LANES // SSM_GROUP
STATE_BLOCK = GROUPS_PER_BLOCK * SSM_STATE
VMEM_LIMIT_BYTES = 56 * 1024 * 1024

assert all(w // d == R_LEN for w, d in zip(DIL_WINDOWS, DIL_RATES))


def _params(*semantics):
    return pltpu.CompilerParams(dimension_semantics=semantics, vmem_limit_bytes=VMEM_LIMIT_BYTES)


def _const_spec(shape):
    zeros = (0,) * len(shape)
    return pl.BlockSpec(shape, lambda *_: zeros, pipeline_mode=pl.Buffered(1))


def _layer_spec(shape, layer):
    if layer is None:
        return _const_spec(shape)
    idx = (layer,) + (0,) * len(shape)
    return pl.BlockSpec((None,) + tuple(shape), lambda *_: idx, pipeline_mode=pl.Buffered(1))


def _rms(x, g):
    return x * lax.rsqrt(jnp.mean(x * x, axis=-1, keepdims=True) + EPS) * g


def _ff_chunks(n_chunks):
    tiles = D_FF // MXU_TILE
    assert tiles * MXU_TILE == D_FF and n_chunks <= tiles
    cuts = [MXU_TILE * ((tiles * c + n_chunks - 1) // n_chunks) for c in range(n_chunks + 1)]
    return list(zip(cuts[:-1], cuts[1:]))


def _swiglu_residual(x, g, wa_ref, wb_ref, wout_ref, n_chunks):
    xn = _rms(x, g).astype(BF16)
    acc = None
    for lo, hi in _ff_chunks(n_chunks):
        a = jnp.dot(xn, wa_ref[:, lo:hi], preferred_element_type=F32)
        b = jnp.dot(xn, wb_ref[:, lo:hi], preferred_element_type=F32)
        act = (a * jax.nn.sigmoid(a) * b).astype(BF16)
        d = jnp.dot(act, wout_ref[lo:hi, :], preferred_element_type=F32)
        acc = d if acc is None else acc + d
    return x + 0.5 * acc


def _ffn_kernel(*refs, n_chunks, attn_in, final, q_scale):
    refs = list(refs)
    x = refs.pop(0)[...]
    if attn_in:
        a_ref, wo_ref = refs.pop(0), refs.pop(0)
        x = x + jnp.dot(a_ref[...].astype(BF16), wo_ref[...], preferred_element_type=F32)
    g_ref, wa_ref, wb_ref, wout_ref = (refs.pop(0) for _ in range(4))
    y = _swiglu_residual(x, g_ref[...], wa_ref, wb_ref, wout_ref, n_chunks)
    if final:
        y = _rms(y, refs.pop(0)[...])
    if q_scale is not None:
        gq_ref, wq_ref = refs.pop(0), refs.pop(0)
    refs.pop(0)[...] = y
    if q_scale is not None:
        q = jnp.dot(_rms(y, gq_ref[...]).astype(BF16), wq_ref[...], preferred_element_type=F32)
        refs.pop(0)[...] = q * q_scale


def _ffn(x, g, weights, attn=None, final_g=None, q_proj=None, tm=512, n_chunks=2):
    n = x.shape[0]
    tm = min(tm, n)
    assert n % tm == 0
    rows = lambda width: pl.BlockSpec((tm, width), lambda i: (i, 0))
    in_specs, args = [rows(D_MODEL)], [x]
    if attn is not None:
        merged, w_o, idx = attn
        in_specs += [rows(ATTN_WIDTH), _layer_spec((ATTN_WIDTH, D_MODEL), idx)]
        args += [merged, w_o]
    in_specs += [_const_spec((1, D_MODEL)), _const_spec((D_MODEL, D_FF)), _const_spec((D_MODEL, D_FF)),
                 _const_spec((D_FF, D_MODEL))]
    args += [g.reshape(1, D_MODEL), *weights]
    if final_g is not None:
        in_specs.append(_const_spec((1, D_MODEL)))
        args.append(final_g.reshape(1, D_MODEL))
    out_specs, out_shape = [rows(D_MODEL)], [jax.ShapeDtypeStruct((n, D_MODEL), F32)]
    q_scale = None
    if q_proj is not None:
        gq, w_q, idx, q_scale = q_proj
        nq = w_q.shape[-1]
        in_specs += [_const_spec((1, D_MODEL)), _layer_spec((D_MODEL, nq), idx)]
        args += [gq.reshape(1, D_MODEL), w_q]
        out_specs.append(rows(nq))
        out_shape.append(jax.ShapeDtypeStruct((n, nq), F32))
    out = pl.pallas_call(
        functools.partial(_ffn_kernel, n_chunks=n_chunks, attn_in=attn is not None,
                          final=final_g is not None, q_scale=q_scale),
        grid=(n // tm,),
        in_specs=in_specs,
        out_specs=out_specs,
        out_shape=out_shape,
        compiler_params=_params("parallel"),
        name="ffn",
    )(*args)
    return out if q_proj is not None else out[0]


def _ffn_cast_kernel(x_ref, g_ref, wa_ref, wb_ref, wout_ref, *rest, final):
    if final:
        gf_ref, o_ref, wa_o, wb_o, wout_o, acc_s = rest
    else:
        o_ref, wa_o, wb_o, wout_o, acc_s = rest
    c = pl.program_id(0)
    wa = wa_ref[...].astype(BF16)
    wb = wb_ref[...].astype(BF16)
    wout = wout_ref[...].astype(BF16)
    wa_o[...] = wa
    wb_o[...] = wb
    wout_o[...] = wout
    x = x_ref[...]
    xn = _rms(x, g_ref[...]).astype(BF16)
    a = jnp.dot(xn, wa, preferred_element_type=F32)
    b = jnp.dot(xn, wb, preferred_element_type=F32)
    d = jnp.dot((a * jax.nn.sigmoid(a) * b).astype(BF16), wout, preferred_element_type=F32)

    @pl.when(c == 0)
    def _():
        acc_s[...] = d

    @pl.when(c > 0)
    def _():
        acc_s[...] += d

    @pl.when(c == pl.num_programs(0) - 1)
    def _():
        y = x + 0.5 * acc_s[...]
        o_ref[...] = _rms(y, gf_ref[...]) if final else y


def _ffn_cast(x, g, w_in, w_out, layer, final_g=None):
    n = x.shape[0]
    n_chunks = D_FF // MXU_TILE
    final = final_g is not None
    in_specs = [_const_spec((n, D_MODEL)), _const_spec((1, D_MODEL)),
                pl.BlockSpec((None, D_MODEL, MXU_TILE), lambda c: (layer, 0, c)),
                pl.BlockSpec((None, D_MODEL, MXU_TILE), lambda c: (layer, 0, n_chunks + c)),
                pl.BlockSpec((None, MXU_TILE, D_MODEL), lambda c: (layer, c, 0))]
    args = [x, g.reshape(1, D_MODEL), w_in, w_in, w_out]
    if final:
        in_specs.append(_const_spec((1, D_MODEL)))
        args.append(final_g.reshape(1, D_MODEL))
    y, wa, wb, wo = pl.pallas_call(
        functools.partial(_ffn_cast_kernel, final=final),
        grid=(n_chunks,),
        in_specs=in_specs,
        out_specs=[pl.BlockSpec((n, D_MODEL), lambda c: (0, 0)),
                   pl.BlockSpec((D_MODEL, MXU_TILE), lambda c: (0, c)),
                   pl.BlockSpec((D_MODEL, MXU_TILE), lambda c: (0, c)),
                   pl.BlockSpec((MXU_TILE, D_MODEL), lambda c: (c, 0))],
        out_shape=[jax.ShapeDtypeStruct((n, D_MODEL), F32),
                   jax.ShapeDtypeStruct((D_MODEL, D_FF), BF16),
                   jax.ShapeDtypeStruct((D_MODEL, D_FF), BF16),
                   jax.ShapeDtypeStruct((D_FF, D_MODEL), BF16)],
        scratch_shapes=[pltpu.VMEM((n, D_MODEL), F32)],
        compiler_params=_params("arbitrary"),
        name="ffn_cast",
    )(*args)
    return y, (wa, wb, wo)


def _disc_kernel(ldt_ref, lr_ref, li_ref, br_ref, bi_ref, cr_ref, ci_ref,
                 ar_ref, ai_ref, bbr_ref, bbi_ref, abr_ref, abi_ref,
                 car_ref, cai_ref, ca2r_ref, ca2i_ref, k0_ref, k1_ref):
    dt = jnp.exp(ldt_ref[...])
    lr = lr_ref[...]
    li = li_ref[...]
    mag = jnp.exp(lr * dt)
    ar = mag * jnp.cos(li * dt)
    ai = mag * jnp.sin(li * dt)
    den = lr * lr + li * li
    zr = ((ar - 1.0) * lr + ai * li) / den
    zi = (ai * lr - (ar - 1.0) * li) / den
    ar_ref[...] = ar
    ai_ref[...] = ai
    br = br_ref[...]
    bi = bi_ref[...]
    bbr = zr[:, None, :] * br - zi[:, None, :] * bi
    bbi = zr[:, None, :] * bi + zi[:, None, :] * br
    bbr_ref[...] = bbr
    bbi_ref[...] = bbi
    a1r, a1i = ar[:, None, :], ai[:, None, :]
    a2r, a2i = (ar * ar - ai * ai)[:, None, :], (2.0 * ar * ai)[:, None, :]
    abr_ref[...] = a1r * bbr - a1i * bbi
    abi_ref[...] = a1r * bbi + a1i * bbr
    cr = cr_ref[...]
    ci = ci_ref[...]
    car = cr * a1r - ci * a1i
    cai = cr * a1i + ci * a1r
    car_ref[...] = car
    cai_ref[...] = cai
    ca2r_ref[...] = cr * a2r - ci * a2i
    ca2i_ref[...] = cr * a2i + ci * a2r

    def re_dot(xr, xi):
        dims = (((2,), (2,)), ((0,), (0,)))
        return (lax.dot_general(xr, bbr, dims, preferred_element_type=F32)
                - lax.dot_general(xi, bbi, dims, preferred_element_type=F32))

    k0_ref[...] = re_dot(cr, ci)
    k1_ref[...] = re_dot(car, cai)


def _discretise(log_dt, lam_re, lam_im, b_re, b_im, c_re, c_im):
    g, p, hg = N_SSM_GROUPS, SSM_STATE, SSM_GROUP
    gp = jax.ShapeDtypeStruct((g, p), F32)
    ghp = jax.ShapeDtypeStruct((g, hg, p), F32)
    ghh = jax.ShapeDtypeStruct((g, hg, hg), F32)
    return pl.pallas_call(
        _disc_kernel,
        out_shape=(gp, gp) + (ghp,) * 8 + (ghh, ghh),
        name="s5_disc",
    )(log_dt.reshape(g, 1), lam_re, lam_im, jnp.swapaxes(b_re, 1, 2), jnp.swapaxes(b_im, 1, 2), c_re, c_im)


def _block_diag_in(bb):
    x = bb.reshape(CH_BLOCKS, GROUPS_PER_BLOCK, SSM_GROUP, SSM_STATE)
    eye = jnp.eye(GROUPS_PER_BLOCK, dtype=bb.dtype)
    return jnp.einsum('kghp,gj->kghjp', x, eye).reshape(CH_BLOCKS, LANES, STATE_BLOCK)


def _block_diag_out(c):
    x = c.reshape(CH_BLOCKS, GROUPS_PER_BLOCK, SSM_GROUP, SSM_STATE)
    eye = jnp.eye(GROUPS_PER_BLOCK, dtype=c.dtype)
    return jnp.einsum('kghp,gj->kgpjh', x, eye).reshape(CH_BLOCKS, STATE_BLOCK, LANES)


def _block_diag_mix(k):
    x = k.reshape(CH_BLOCKS, GROUPS_PER_BLOCK, SSM_GROUP, SSM_GROUP)
    eye = jnp.eye(GROUPS_PER_BLOCK, dtype=k.dtype)
    return jnp.einsum('kgoi,gj->kgijo', x, eye).reshape(CH_BLOCKS, LANES, LANES)


def _pair_weights(bbr, bbi, abr, abi, car, cai, ca2r, ca2i, k0, k1):
    t0 = jnp.concatenate([_block_diag_in(abr), _block_diag_in(abi)], axis=-1)
    t1 = jnp.concatenate([_block_diag_in(bbr), _block_diag_in(bbi)], axis=-1)
    w_in = jnp.concatenate([t0, t1], axis=1)
    w_out_r = jnp.concatenate([_block_diag_out(car), _block_diag_out(ca2r)], axis=-1)
    w_out_i = jnp.concatenate([_block_diag_out(-cai), _block_diag_out(-ca2i)], axis=-1)
    m0, m1 = _block_diag_mix(k0), _block_diag_mix(k1)
    w_loc = jnp.concatenate([jnp.concatenate([m0, m1], axis=-1),
                             jnp.concatenate([jnp.zeros_like(m0), m0], axis=-1)], axis=1)
    return tuple(w.astype(BF16) for w in (w_in, w_out_r, w_out_i, w_loc))


SCAN_COLS = 512


def _s5_kernel(h_ref, g_ref, s0r_ref, s0i_ref, ar_ref, ai_ref, wb_ref, wcr_ref, wci_ref, d_ref,
               gw_ref, gb_ref, o_ref, sr_ref, si_ref, u_s, xr_s, xi_s, *, rb, tt):
    n_slab, per_slab = h_ref.shape[:2]
    assert n_slab * per_slab == rb * tt and (n_slab == rb or tt == 1)

    def seq_rows(s):
        return pl.ds(s, per_slab, stride=n_slab) if n_slab > 1 else pl.ds(0, per_slab)

    @pl.when(pl.program_id(0) == 0)
    def _():
        sr_ref[...] = s0r_ref[...]
        si_ref[...] = s0i_ref[...]

    for s in range(n_slab):
        un = _rms(h_ref[s], g_ref[...])
        for k in range(CH_BLOCKS):
            u_s[k, seq_rows(s), :] = un[:, k * LANES:(k + 1) * LANES]
    for k in range(CH_BLOCKS):
        bu = jnp.dot(u_s[k].astype(BF16), wb_ref[k], preferred_element_type=F32)
        xr_s[:, k * STATE_BLOCK:(k + 1) * STATE_BLOCK] = bu[:, :STATE_BLOCK]
        xi_s[:, k * STATE_BLOCK:(k + 1) * STATE_BLOCK] = bu[:, STATE_BLOCK:]

    for c in range(STATE_W // SCAN_COLS):
        cs = slice(c * SCAN_COLS, (c + 1) * SCAN_COLS)
        a_r = jnp.broadcast_to(ar_ref[:, cs], (rb, SCAN_COLS))
        a_i = jnp.broadcast_to(ai_ref[:, cs], (rb, SCAN_COLS))

        def step(t, carry, cs=cs, a_r=a_r, a_i=a_i):
            s_r, s_i = carry
            rows = pl.ds(pl.multiple_of(t * rb, rb), rb)
            n_r = a_r * s_r - a_i * s_i + xr_s[rows, cs]
            n_i = a_r * s_i + a_i * s_r + xi_s[rows, cs]
            xr_s[rows, cs] = n_r
            xi_s[rows, cs] = n_i
            return n_r, n_i

        s_r, s_i = lax.fori_loop(0, tt, step, (sr_ref[:, cs], si_ref[:, cs]), unroll=True)
        sr_ref[:, cs] = s_r
        si_ref[:, cs] = s_i

    ys = []
    for k in range(CH_BLOCKS):
        ks = slice(k * STATE_BLOCK, (k + 1) * STATE_BLOCK)
        ys.append(jnp.dot(xr_s[:, ks].astype(BF16), wcr_ref[k], preferred_element_type=F32)
                  + jnp.dot(xi_s[:, ks].astype(BF16), wci_ref[k], preferred_element_type=F32)
                  + d_ref[:, k * LANES:(k + 1) * LANES] * u_s[k])
    act = jax.nn.gelu(jnp.concatenate(ys, axis=1)).astype(BF16)
    z = jnp.dot(act, gw_ref[...], preferred_element_type=F32) + gb_ref[...]
    mix = z[:, :D_MODEL] * jax.nn.sigmoid(z[:, D_MODEL:])
    for k in range(CH_BLOCKS):
        u_s[k] = mix[:, k * LANES:(k + 1) * LANES]
    for s in range(n_slab):
        o_ref[s] = h_ref[s] + jnp.concatenate([u_s[k, seq_rows(s), :] for k in range(CH_BLOCKS)], axis=1)


def _s5(h, g, s0r, s0i, ar, ai, wb, wcr, wci, d_skip, glu_w, glu_b, *, layer, tt):
    rb, t_len, _ = h.shape
    assert t_len % tt == 0 and rb % 8 == 0
    rows = rb * tt
    block = (1, rb, D_MODEL) if t_len == 1 else (rb, tt, D_MODEL)
    h_in = h.reshape(1, rb, D_MODEL) if t_len == 1 else h
    state = jax.ShapeDtypeStruct((rb, STATE_W), F32)
    out, sr, si = pl.pallas_call(
        functools.partial(_s5_kernel, rb=rb, tt=tt),
        grid=(t_len // tt,),
        in_specs=[pl.BlockSpec(block, lambda i: (0, i, 0)),
                  _const_spec((1, D_MODEL)),
                  _const_spec((rb, STATE_W)), _const_spec((rb, STATE_W)),
                  _const_spec((1, STATE_W)), _const_spec((1, STATE_W)),
                  _const_spec((CH_BLOCKS, LANES, 2 * STATE_BLOCK)),
                  _const_spec((CH_BLOCKS, STATE_BLOCK, LANES)),
                  _const_spec((CH_BLOCKS, STATE_BLOCK, LANES)),
                  _const_spec((1, D_MODEL)),
                  _layer_spec((D_MODEL, 2 * D_MODEL), layer),
                  _const_spec((1, 2 * D_MODEL))],
        out_specs=(pl.BlockSpec(block, lambda i: (0, i, 0)),
                   pl.BlockSpec((rb, STATE_W), lambda i: (0, 0)),
                   pl.BlockSpec((rb, STATE_W), lambda i: (0, 0))),
        out_shape=(jax.ShapeDtypeStruct(h_in.shape, F32), state, state),
        scratch_shapes=[pltpu.VMEM((CH_BLOCKS, rows, LANES), F32),
                        pltpu.VMEM((rows, STATE_W), F32), pltpu.VMEM((rows, STATE_W), F32)],
        compiler_params=_params("arbitrary"),
        name="s5",
    )(h_in, g.reshape(1, D_MODEL), s0r, s0i, ar, ai, wb, wcr, wci, d_skip.reshape(1, D_MODEL),
      glu_w, glu_b.reshape(1, 2 * D_MODEL))
    return out.reshape(h.shape), sr, si


def _s5_pair_kernel(h_ref, g_ref, ar_ref, ai_ref, win_ref, wor_ref, woi_ref, wl_ref, d_ref,
                    gw_ref, gb_ref, o_ref, sr_ref, si_ref, u_s, xr_s, xi_s, *, rb, tt):
    pairs = tt // 2
    prow = pairs * rb

    def seq_rows(s):
        return pl.ds(s, tt, stride=rb)

    def split(x):
        x3 = x.reshape(pairs, 2 * rb, x.shape[-1])
        return x3[:, :rb].reshape(prow, x.shape[-1]), x3[:, rb:].reshape(prow, x.shape[-1])

    def interleave(even, odd):
        n = even.shape[-1]
        return jnp.concatenate([even.reshape(pairs, rb, n), odd.reshape(pairs, rb, n)], axis=1).reshape(tt * rb, n)

    @pl.when(pl.program_id(0) == 0)
    def _():
        sr_ref[...] = jnp.zeros_like(sr_ref)
        si_ref[...] = jnp.zeros_like(si_ref)

    for s in range(rb):
        un = _rms(h_ref[s], g_ref[...])
        for k in range(CH_BLOCKS):
            u_s[k, seq_rows(s), :] = un[:, k * LANES:(k + 1) * LANES]

    lhs = []
    for k in range(CH_BLOCKS):
        ks = slice(k * STATE_BLOCK, (k + 1) * STATE_BLOCK)
        lhs.append(jnp.concatenate(split(u_s[k]), axis=1).astype(BF16))
        w = jnp.dot(lhs[k], win_ref[k], preferred_element_type=F32)
        xr_s[:, ks] = w[:, :STATE_BLOCK]
        xi_s[:, ks] = w[:, STATE_BLOCK:]

    for c in range(STATE_W // SCAN_COLS):
        cs = slice(c * SCAN_COLS, (c + 1) * SCAN_COLS)
        a_r, a_i = ar_ref[:, cs], ai_ref[:, cs]
        a2_r = jnp.broadcast_to(a_r * a_r - a_i * a_i, (rb, SCAN_COLS))
        a2_i = jnp.broadcast_to(2.0 * a_r * a_i, (rb, SCAN_COLS))

        def step(j, carry, cs=cs, a2_r=a2_r, a2_i=a2_i):
            s_r, s_i = carry
            rows = pl.ds(pl.multiple_of(j * rb, rb), rb)
            n_r = a2_r * s_r - a2_i * s_i + xr_s[rows, cs]
            n_i = a2_r * s_i + a2_i * s_r + xi_s[rows, cs]
            xr_s[rows, cs] = s_r
            xi_s[rows, cs] = s_i
            return n_r, n_i

        s_r, s_i = lax.fori_loop(0, pairs, step, (sr_ref[:, cs], si_ref[:, cs]), unroll=True)
        sr_ref[:, cs] = s_r
        si_ref[:, cs] = s_i

    ys = []
    for k in range(CH_BLOCKS):
        ks = slice(k * STATE_BLOCK, (k + 1) * STATE_BLOCK)
        y2 = (jnp.dot(xr_s[:, ks].astype(BF16), wor_ref[k], preferred_element_type=F32)
              + jnp.dot(xi_s[:, ks].astype(BF16), woi_ref[k], preferred_element_type=F32)
              + jnp.dot(lhs[k], wl_ref[k], preferred_element_type=F32))
        ys.append(interleave(y2[:, :LANES], y2[:, LANES:]) + d_ref[:, k * LANES:(k + 1) * LANES] * u_s[k])
    act = jax.nn.gelu(jnp.concatenate(ys, axis=1)).astype(BF16)
    z = jnp.dot(act, gw_ref[...], preferred_element_type=F32) + gb_ref[...]
    mix = z[:, :D_MODEL] * jax.nn.sigmoid(z[:, D_MODEL:])
    for k in range(CH_BLOCKS):
        u_s[k] = mix[:, k * LANES:(k + 1) * LANES]
    for s in range(rb):
        o_ref[s] = h_ref[s] + jnp.concatenate([u_s[k, seq_rows(s), :] for k in range(CH_BLOCKS)], axis=1)


def _s5_pairs(h, g, ar, ai, w_in, w_out_r, w_out_i, w_loc, d_skip, glu_w, glu_b, *, layer, tt):
    rb, t_len, _ = h.shape
    assert t_len % tt == 0 and tt % 2 == 0 and rb % 8 == 0
    rows, prow = rb * tt, rb * tt // 2
    block = (rb, tt, D_MODEL)
    state = jax.ShapeDtypeStruct((rb, STATE_W), F32)
    return pl.pallas_call(
        functools.partial(_s5_pair_kernel, rb=rb, tt=tt),
        grid=(t_len // tt,),
        in_specs=[pl.BlockSpec(block, lambda i: (0, i, 0)),
                  _const_spec((1, D_MODEL)),
                  _const_spec((1, STATE_W)), _const_spec((1, STATE_W)),
                  _const_spec((CH_BLOCKS, 2 * LANES, 2 * STATE_BLOCK)),
                  _const_spec((CH_BLOCKS, STATE_BLOCK, 2 * LANES)),
                  _const_spec((CH_BLOCKS, STATE_BLOCK, 2 * LANES)),
                  _const_spec((CH_BLOCKS, 2 * LANES, 2 * LANES)),
                  _const_spec((1, D_MODEL)),
                  _layer_spec((D_MODEL, 2 * D_MODEL), layer),
                  _const_spec((1, 2 * D_MODEL))],
        out_specs=(pl.BlockSpec(block, lambda i: (0, i, 0)),
                   pl.BlockSpec((rb, STATE_W), lambda i: (0, 0)),
                   pl.BlockSpec((rb, STATE_W), lambda i: (0, 0))),
        out_shape=(jax.ShapeDtypeStruct(h.shape, F32), state, state),
        scratch_shapes=[pltpu.VMEM((CH_BLOCKS, rows, LANES), F32),
                        pltpu.VMEM((prow, STATE_W), F32), pltpu.VMEM((prow, STATE_W), F32)],
        compiler_params=_params("arbitrary"),
        name="s5_pairs",
    )(h, g.reshape(1, D_MODEL), ar, ai, w_in, w_out_r, w_out_i, w_loc, d_skip.reshape(1, D_MODEL),
      glu_w, glu_b.reshape(1, 2 * D_MODEL))


def _proj_kernel(x_ref, g_ref, w_ref, o_ref, *, scale):
    xn = _rms(x_ref[...], g_ref[...]).astype(BF16)
    y = jnp.dot(xn, w_ref[...], preferred_element_type=F32)
    o_ref[...] = y if scale == 1.0 else y * scale


def _proj(x, g, w, layer=None, scale=1.0, tm=512):
    n = x.shape[0]
    tm = min(tm, n)
    nout = w.shape[-1]
    return pl.pallas_call(
        functools.partial(_proj_kernel, scale=scale),
        grid=(n // tm,),
        in_specs=[pl.BlockSpec((tm, D_MODEL), lambda i: (i, 0)),
                  _const_spec((1, D_MODEL)),
                  _layer_spec((D_MODEL, nout), layer)],
        out_specs=pl.BlockSpec((tm, nout), lambda i: (i, 0)),
        out_shape=jax.ShapeDtypeStruct((n, nout), F32),
        compiler_params=_params("parallel"),
        name="proj",
    )(x, g.reshape(1, D_MODEL), w)


def _kv_tails(t_len, tm):
    tails = []
    for w in DIL_WINDOWS:
        n_tail = min(w, t_len)
        if n_tail >= tm:
            assert n_tail % tm == 0
            tails.append(((t_len - n_tail) // tm, 0, tm))
        else:
            tails.append((t_len // tm - 1, tm - n_tail, n_tail))
    return tails


def _kv_kernel(x_ref, g_ref, w_ref, wt_ref, kv_ref, *tail_refs, tails):
    j = pl.program_id(1)
    xn = _rms(x_ref[...], g_ref[...]).astype(BF16)
    kv_ref[...] = jnp.dot(xn, w_ref[...], preferred_element_type=F32)
    for g, (first_tile, row0, n_tok) in enumerate(tails):
        def emit(g=g, row0=row0, n_tok=n_tok):
            wt = wt_ref[g * 2 * ATTN_WIDTH:(g + 1) * 2 * ATTN_WIDTH, :]
            tail_refs[g][...] = lax.dot_general(wt, xn[row0:row0 + n_tok], (((1,), (1,)), ((), ())),
                                                preferred_element_type=F32)

        if first_tile == 0:
            emit()
        else:
            pl.when(j >= first_tile)(emit)


def _kv_proj(x, g, w, w_t, tm=512):
    bn, t_len, _ = x.shape
    tm = min(tm, t_len)
    assert t_len % tm == 0
    nout = w.shape[-1]
    tails = _kv_tails(t_len, tm)
    tail_specs, tail_shapes = [], []
    for (first_tile, _, n_tok), win in zip(tails, DIL_WINDOWS):
        tail_specs.append(pl.BlockSpec((None, 2 * ATTN_WIDTH, n_tok),
                                       lambda b, j, f=first_tile: (b, 0, jnp.maximum(j - f, 0))))
        tail_shapes.append(jax.ShapeDtypeStruct((bn, 2 * ATTN_WIDTH, min(win, t_len)), F32))
    return pl.pallas_call(
        functools.partial(_kv_kernel, tails=tails),
        grid=(bn, t_len // tm),
        in_specs=[pl.BlockSpec((None, tm, D_MODEL), lambda b, j: (b, j, 0)),
                  _const_spec((1, D_MODEL)),
                  _const_spec((D_MODEL, nout)),
                  _const_spec((nout, D_MODEL))],
        out_specs=[pl.BlockSpec((None, tm, nout), lambda b, j: (b, j, 0))] + tail_specs,
        out_shape=[jax.ShapeDtypeStruct((bn, t_len, nout), F32)] + tail_shapes,
        compiler_params=_params("parallel", "arbitrary"),
        name="kv_proj",
    )(x, g.reshape(1, D_MODEL), w, w_t)


def _oproj_kernel(a_ref, w_ref, h_ref, o_ref):
    o_ref[...] = h_ref[...] + jnp.dot(a_ref[...].astype(BF16), w_ref[...], preferred_element_type=F32)


def _oproj(a, w, layer, h, tm=512):
    n = h.shape[0]
    tm = min(tm, n)
    return pl.pallas_call(
        _oproj_kernel,
        grid=(n // tm,),
        in_specs=[pl.BlockSpec((tm, ATTN_WIDTH), lambda i: (i, 0)),
                  _layer_spec((ATTN_WIDTH, D_MODEL), layer),
                  pl.BlockSpec((tm, D_MODEL), lambda i: (i, 0))],
        out_specs=pl.BlockSpec((tm, D_MODEL), lambda i: (i, 0)),
        out_shape=jax.ShapeDtypeStruct((n, D_MODEL), F32),
        compiler_params=_params("parallel"),
        name="oproj",
    )(a, w, h)


def _alibi_slopes():
    n = N_DIL * HEADS_PER_GROUP
    s = jnp.exp2(-8.0 * jnp.arange(1, n + 1, dtype=F32) / n)
    return s.reshape(N_DIL, HEADS_PER_GROUP)


def _prompt_bias():
    qi = jnp.arange(R_LEN)[:, None]
    kj = jnp.arange(2 * R_LEN)[None, :]
    diff = R_LEN + qi - kj
    valid = (diff >= 0) & (diff <= R_LEN)
    dil = jnp.asarray(DIL_RATES, jnp.int32)[:, None, None]
    dist = (dil * diff[None]).astype(F32)
    bias = -_alibi_slopes()[:, :, None, None] * dist[:, None]
    return jnp.where(valid[None, None], bias, NEG)


def _sample_bias():
    kk = R_LEN - jnp.arange(R_LEN)
    dil = jnp.asarray(DIL_RATES, jnp.int32)[:, None]
    dist = (dil * kk[None]).astype(F32)
    return -_alibi_slopes()[:, :, None] * dist[:, None]


MERGE_ROWS = 256
INTERLEAVE = 16


def _attn_kernel(q0, q1, q2, k0, k1, k2, v0, v1, v2, bias_ref, o_ref, acc_s, m_s, l_s, *, t_len):
    qs, ks, vs = (q0, q1, q2), (k0, k1, k2), (v0, v1, v2)
    lo = lax.broadcasted_iota(jnp.int32, (R_LEN, LANES), 1) < HEAD_DIM

    def block(g, q_idx, k_idx, first):
        q = qs[g][q_idx, :]
        qb = jnp.concatenate([jnp.where(lo, q, 0.0), jnp.where(lo, 0.0, q)], axis=0).astype(BF16)
        kb = ks[g][k_idx, :].astype(BF16)
        vb = vs[g][k_idx, :].astype(BF16)
        s = lax.dot_general(qb, kb, (((1,), (1,)), ((), ())), preferred_element_type=F32)
        bias = bias_ref[g, :, :, R_LEN:] if first else bias_ref[g]
        s = s + bias.reshape(2 * R_LEN, bias.shape[-1])
        m = jnp.max(s, axis=-1, keepdims=True)
        p = jnp.exp(s - m)
        l = jnp.sum(p, axis=-1, keepdims=True)
        a = jnp.dot(p.astype(BF16), vb, preferred_element_type=F32)
        acc_s[g, q_idx, :] = jnp.where(lo, a[:R_LEN], a[R_LEN:])
        m_s[g, q_idx, :] = jnp.where(lo, m[:R_LEN], m[R_LEN:])
        l_s[g, q_idx, :] = jnp.where(lo, l[:R_LEN], l[R_LEN:])

    for g in range(N_DIL):
        dil = DIL_RATES[g]
        span = R_LEN * dil
        n_blocks = t_len // span

        def rows(start, n, dil=dil):
            if dil > 1:
                return pl.ds(start, n, stride=dil)
            return pl.ds(start if isinstance(start, int) else pl.multiple_of(start, R_LEN), n)

        if dil <= INTERLEAVE:
            for r in range(dil):
                block(g, rows(r, R_LEN), rows(r, R_LEN), True)
        else:
            def first_blocks(i, c, g=g, rows=rows):
                for j in range(INTERLEAVE):
                    r = i * INTERLEAVE + j
                    block(g, rows(r, R_LEN), rows(r, R_LEN), True)
                return c

            lax.fori_loop(0, dil // INTERLEAVE, first_blocks, 0)

        later = n_blocks - 1
        if later == 0:
            continue
        per_trip = max(w for w in range(1, max(INTERLEAVE // dil, 1) + 1) if later % w == 0)

        def later_blocks(i, c, g=g, dil=dil, span=span, rows=rows, per_trip=per_trip):
            for j in range(per_trip):
                nb = 1 + i * per_trip + j
                for r in range(dil):
                    start = nb * span + r
                    block(g, rows(start, R_LEN), rows(start - span, 2 * R_LEN), False)
            return c

        if later == per_trip:
            later_blocks(0, 0)
        else:
            lax.fori_loop(0, later // per_trip, later_blocks, 0)

    def merge(i, c):
        rows = pl.ds(pl.multiple_of(i * MERGE_ROWS, MERGE_ROWS), MERGE_ROWS)
        ms = [m_s[g, rows, :] for g in range(N_DIL)]
        mx = jnp.maximum(jnp.maximum(ms[0], ms[1]), ms[2])
        ws = [jnp.exp(m - mx) for m in ms]
        den = ws[0] * l_s[0, rows, :] + ws[1] * l_s[1, rows, :] + ws[2] * l_s[2, rows, :]
        num = ws[0] * acc_s[0, rows, :] + ws[1] * acc_s[1, rows, :] + ws[2] * acc_s[2, rows, :]
        o_ref[rows, :] = num / den
        return c

    lax.fori_loop(0, t_len // MERGE_ROWS, merge, 0)


def _attn_prompt(q, kv, bias):
    bn, t_len, _ = q.shape
    pairs = ATTN_WIDTH // LANES
    per_group = 2 * ATTN_WIDTH // LANES

    def col(off):
        return pl.BlockSpec((None, t_len, LANES), lambda b, hp: (b, 0, off + hp))

    in_specs = ([col(g * pairs) for g in range(N_DIL)]
                + [col(g * per_group) for g in range(N_DIL)]
                + [col(g * per_group + pairs) for g in range(N_DIL)]
                + [pl.BlockSpec((N_DIL, 2, R_LEN, 2 * R_LEN), lambda b, hp: (0, hp, 0, 0))])
    scratch = pltpu.VMEM((N_DIL, t_len, LANES), F32)
    return pl.pallas_call(
        functools.partial(_attn_kernel, t_len=t_len),
        grid=(bn, pairs),
        in_specs=in_specs,
        out_specs=pl.BlockSpec((None, t_len, LANES), lambda b, hp: (b, 0, hp)),
        out_shape=jax.ShapeDtypeStruct((bn, t_len, ATTN_WIDTH), F32),
        scratch_shapes=[scratch, scratch, scratch],
        compiler_params=_params("parallel", "parallel"),
        name="attn_prompt",
    )(q, q, q, kv, kv, kv, kv, kv, kv, bias)


SAMPLE_BLOCK = 8


def _window_kernel(c_ref, o_ref, *, dil):
    w = c_ref.shape[-1]
    pos = lax.broadcasted_iota(jnp.int32, (w, R_LEN), 0)
    col = lax.broadcasted_iota(jnp.int32, (w, R_LEN), 1)
    pick = jnp.where(pos == col * dil, 1.0, 0.0).astype(BF16)
    o_ref[...] = jnp.dot(c_ref[...].astype(BF16), pick, preferred_element_type=F32).astype(BF16)


def _strided_window(cache, dil):
    bd, w = cache.shape[:2]
    assert w == R_LEN * dil
    slab = jnp.transpose(cache, (0, 2, 3, 4, 1)).reshape(bd, 2 * ATTN_WIDTH, w)
    if dil == 1:
        return slab
    return pl.pallas_call(
        functools.partial(_window_kernel, dil=dil),
        grid=(bd,),
        in_specs=[pl.BlockSpec((None, 2 * ATTN_WIDTH, w), lambda i: (i, 0, 0))],
        out_specs=pl.BlockSpec((None, 2 * ATTN_WIDTH, R_LEN), lambda i: (i, 0, 0)),
        out_shape=jax.ShapeDtypeStruct((bd, 2 * ATTN_WIDTH, R_LEN), BF16),
        compiler_params=_params("parallel"),
        name="cache_window",
    )(slab)


def _attn_sample_kernel(q_ref, kvn_ref, c0, c1, c2, bias_ref, o_ref):
    windows = (c0, c1, c2)
    sub = lax.broadcasted_iota(jnp.int32, (HEADS_PER_GROUP, ATTN_WIDTH), 0)
    lane = lax.broadcasted_iota(jnp.int32, (HEADS_PER_GROUP, ATTN_WIDTH), 1)
    own = (lane // HEAD_DIM) == sub
    for b in range(SAMPLE_BLOCK):
        os_, ms, ls = [], [], []
        for g in range(N_DIL):
            base = g * 2 * ATTN_WIDTH
            qbd = jnp.where(own, q_ref[b:b + 1, g * ATTN_WIDTH:(g + 1) * ATTN_WIDTH], 0.0)
            kt = windows[g][b, :ATTN_WIDTH, :].astype(BF16)
            vt = windows[g][b, ATTN_WIDTH:, :].astype(BF16)
            kn = kvn_ref[b:b + 1, base:base + ATTN_WIDTH]
            vn = kvn_ref[b:b + 1, base + ATTN_WIDTH:base + 2 * ATTN_WIDTH]
            s = jnp.dot(qbd.astype(BF16), kt, preferred_element_type=F32) + bias_ref[g]
            s_new = jnp.sum(qbd * kn, axis=-1, keepdims=True)
            m = jnp.maximum(jnp.max(s, axis=-1, keepdims=True), s_new)
            p = jnp.exp(s - m)
            p_new = jnp.exp(s_new - m)
            ls.append(jnp.sum(p, axis=-1, keepdims=True) + p_new)
            os_.append(lax.dot_general(p.astype(BF16), vt, (((1,), (1,)), ((), ())),
                                       preferred_element_type=F32) + p_new * vn)
            ms.append(m)
        mx = jnp.maximum(jnp.maximum(ms[0], ms[1]), ms[2])
        ws = [jnp.exp(m - mx) for m in ms]
        den = ws[0] * ls[0] + ws[1] * ls[1] + ws[2] * ls[2]
        o = (ws[0] * os_[0] + ws[1] * os_[1] + ws[2] * os_[2]) / den
        o_ref[b:b + 1, :] = jnp.sum(jnp.where(own, o, 0.0), axis=0, keepdims=True)


def _attn_sample(q, kv_new, windows, bias):
    bd = q.shape[0]
    assert bd % SAMPLE_BLOCK == 0
    window_spec = pl.BlockSpec((SAMPLE_BLOCK, 2 * ATTN_WIDTH, R_LEN), lambda i: (i, 0, 0))
    return pl.pallas_call(
        _attn_sample_kernel,
        grid=(bd // SAMPLE_BLOCK,),
        in_specs=[pl.BlockSpec((SAMPLE_BLOCK, N_DIL * ATTN_WIDTH), lambda i: (i, 0)),
                  pl.BlockSpec((SAMPLE_BLOCK, N_DIL * 2 * ATTN_WIDTH), lambda i: (i, 0))]
                 + [window_spec] * N_DIL + [_const_spec((N_DIL, HEADS_PER_GROUP, R_LEN))],
        out_specs=pl.BlockSpec((SAMPLE_BLOCK, ATTN_WIDTH), lambda i: (i, 0)),
        out_shape=jax.ShapeDtypeStruct((bd, ATTN_WIDTH), F32),
        compiler_params=_params("parallel"),
        name="attn_sample",
    )(q, kv_new, *windows, bias)


def kernel(x_prompt, x_sample, state_ssm_re, state_ssm_im, cache_kv_w128, cache_kv_w512, cache_kv_w2048, norm_ffn1, ffn1_w_in, ffn1_w_out, norm_mix, norm_ffn2, ffn2_w_in, ffn2_w_out, norm_kv, norm_final, ssm_log_dt, ssm_lambda_re, ssm_lambda_im, ssm_b_re, ssm_b_im, ssm_c_re, ssm_c_im, ssm_d, glu_w, glu_b, attn_w_q, attn_w_kv, attn_w_o):
    bn, t_len, _ = x_prompt.shape
    bd = x_sample.shape[0]
    assert x_sample.shape[1] == 1

    glu_wb = glu_w.astype(BF16)
    w_q, w_kv, w_o = attn_w_q.astype(BF16), attn_w_kv.astype(BF16), attn_w_o.astype(BF16)
    q_scale = HEAD_DIM ** -0.5

    ssm, ssm_pairs = [], []
    for layer in range(N_A_LAYERS):
        ar, ai, bbr, bbi, *products = _discretise(
            ssm_log_dt[layer], ssm_lambda_re[layer], ssm_lambda_im[layer],
            ssm_b_re[layer], ssm_b_im[layer], ssm_c_re[layer], ssm_c_im[layer])
        ar, ai = ar.reshape(1, STATE_W), ai.reshape(1, STATE_W)
        tail = (ssm_d[layer].reshape(D_MODEL), glu_wb, glu_b[layer])
        wb = jnp.concatenate([_block_diag_in(bbr), _block_diag_in(bbi)], axis=-1).astype(BF16)
        wcr = _block_diag_out(ssm_c_re[layer]).astype(BF16)
        wci = _block_diag_out(-ssm_c_im[layer]).astype(BF16)
        ssm.append((ar, ai, wb, wcr, wci) + tail)
        ssm_pairs.append((ar, ai) + _pair_weights(bbr, bbi, *products) + tail)

    def states(parts, n_seq):
        return jnp.stack([p.reshape(n_seq, N_SSM_GROUPS, SSM_STATE) for p in parts])

    s0r = state_ssm_re.reshape(N_A_LAYERS, bd, STATE_W)
    s0i = state_ssm_im.reshape(N_A_LAYERS, bd, STATE_W)
    windows = [_strided_window(cache, dil)
               for cache, dil in zip((cache_kv_w128, cache_kv_w512, cache_kv_w2048), DIL_RATES)]
    sample_bias = _sample_bias()
    ffn1_w, ffn2_w, fin_r, fin_i = [], [], [], []
    hs = x_sample.reshape(bd, D_MODEL)
    for layer in range(DEPTH):
        if layer == N_A_LAYERS:
            kv_s = _proj(hs, norm_kv, w_kv)
        hs, w = _ffn_cast(hs, norm_ffn1[layer], ffn1_w_in, ffn1_w_out, layer)
        ffn1_w.append(w)
        if layer < N_A_LAYERS:
            hs, sr, si = _s5(hs.reshape(bd, 1, D_MODEL), norm_mix[layer], s0r[layer], s0i[layer],
                             *ssm[layer], layer=layer, tt=1)
            hs = hs.reshape(bd, D_MODEL)
            fin_r.append(sr)
            fin_i.append(si)
        else:
            b = layer - N_A_LAYERS
            q = _proj(hs, norm_mix[layer], w_q, layer=b, scale=q_scale)
            hs = _oproj(_attn_sample(q, kv_s, windows, sample_bias), w_o, b, hs)
        hs, w = _ffn_cast(hs, norm_ffn2[layer], ffn2_w_in, ffn2_w_out, layer,
                          final_g=norm_final if layer == DEPTH - 1 else None)
        ffn2_w.append(w)
    y_sample = hs.reshape(bd, 1, D_MODEL)
    ssm_re_s, ssm_im_s = states(fin_r, bd), states(fin_i, bd)
    kv_s = kv_s.reshape(bd, 1, N_DIL, 2, HEADS_PER_GROUP, HEAD_DIM)

    prompt_bias = _prompt_bias()
    fin_r, fin_i = [], []
    h = x_prompt.reshape(bn * t_len, D_MODEL)
    for layer in range(DEPTH):
        if layer == N_A_LAYERS:
            kv_p, *kv_tails = _kv_proj(h.reshape(bn, t_len, D_MODEL), norm_kv, w_kv, w_kv.T)
        if layer < N_A_LAYERS:
            h = _ffn(h, norm_ffn1[layer], ffn1_w[layer])
            h, sr, si = _s5_pairs(h.reshape(bn, t_len, D_MODEL), norm_mix[layer],
                                  *ssm_pairs[layer], layer=layer, tt=64)
            h = h.reshape(bn * t_len, D_MODEL)
            fin_r.append(sr)
            fin_i.append(si)
            h = _ffn(h, norm_ffn2[layer], ffn2_w[layer])
        else:
            b = layer - N_A_LAYERS
            h, q = _ffn(h, norm_ffn1[layer], ffn1_w[layer], q_proj=(norm_mix[layer], w_q, b, q_scale))
            merged = _attn_prompt(q.reshape(bn, t_len, -1), kv_p, prompt_bias)
            h = _ffn(h, norm_ffn2[layer], ffn2_w[layer], attn=(merged.reshape(bn * t_len, ATTN_WIDTH), w_o, b),
                     final_g=norm_final if layer == DEPTH - 1 else None)
    y_prompt = h.reshape(bn, t_len, D_MODEL)
    ssm_re_p, ssm_im_p = states(fin_r, bn), states(fin_i, bn)
    kv_out_p = [jnp.transpose(tail.reshape(bn, 2, HEADS_PER_GROUP, HEAD_DIM, -1), (0, 4, 1, 2, 3))
                for tail in kv_tails]

    return (y_prompt, y_sample, ssm_re_p, ssm_im_p, kv_out_p[0], kv_out_p[1], kv_out_p[2],
            ssm_re_s, ssm_im_s, kv_s[:, :, 0], kv_s[:, :, 1], kv_s[:, :, 2])
```

```python
import functools

import jax
import jax.numpy as jnp
from jax import lax
from jax.experimental import pallas as pl
from jax.experimental.pallas import tpu as pltpu

F32 = jnp.float32
BF16 = jnp.bfloat16

D_MODEL = 1024
DEPTH = 4
N_A_LAYERS = DEPTH // 2
SSM_GROUP = 16
N_SSM_GROUPS = D_MODEL // SSM_GROUP
SSM_STATE = 64
STATE_W = N_SSM_GROUPS * SSM_STATE
DIL_WINDOWS = (128, 512, 2048)
DIL_RATES = (1, 4, 16)
N_DIL = 3
HEADS_PER_GROUP = 8
HEAD_DIM = 64
ATTN_WIDTH = HEADS_PER_GROUP * HEAD_DIM
R_LEN = DIL_WINDOWS[0] // DIL_RATES[0]
D_FF = 2816
EPS = 1e-6
NEG = -1e30

LANES = 128
MXU_TILE = 256
CH_BLOCKS = D_MODEL // LANES
GROUPS_PER_BLOCK = LANES // SSM_GROUP
STATE_BLOCK = GROUPS_PER_BLOCK * SSM_STATE
VMEM_LIMIT_BYTES = 56 * 1024 * 1024

assert all(w // d == R_LEN for w, d in zip(DIL_WINDOWS, DIL_RATES))


def _params(*semantics):
    return pltpu.CompilerParams(dimension_semantics=semantics, vmem_limit_bytes=VMEM_LIMIT_BYTES)


def _const_spec(shape):
    zeros = (0,) * len(shape)
    return pl.BlockSpec(shape, lambda *_: zeros, pipeline_mode=pl.Buffered(1))


def _layer_spec(shape, layer):
    if layer is None:
        return _const_spec(shape)
    idx = (layer,) + (0,) * len(shape)
    return pl.BlockSpec((None,) + tuple(shape), lambda *_: idx, pipeline_mode=pl.Buffered(1))


def _rms(x, g):
    return x * lax.rsqrt(jnp.mean(x * x, axis=-1, keepdims=True) + EPS) * g


def _ff_chunks(n_chunks):
    tiles = D_FF // MXU_TILE
    assert tiles * MXU_TILE == D_FF and n_chunks <= tiles
    cuts = [MXU_TILE * ((tiles * c + n_chunks - 1) // n_chunks) for c in range(n_chunks + 1)]
    return list(zip(cuts[:-1], cuts[1:]))


def _swiglu_residual(x, g, wa_ref, wb_ref, wout_ref, n_chunks):
    xn = _rms(x, g).astype(BF16)
    acc = None
    for lo, hi in _ff_chunks(n_chunks):
        a = jnp.dot(xn, wa_ref[:, lo:hi], preferred_element_type=F32)
        b = jnp.dot(xn, wb_ref[:, lo:hi], preferred_element_type=F32)
        act = (a * jax.nn.sigmoid(a) * b).astype(BF16)
        d = jnp.dot(act, wout_ref[lo:hi, :], preferred_element_type=F32)
        acc = d if acc is None else acc + d
    return x + 0.5 * acc


def _ffn_kernel(*refs, n_chunks, attn_in, final, q_scale):
    refs = list(refs)
    x = refs.pop(0)[...]
    if attn_in:
        a_ref, wo_ref = refs.pop(0), refs.pop(0)
        x = x + jnp.dot(a_ref[...].astype(BF16), wo_ref[...], preferred_element_type=F32)
    g_ref, wa_ref, wb_ref, wout_ref = (refs.pop(0) for _ in range(4))
    y = _swiglu_residual(x, g_ref[...], wa_ref, wb_ref, wout_ref, n_chunks)
    if final:
        y = _rms(y, refs.pop(0)[...])
    if q_scale is not None:
        gq_ref, wq_ref = refs.pop(0), refs.pop(0)
    refs.pop(0)[...] = y
    if q_scale is not None:
        q = jnp.dot(_rms(y, gq_ref[...]).astype(BF16), wq_ref[...], preferred_element_type=F32)
        refs.pop(0)[...] = q * q_scale


def _ffn(x, g, weights, attn=None, final_g=None, q_proj=None, tm=512, n_chunks=2):
    n = x.shape[0]
    tm = min(tm, n)
    assert n % tm == 0
    rows = lambda width: pl.BlockSpec((tm, width), lambda i: (i, 0))
    in_specs, args = [rows(D_MODEL)], [x]
    if attn is not None:
        merged, w_o, idx = attn
        in_specs += [rows(ATTN_WIDTH), _layer_spec((ATTN_WIDTH, D_MODEL), idx)]
        args += [merged, w_o]
    in_specs += [_const_spec((1, D_MODEL)), _const_spec((D_MODEL, D_FF)), _const_spec((D_MODEL, D_FF)),
                 _const_spec((D_FF, D_MODEL))]
    args += [g.reshape(1, D_MODEL), *weights]
    if final_g is not None:
        in_specs.append(_const_spec((1, D_MODEL)))
        args.append(final_g.reshape(1, D_MODEL))
    out_specs, out_shape = [rows(D_MODEL)], [jax.ShapeDtypeStruct((n, D_MODEL), F32)]
    q_scale = None
    if q_proj is not None:
        gq, w_q, idx, q_scale = q_proj
        nq = w_q.shape[-1]
        in_specs += [_const_spec((1, D_MODEL)), _layer_spec((D_MODEL, nq), idx)]
        args += [gq.reshape(1, D_MODEL), w_q]
        out_specs.append(rows(nq))
        out_shape.append(jax.ShapeDtypeStruct((n, nq), F32))
    out = pl.pallas_call(
        functools.partial(_ffn_kernel, n_chunks=n_chunks, attn_in=attn is not None,
                          final=final_g is not None, q_scale=q_scale),
        grid=(n // tm,),
        in_specs=in_specs,
        out_specs=out_specs,
        out_shape=out_shape,
        compiler_params=_params("parallel"),
        name="ffn",
    )(*args)
    return out if q_proj is not None else out[0]


def _ffn_cast_kernel(x_ref, g_ref, wa_ref, wb_ref, wout_ref, *rest, final):
    if final:
        gf_ref, o_ref, wa_o, wb_o, wout_o, acc_s = rest
    else:
        o_ref, wa_o, wb_o, wout_o, acc_s = rest
    c = pl.program_id(0)
    wa = wa_ref[...].astype(BF16)
    wb = wb_ref[...].astype(BF16)
    wout = wout_ref[...].astype(BF16)
    wa_o[...] = wa
    wb_o[...] = wb
    wout_o[...] = wout
    x = x_ref[...]
    xn = _rms(x, g_ref[...]).astype(BF16)
    a = jnp.dot(xn, wa, preferred_element_type=F32)
    b = jnp.dot(xn, wb, preferred_element_type=F32)
    d = jnp.dot((a * jax.nn.sigmoid(a) * b).astype(BF16), wout, preferred_element_type=F32)

    @pl.when(c == 0)
    def _():
        acc_s[...] = d

    @pl.when(c > 0)
    def _():
        acc_s[...] += d

    @pl.when(c == pl.num_programs(0) - 1)
    def _():
        y = x + 0.5 * acc_s[...]
        o_ref[...] = _rms(y, gf_ref[...]) if final else y


def _ffn_cast(x, g, w_in, w_out, layer, final_g=None):
    n = x.shape[0]
    n_chunks = D_FF // MXU_TILE
    final = final_g is not None
    in_specs = [_const_spec((n, D_MODEL)), _const_spec((1, D_MODEL)),
                pl.BlockSpec((None, D_MODEL, MXU_TILE), lambda c: (layer, 0, c)),
                pl.BlockSpec((None, D_MODEL, MXU_TILE), lambda c: (layer, 0, n_chunks + c)),
                pl.BlockSpec((None, MXU_TILE, D_MODEL), lambda c: (layer, c, 0))]
    args = [x, g.reshape(1, D_MODEL), w_in, w_in, w_out]
    if final:
        in_specs.append(_const_spec((1, D_MODEL)))
        args.append(final_g.reshape(1, D_MODEL))
    y, wa, wb, wo = pl.pallas_call(
        functools.partial(_ffn_cast_kernel, final=final),
        grid=(n_chunks,),
        in_specs=in_specs,
        out_specs=[pl.BlockSpec((n, D_MODEL), lambda c: (0, 0)),
                   pl.BlockSpec((D_MODEL, MXU_TILE), lambda c: (0, c)),
                   pl.BlockSpec((D_MODEL, MXU_TILE), lambda c: (0, c)),
                   pl.BlockSpec((MXU_TILE, D_MODEL), lambda c: (c, 0))],
        out_shape=[jax.ShapeDtypeStruct((n, D_MODEL), F32),
                   jax.ShapeDtypeStruct((D_MODEL, D_FF), BF16),
                   jax.ShapeDtypeStruct((D_MODEL, D_FF), BF16),
                   jax.ShapeDtypeStruct((D_FF, D_MODEL), BF16)],
        scratch_shapes=[pltpu.VMEM((n, D_MODEL), F32)],
        compiler_params=_params("arbitrary"),
        name="ffn_cast",
    )(*args)
    return y, (wa, wb, wo)


def _ssm_weights_kernel(ldt_ref, lr_ref, li_ref, br_ref, bi_ref, cr_ref, ci_ref,
                        ar_ref, ai_ref, wb_ref, wcr_ref, wci_ref, win_ref, wor_ref, woi_ref, wl_ref):
    dt = jnp.exp(ldt_ref[...])
    lr = lr_ref[...]
    li = li_ref[...]
    mag = jnp.exp(lr * dt)
    ar = mag * jnp.cos(li * dt)
    ai = mag * jnp.sin(li * dt)
    den = lr * lr + li * li
    zr = ((ar - 1.0) * lr + ai * li) / den
    zi = (ai * lr - (ar - 1.0) * li) / den
    ar_ref[...] = ar
    ai_ref[...] = ai
    br = br_ref[...]
    bi = bi_ref[...]
    bbr = zr[:, None, :] * br - zi[:, None, :] * bi
    bbi = zr[:, None, :] * bi + zi[:, None, :] * br
    a1r, a1i = ar[:, None, :], ai[:, None, :]
    a2r, a2i = (ar * ar - ai * ai)[:, None, :], (2.0 * ar * ai)[:, None, :]
    abr = a1r * bbr - a1i * bbi
    abi = a1r * bbi + a1i * bbr
    cr = cr_ref[...]
    ci = ci_ref[...]
    car = cr * a1r - ci * a1i
    cai = cr * a1i + ci * a1r
    ca2r = cr * a2r - ci * a2i
    ca2i = cr * a2i + ci * a2r

    def re_dot(xr, xi):
        dims = (((2,), (2,)), ((0,), (0,)))
        return (lax.dot_general(xr, bbr, dims, preferred_element_type=F32)
                - lax.dot_general(xi, bbi, dims, preferred_element_type=F32))

    k0 = re_dot(cr, ci)
    k1 = re_dot(car, cai)

    def same_group(n_rows, row_span, n_cols, col_span):
        r = lax.broadcasted_iota(jnp.int32, (n_rows, n_cols), 0) // row_span
        c = lax.broadcasted_iota(jnp.int32, (n_rows, n_cols), 1) // col_span
        return r == c

    def replicate(n, copies, transpose):
        shape = (copies * n, n) if transpose else (n, copies * n)
        r = lax.broadcasted_iota(jnp.int32, shape, 0)
        c = lax.broadcasted_iota(jnp.int32, shape, 1)
        return jnp.where(r % n == c % n, 1.0, 0.0).astype(BF16)

    gpb = GROUPS_PER_BLOCK
    rep_p = replicate(SSM_STATE, gpb, False)
    rep_pt = replicate(SSM_STATE, gpb, True)
    rep_ht = replicate(SSM_GROUP, gpb, True)
    m_in = same_group(LANES, SSM_GROUP, STATE_BLOCK, SSM_STATE)
    m_out = same_group(STATE_BLOCK, SSM_STATE, LANES, SSM_GROUP)
    m_mix = same_group(LANES, SSM_GROUP, LANES, SSM_GROUP)
    nt = (((1,), (1,)), ((), ()))

    def rows_of(x, k):
        return x[k * gpb:(k + 1) * gpb].reshape(LANES, x.shape[-1]).astype(BF16)

    def bd_in(x, k):
        return jnp.where(m_in, jnp.dot(rows_of(x, k), rep_p, preferred_element_type=F32), 0.0)

    def bd_out(x, k):
        return jnp.where(m_out, lax.dot_general(rep_pt, rows_of(x, k), nt, preferred_element_type=F32), 0.0)

    def bd_mix(x, k):
        return jnp.where(m_mix, lax.dot_general(rep_ht, rows_of(x, k), nt, preferred_element_type=F32), 0.0)

    for k in range(CH_BLOCKS):
        one_step = jnp.concatenate([bd_in(bbr, k), bd_in(bbi, k)], axis=1)
        wb_ref[k] = one_step.astype(BF16)
        wcr_ref[k] = bd_out(cr, k).astype(BF16)
        wci_ref[k] = (-bd_out(ci, k)).astype(BF16)
        win_ref[k] = jnp.concatenate([jnp.concatenate([bd_in(abr, k), bd_in(abi, k)], axis=1), one_step],
                                     axis=0).astype(BF16)
        wor_ref[k] = jnp.concatenate([bd_out(car, k), bd_out(ca2r, k)], axis=1).astype(BF16)
        woi_ref[k] = (-jnp.concatenate([bd_out(cai, k), bd_out(ca2i, k)], axis=1)).astype(BF16)
        m0, m1 = bd_mix(k0, k), bd_mix(k1, k)
        wl_ref[k] = jnp.concatenate([jnp.concatenate([m0, m1], axis=1),
                                     jnp.concatenate([jnp.zeros_like(m0), m0], axis=1)], axis=0).astype(BF16)


def _ssm_weights(log_dt, lam_re, lam_im, b_re, b_im, c_re, c_im):
    g, p = N_SSM_GROUPS, SSM_STATE
    gp = jax.ShapeDtypeStruct((g, p), F32)
    stack = lambda rows, cols: jax.ShapeDtypeStruct((CH_BLOCKS, rows, cols), BF16)
    return pl.pallas_call(
        _ssm_weights_kernel,
        out_shape=(gp, gp,
                   stack(LANES, 2 * STATE_BLOCK), stack(STATE_BLOCK, LANES), stack(STATE_BLOCK, LANES),
                   stack(2 * LANES, 2 * STATE_BLOCK), stack(STATE_BLOCK, 2 * LANES), stack(STATE_BLOCK, 2 * LANES),
                   stack(2 * LANES, 2 * LANES)),
        compiler_params=pltpu.CompilerParams(vmem_limit_bytes=VMEM_LIMIT_BYTES),
        name="s5_weights",
    )(log_dt.reshape(g, 1), lam_re, lam_im, jnp.swapaxes(b_re, 1, 2), jnp.swapaxes(b_im, 1, 2), c_re, c_im)


SCAN_COLS = 512


def _s5_kernel(h_ref, g_ref, s0r_ref, s0i_ref, ar_ref, ai_ref, wb_ref, wcr_ref, wci_ref, d_ref,
               gw_ref, gb_ref, o_ref, sr_ref, si_ref, u_s, xr_s, xi_s, *, rb, tt):
    n_slab, per_slab = h_ref.shape[:2]
    assert n_slab * per_slab == rb * tt and (n_slab == rb or tt == 1)

    def seq_rows(s):
        return pl.ds(s, per_slab, stride=n_slab) if n_slab > 1 else pl.ds(0, per_slab)

    @pl.when(pl.program_id(0) == 0)
    def _():
        sr_ref[...] = s0r_ref[...]
        si_ref[...] = s0i_ref[...]

    for s in range(n_slab):
        un = _rms(h_ref[s], g_ref[...])
        for k in range(CH_BLOCKS):
            u_s[k, seq_rows(s), :] = un[:, k * LANES:(k + 1) * LANES]
    for k in range(CH_BLOCKS):
        bu = jnp.dot(u_s[k].astype(BF16), wb_ref[k], preferred_element_type=F32)
        xr_s[:, k * STATE_BLOCK:(k + 1) * STATE_BLOCK] = bu[:, :STATE_BLOCK]
        xi_s[:, k * STATE_BLOCK:(k + 1) * STATE_BLOCK] = bu[:, STATE_BLOCK:]

    for c in range(STATE_W // SCAN_COLS):
        cs = slice(c * SCAN_COLS, (c + 1) * SCAN_COLS)
        a_r = jnp.broadcast_to(ar_ref[:, cs], (rb, SCAN_COLS))
        a_i = jnp.broadcast_to(ai_ref[:, cs], (rb, SCAN_COLS))

        def step(t, carry, cs=cs, a_r=a_r, a_i=a_i):
            s_r, s_i = carry
            rows = pl.ds(pl.multiple_of(t * rb, rb), rb)
            n_r = a_r * s_r - a_i * s_i + xr_s[rows, cs]
            n_i = a_r * s_i + a_i * s_r + xi_s[rows, cs]
            xr_s[rows, cs] = n_r
            xi_s[rows, cs] = n_i
            return n_r, n_i

        s_r, s_i = lax.fori_loop(0, tt, step, (sr_ref[:, cs], si_ref[:, cs]), unroll=True)
        sr_ref[:, cs] = s_r
        si_ref[:, cs] = s_i

    ys = []
    for k in range(CH_BLOCKS):
        ks = slice(k * STATE_BLOCK, (k + 1) * STATE_BLOCK)
        ys.append(jnp.dot(xr_s[:, ks].astype(BF16), wcr_ref[k], preferred_element_type=F32)
                  + jnp.dot(xi_s[:, ks].astype(BF16), wci_ref[k], preferred_element_type=F32)
                  + d_ref[:, k * LANES:(k + 1) * LANES] * u_s[k])
    act = jax.nn.gelu(jnp.concatenate(ys, axis=1)).astype(BF16)
    z = jnp.dot(act, gw_ref[...], preferred_element_type=F32) + gb_ref[...]
    mix = z[:, :D_MODEL] * jax.nn.sigmoid(z[:, D_MODEL:])
    for k in range(CH_BLOCKS):
        u_s[k] = mix[:, k * LANES:(k + 1) * LANES]
    for s in range(n_slab):
        o_ref[s] = h_ref[s] + jnp.concatenate([u_s[k, seq_rows(s), :] for k in range(CH_BLOCKS)], axis=1)


def _s5(h, g, s0r, s0i, ar, ai, wb, wcr, wci, d_skip, glu_w, glu_b, *, layer, tt):
    rb, t_len, _ = h.shape
    assert t_len % tt == 0 and rb % 8 == 0
    rows = rb * tt
    block = (1, rb, D_MODEL) if t_len == 1 else (rb, tt, D_MODEL)
    h_in = h.reshape(1, rb, D_MODEL) if t_len == 1 else h
    state = jax.ShapeDtypeStruct((rb, STATE_W), F32)
    out, sr, si = pl.pallas_call(
        functools.partial(_s5_kernel, rb=rb, tt=tt),
        grid=(t_len // tt,),
        in_specs=[pl.BlockSpec(block, lambda i: (0, i, 0)),
                  _const_spec((1, D_MODEL)),
                  _const_spec((rb, STATE_W)), _const_spec((rb, STATE_W)),
                  _const_spec((1, STATE_W)), _const_spec((1, STATE_W)),
                  _const_spec((CH_BLOCKS, LANES, 2 * STATE_BLOCK)),
                  _const_spec((CH_BLOCKS, STATE_BLOCK, LANES)),
                  _const_spec((CH_BLOCKS, STATE_BLOCK, LANES)),
                  _const_spec((1, D_MODEL)),
                  _layer_spec((D_MODEL, 2 * D_MODEL), layer),
                  _const_spec((1, 2 * D_MODEL))],
        out_specs=(pl.BlockSpec(block, lambda i: (0, i, 0)),
                   pl.BlockSpec((rb, STATE_W), lambda i: (0, 0)),
                   pl.BlockSpec((rb, STATE_W), lambda i: (0, 0))),
        out_shape=(jax.ShapeDtypeStruct(h_in.shape, F32), state, state),
        scratch_shapes=[pltpu.VMEM((CH_BLOCKS, rows, LANES), F32),
                        pltpu.VMEM((rows, STATE_W), F32), pltpu.VMEM((rows, STATE_W), F32)],
        compiler_params=_params("arbitrary"),
        name="s5",
    )(h_in, g.reshape(1, D_MODEL), s0r, s0i, ar, ai, wb, wcr, wci, d_skip.reshape(1, D_MODEL),
      glu_w, glu_b.reshape(1, 2 * D_MODEL))
    return out.reshape(h.shape), sr, si


def _s5_pair_kernel(h_ref, g_ref, ar_ref, ai_ref, win_ref, wor_ref, woi_ref, wl_ref, d_ref,
                    gw_ref, gb_ref, o_ref, sr_ref, si_ref, u_s, xr_s, xi_s, *, rb, tt):
    pairs = tt // 2
    prow = pairs * rb

    def seq_rows(s):
        return pl.ds(s, tt, stride=rb)

    def split(x):
        x3 = x.reshape(pairs, 2 * rb, x.shape[-1])
        return x3[:, :rb].reshape(prow, x.shape[-1]), x3[:, rb:].reshape(prow, x.shape[-1])

    def interleave(even, odd):
        n = even.shape[-1]
        return jnp.concatenate([even.reshape(pairs, rb, n), odd.reshape(pairs, rb, n)], axis=1).reshape(tt * rb, n)

    @pl.when(pl.program_id(0) == 0)
    def _():
        sr_ref[...] = jnp.zeros_like(sr_ref)
        si_ref[...] = jnp.zeros_like(si_ref)

    for s in range(rb):
        un = _rms(h_ref[s], g_ref[...])
        for k in range(CH_BLOCKS):
            u_s[k, seq_rows(s), :] = un[:, k * LANES:(k + 1) * LANES]

    lhs = []
    for k in range(CH_BLOCKS):
        ks = slice(k * STATE_BLOCK, (k + 1) * STATE_BLOCK)
        lhs.append(jnp.concatenate(split(u_s[k]), axis=1).astype(BF16))
        w = jnp.dot(lhs[k], win_ref[k], preferred_element_type=F32)
        xr_s[:, ks] = w[:, :STATE_BLOCK]
        xi_s[:, ks] = w[:, STATE_BLOCK:]

    for c in range(STATE_W // SCAN_COLS):
        cs = slice(c * SCAN_COLS, (c + 1) * SCAN_COLS)
        a_r, a_i = ar_ref[:, cs], ai_ref[:, cs]
        a2_r = jnp.broadcast_to(a_r * a_r - a_i * a_i, (rb, SCAN_COLS))
        a2_i = jnp.broadcast_to(2.0 * a_r * a_i, (rb, SCAN_COLS))

        def step(j, carry, cs=cs, a2_r=a2_r, a2_i=a2_i):
            s_r, s_i = carry
            rows = pl.ds(pl.multiple_of(j * rb, rb), rb)
            n_r = a2_r * s_r - a2_i * s_i + xr_s[rows, cs]
            n_i = a2_r * s_i + a2_i * s_r + xi_s[rows, cs]
            xr_s[rows, cs] = s_r
            xi_s[rows, cs] = s_i
            return n_r, n_i

        s_r, s_i = lax.fori_loop(0, pairs, step, (sr_ref[:, cs], si_ref[:, cs]), unroll=True)
        sr_ref[:, cs] = s_r
        si_ref[:, cs] = s_i

    ys = []
    for k in range(CH_BLOCKS):
        ks = slice(k * STATE_BLOCK, (k + 1) * STATE_BLOCK)
        y2 = (jnp.dot(xr_s[:, ks].astype(BF16), wor_ref[k], preferred_element_type=F32)
              + jnp.dot(xi_s[:, ks].astype(BF16), woi_ref[k], preferred_element_type=F32)
              + jnp.dot(lhs[k], wl_ref[k], preferred_element_type=F32))
        ys.append(interleave(y2[:, :LANES], y2[:, LANES:]) + d_ref[:, k * LANES:(k + 1) * LANES] * u_s[k])
    act = jax.nn.gelu(jnp.concatenate(ys, axis=1)).astype(BF16)
    z = jnp.dot(act, gw_ref[...], preferred_element_type=F32) + gb_ref[...]
    mix = z[:, :D_MODEL] * jax.nn.sigmoid(z[:, D_MODEL:])
    for k in range(CH_BLOCKS):
        u_s[k] = mix[:, k * LANES:(k + 1) * LANES]
    for s in range(rb):
        o_ref[s] = h_ref[s] + jnp.concatenate([u_s[k, seq_rows(s), :] for k in range(CH_BLOCKS)], axis=1)


def _s5_pairs(h, g, ar, ai, w_in, w_out_r, w_out_i, w_loc, d_skip, glu_w, glu_b, *, layer, tt):
    rb, t_len, _ = h.shape
    assert t_len % tt == 0 and tt % 2 == 0 and rb % 8 == 0
    rows, prow = rb * tt, rb * tt // 2
    block = (rb, tt, D_MODEL)
    state = jax.ShapeDtypeStruct((rb, STATE_W), F32)
    return pl.pallas_call(
        functools.partial(_s5_pair_kernel, rb=rb, tt=tt),
        grid=(t_len // tt,),
        in_specs=[pl.BlockSpec(block, lambda i: (0, i, 0)),
                  _const_spec((1, D_MODEL)),
                  _const_spec((1, STATE_W)), _const_spec((1, STATE_W)),
                  _const_spec((CH_BLOCKS, 2 * LANES, 2 * STATE_BLOCK)),
                  _const_spec((CH_BLOCKS, STATE_BLOCK, 2 * LANES)),
                  _const_spec((CH_BLOCKS, STATE_BLOCK, 2 * LANES)),
                  _const_spec((CH_BLOCKS, 2 * LANES, 2 * LANES)),
                  _const_spec((1, D_MODEL)),
                  _layer_spec((D_MODEL, 2 * D_MODEL), layer),
                  _const_spec((1, 2 * D_MODEL))],
        out_specs=(pl.BlockSpec(block, lambda i: (0, i, 0)),
                   pl.BlockSpec((rb, STATE_W), lambda i: (0, 0)),
                   pl.BlockSpec((rb, STATE_W), lambda i: (0, 0))),
        out_shape=(jax.ShapeDtypeStruct(h.shape, F32), state, state),
        scratch_shapes=[pltpu.VMEM((CH_BLOCKS, rows, LANES), F32),
                        pltpu.VMEM((prow, STATE_W), F32), pltpu.VMEM((prow, STATE_W), F32)],
        compiler_params=_params("arbitrary"),
        name="s5_pairs",
    )(h, g.reshape(1, D_MODEL), ar, ai, w_in, w_out_r, w_out_i, w_loc, d_skip.reshape(1, D_MODEL),
      glu_w, glu_b.reshape(1, 2 * D_MODEL))


def _proj_kernel(x_ref, g_ref, w_ref, o_ref, *, scale):
    xn = _rms(x_ref[...], g_ref[...]).astype(BF16)
    y = jnp.dot(xn, w_ref[...], preferred_element_type=F32)
    o_ref[...] = y if scale == 1.0 else y * scale


def _proj(x, g, w, layer=None, scale=1.0, tm=512):
    n = x.shape[0]
    tm = min(tm, n)
    nout = w.shape[-1]
    return pl.pallas_call(
        functools.partial(_proj_kernel, scale=scale),
        grid=(n // tm,),
        in_specs=[pl.BlockSpec((tm, D_MODEL), lambda i: (i, 0)),
                  _const_spec((1, D_MODEL)),
                  _layer_spec((D_MODEL, nout), layer)],
        out_specs=pl.BlockSpec((tm, nout), lambda i: (i, 0)),
        out_shape=jax.ShapeDtypeStruct((n, nout), F32),
        compiler_params=_params("parallel"),
        name="proj",
    )(x, g.reshape(1, D_MODEL), w)


def _kv_tails(t_len, tm):
    tails = []
    for w in DIL_WINDOWS:
        n_tail = min(w, t_len)
        if n_tail >= tm:
            assert n_tail % tm == 0
            tails.append(((t_len - n_tail) // tm, 0, tm))
        else:
            tails.append((t_len // tm - 1, tm - n_tail, n_tail))
    return tails


def _kv_kernel(x_ref, g_ref, w_ref, wt_ref, kv_ref, *tail_refs, tails):
    j = pl.program_id(1)
    xn = _rms(x_ref[...], g_ref[...]).astype(BF16)
    kv_ref[...] = jnp.dot(xn, w_ref[...], preferred_element_type=F32)
    for g, (first_tile, row0, n_tok) in enumerate(tails):
        def emit(g=g, row0=row0, n_tok=n_tok):
            wt = wt_ref[g * 2 * ATTN_WIDTH:(g + 1) * 2 * ATTN_WIDTH, :]
            tail_refs[g][...] = lax.dot_general(wt, xn[row0:row0 + n_tok], (((1,), (1,)), ((), ())),
                                                preferred_element_type=F32)

        if first_tile == 0:
            emit()
        else:
            pl.when(j >= first_tile)(emit)


def _kv_proj(x, g, w, w_t, tm=512):
    bn, t_len, _ = x.shape
    tm = min(tm, t_len)
    assert t_len % tm == 0
    nout = w.shape[-1]
    tails = _kv_tails(t_len, tm)
    tail_specs, tail_shapes = [], []
    for (first_tile, _, n_tok), win in zip(tails, DIL_WINDOWS):
        tail_specs.append(pl.BlockSpec((None, 2 * ATTN_WIDTH, n_tok),
                                       lambda b, j, f=first_tile: (b, 0, jnp.maximum(j - f, 0))))
        tail_shapes.append(jax.ShapeDtypeStruct((bn, 2 * ATTN_WIDTH, min(win, t_len)), F32))
    return pl.pallas_call(
        functools.partial(_kv_kernel, tails=tails),
        grid=(bn, t_len // tm),
        in_specs=[pl.BlockSpec((None, tm, D_MODEL), lambda b, j: (b, j, 0)),
                  _const_spec((1, D_MODEL)),
                  _const_spec((D_MODEL, nout)),
                  _const_spec((nout, D_MODEL))],
        out_specs=[pl.BlockSpec((None, tm, nout), lambda b, j: (b, j, 0))] + tail_specs,
        out_shape=[jax.ShapeDtypeStruct((bn, t_len, nout), F32)] + tail_shapes,
        compiler_params=_params("parallel", "arbitrary"),
        name="kv_proj",
    )(x, g.reshape(1, D_MODEL), w, w_t)


def _oproj_kernel(a_ref, w_ref, h_ref, o_ref):
    o_ref[...] = h_ref[...] + jnp.dot(a_ref[...].astype(BF16), w_ref[...], preferred_element_type=F32)


def _oproj(a, w, layer, h, tm=512):
    n = h.shape[0]
    tm = min(tm, n)
    return pl.pallas_call(
        _oproj_kernel,
        grid=(n // tm,),
        in_specs=[pl.BlockSpec((tm, ATTN_WIDTH), lambda i: (i, 0)),
                  _layer_spec((ATTN_WIDTH, D_MODEL), layer),
                  pl.BlockSpec((tm, D_MODEL), lambda i: (i, 0))],
        out_specs=pl.BlockSpec((tm, D_MODEL), lambda i: (i, 0)),
        out_shape=jax.ShapeDtypeStruct((n, D_MODEL), F32),
        compiler_params=_params("parallel"),
        name="oproj",
    )(a, w, h)


def _alibi_slopes():
    n = N_DIL * HEADS_PER_GROUP
    s = jnp.exp2(-8.0 * jnp.arange(1, n + 1, dtype=F32) / n)
    return s.reshape(N_DIL, HEADS_PER_GROUP)


def _prompt_bias():
    qi = jnp.arange(R_LEN)[:, None]
    kj = jnp.arange(2 * R_LEN)[None, :]
    diff = R_LEN + qi - kj
    valid = (diff >= 0) & (diff <= R_LEN)
    dil = jnp.asarray(DIL_RATES, jnp.int32)[:, None, None]
    dist = (dil * diff[None]).astype(F32)
    bias = -_alibi_slopes()[:, :, None, None] * dist[:, None]
    return jnp.where(valid[None, None], bias, NEG)


def _sample_bias():
    kk = R_LEN - jnp.arange(R_LEN)
    dil = jnp.asarray(DIL_RATES, jnp.int32)[:, None]
    dist = (dil * kk[None]).astype(F32)
    return -_alibi_slopes()[:, :, None] * dist[:, None]


MERGE_ROWS = 256
INTERLEAVE = 16


def _attn_kernel(q0, q1, q2, k0, k1, k2, v0, v1, v2, bias_ref, o_ref, acc_s, m_s, l_s, *, t_len):
    qs, ks, vs = (q0, q1, q2), (k0, k1, k2), (v0, v1, v2)
    lo = lax.broadcasted_iota(jnp.int32, (R_LEN, LANES), 1) < HEAD_DIM

    def block(g, q_idx, k_idx, first):
        q = qs[g][q_idx, :]
        qb = jnp.concatenate([jnp.where(lo, q, 0.0), jnp.where(lo, 0.0, q)], axis=0).astype(BF16)
        kb = ks[g][k_idx, :].astype(BF16)
        vb = vs[g][k_idx, :].astype(BF16)
        s = lax.dot_general(qb, kb, (((1,), (1,)), ((), ())), preferred_element_type=F32)
        bias = bias_ref[g, :, :, R_LEN:] if first else bias_ref[g]
        s = s + bias.reshape(2 * R_LEN, bias.shape[-1])
        m = jnp.max(s, axis=-1, keepdims=True)
        p = jnp.exp(s - m)
        l = jnp.sum(p, axis=-1, keepdims=True)
        a = jnp.dot(p.astype(BF16), vb, preferred_element_type=F32)
        acc_s[g, q_idx, :] = jnp.where(lo, a[:R_LEN], a[R_LEN:])
        m_s[g, q_idx, :] = jnp.where(lo, m[:R_LEN], m[R_LEN:])
        l_s[g, q_idx, :] = jnp.where(lo, l[:R_LEN], l[R_LEN:])

    for g in range(N_DIL):
        dil = DIL_RATES[g]
        span = R_LEN * dil
        n_blocks = t_len // span

        def rows(start, n, dil=dil):
            if dil > 1:
                return pl.ds(start, n, stride=dil)
            return pl.ds(start if isinstance(start, int) else pl.multiple_of(start, R_LEN), n)

        if dil <= INTERLEAVE:
            for r in range(dil):
                block(g, rows(r, R_LEN), rows(r, R_LEN), True)
        else:
            def first_blocks(i, c, g=g, rows=rows):
                for j in range(INTERLEAVE):
                    r = i * INTERLEAVE + j
                    block(g, rows(r, R_LEN), rows(r, R_LEN), True)
                return c

            lax.fori_loop(0, dil // INTERLEAVE, first_blocks, 0)

        later = n_blocks - 1
        if later == 0:
            continue
        per_trip = max(w for w in range(1, max(INTERLEAVE // dil, 1) + 1) if later % w == 0)

        def later_blocks(i, c, g=g, dil=dil, span=span, rows=rows, per_trip=per_trip):
            for j in range(per_trip):
                nb = 1 + i * per_trip + j
                for r in range(dil):
                    start = nb * span + r
                    block(g, rows(start, R_LEN), rows(start - span, 2 * R_LEN), False)
            return c

        if later == per_trip:
            later_blocks(0, 0)
        else:
            lax.fori_loop(0, later // per_trip, later_blocks, 0)

    def merge(i, c):
        rows = pl.ds(pl.multiple_of(i * MERGE_ROWS, MERGE_ROWS), MERGE_ROWS)
        ms = [m_s[g, rows, :] for g in range(N_DIL)]
        mx = jnp.maximum(jnp.maximum(ms[0], ms[1]), ms[2])
        ws = [jnp.exp(m - mx) for m in ms]
        den = ws[0] * l_s[0, rows, :] + ws[1] * l_s[1, rows, :] + ws[2] * l_s[2, rows, :]
        num = ws[0] * acc_s[0, rows, :] + ws[1] * acc_s[1, rows, :] + ws[2] * acc_s[2, rows, :]
        o_ref[rows, :] = num / den
        return c

    lax.fori_loop(0, t_len // MERGE_ROWS, merge, 0)


def _attn_prompt(q, kv, bias):
    bn, t_len, _ = q.shape
    pairs = ATTN_WIDTH // LANES
    per_group = 2 * ATTN_WIDTH // LANES

    def col(off):
        return pl.BlockSpec((None, t_len, LANES), lambda b, hp: (b, 0, off + hp))

    in_specs = ([col(g * pairs) for g in range(N_DIL)]
                + [col(g * per_group) for g in range(N_DIL)]
                + [col(g * per_group + pairs) for g in range(N_DIL)]
                + [pl.BlockSpec((N_DIL, 2, R_LEN, 2 * R_LEN), lambda b, hp: (0, hp, 0, 0))])
    scratch = pltpu.VMEM((N_DIL, t_len, LANES), F32)
    return pl.pallas_call(
        functools.partial(_attn_kernel, t_len=t_len),
        grid=(bn, pairs),
        in_specs=in_specs,
        out_specs=pl.BlockSpec((None, t_len, LANES), lambda b, hp: (b, 0, hp)),
        out_shape=jax.ShapeDtypeStruct((bn, t_len, ATTN_WIDTH), F32),
        scratch_shapes=[scratch, scratch, scratch],
        compiler_params=_params("parallel", "parallel"),
        name="attn_prompt",
    )(q, q, q, kv, kv, kv, kv, kv, kv, bias)


SAMPLE_BLOCK = 8


def _window_kernel(c_ref, o_ref, *, dil):
    w = c_ref.shape[-1]
    pos = lax.broadcasted_iota(jnp.int32, (w, R_LEN), 0)
    col = lax.broadcasted_iota(jnp.int32, (w, R_LEN), 1)
    pick = jnp.where(pos == col * dil, 1.0, 0.0).astype(BF16)
    o_ref[...] = jnp.dot(c_ref[...].astype(BF16), pick, preferred_element_type=F32).astype(BF16)


def _strided_window(cache, dil):
    bd, w = cache.shape[:2]
    assert w == R_LEN * dil
    slab = jnp.transpose(cache, (0, 2, 3, 4, 1)).reshape(bd, 2 * ATTN_WIDTH, w)
    if dil == 1:
        return slab
    return pl.pallas_call(
        functools.partial(_window_kernel, dil=dil),
        grid=(bd,),
        in_specs=[pl.BlockSpec((None, 2 * ATTN_WIDTH, w), lambda i: (i, 0, 0))],
        out_specs=pl.BlockSpec((None, 2 * ATTN_WIDTH, R_LEN), lambda i: (i, 0, 0)),
        out_shape=jax.ShapeDtypeStruct((bd, 2 * ATTN_WIDTH, R_LEN), BF16),
        compiler_params=_params("parallel"),
        name="cache_window",
    )(slab)


def _attn_sample_kernel(q_ref, kvn_ref, c0, c1, c2, bias_ref, o_ref):
    windows = (c0, c1, c2)
    sub = lax.broadcasted_iota(jnp.int32, (HEADS_PER_GROUP, ATTN_WIDTH), 0)
    lane = lax.broadcasted_iota(jnp.int32, (HEADS_PER_GROUP, ATTN_WIDTH), 1)
    own = (lane // HEAD_DIM) == sub
    for b in range(SAMPLE_BLOCK):
        os_, ms, ls = [], [], []
        for g in range(N_DIL):
            base = g * 2 * ATTN_WIDTH
            qbd = jnp.where(own, q_ref[b:b + 1, g * ATTN_WIDTH:(g + 1) * ATTN_WIDTH], 0.0)
            kt = windows[g][b, :ATTN_WIDTH, :].astype(BF16)
            vt = windows[g][b, ATTN_WIDTH:, :].astype(BF16)
            kn = kvn_ref[b:b + 1, base:base + ATTN_WIDTH]
            vn = kvn_ref[b:b + 1, base + ATTN_WIDTH:base + 2 * ATTN_WIDTH]
            s = jnp.dot(qbd.astype(BF16), kt, preferred_element_type=F32) + bias_ref[g]
            s_new = jnp.sum(qbd * kn, axis=-1, keepdims=True)
            m = jnp.maximum(jnp.max(s, axis=-1, keepdims=True), s_new)
            p = jnp.exp(s - m)
            p_new = jnp.exp(s_new - m)
            ls.append(jnp.sum(p, axis=-1, keepdims=True) + p_new)
            os_.append(lax.dot_general(p.astype(BF16), vt, (((1,), (1,)), ((), ())),
                                       preferred_element_type=F32) + p_new * vn)
            ms.append(m)
        mx = jnp.maximum(jnp.maximum(ms[0], ms[1]), ms[2])
        ws = [jnp.exp(m - mx) for m in ms]
        den = ws[0] * ls[0] + ws[1] * ls[1] + ws[2] * ls[2]
        o = (ws[0] * os_[0] + ws[1] * os_[1] + ws[2] * os_[2]) / den
        o_ref[b:b + 1, :] = jnp.sum(jnp.where(own, o, 0.0), axis=0, keepdims=True)


def _attn_sample(q, kv_new, windows, bias):
    bd = q.shape[0]
    assert bd % SAMPLE_BLOCK == 0
    window_spec = pl.BlockSpec((SAMPLE_BLOCK, 2 * ATTN_WIDTH, R_LEN), lambda i: (i, 0, 0))
    return pl.pallas_call(
        _attn_sample_kernel,
        grid=(bd // SAMPLE_BLOCK,),
        in_specs=[pl.BlockSpec((SAMPLE_BLOCK, N_DIL * ATTN_WIDTH), lambda i: (i, 0)),
                  pl.BlockSpec((SAMPLE_BLOCK, N_DIL * 2 * ATTN_WIDTH), lambda i: (i, 0))]
                 + [window_spec] * N_DIL + [_const_spec((N_DIL, HEADS_PER_GROUP, R_LEN))],
        out_specs=pl.BlockSpec((SAMPLE_BLOCK, ATTN_WIDTH), lambda i: (i, 0)),
        out_shape=jax.ShapeDtypeStruct((bd, ATTN_WIDTH), F32),
        compiler_params=_params("parallel"),
        name="attn_sample",
    )(q, kv_new, *windows, bias)


def kernel(x_prompt, x_sample, state_ssm_re, state_ssm_im, cache_kv_w128, cache_kv_w512, cache_kv_w2048, norm_ffn1, ffn1_w_in, ffn1_w_out, norm_mix, norm_ffn2, ffn2_w_in, ffn2_w_out, norm_kv, norm_final, ssm_log_dt, ssm_lambda_re, ssm_lambda_im, ssm_b_re, ssm_b_im, ssm_c_re, ssm_c_im, ssm_d, glu_w, glu_b, attn_w_q, attn_w_kv, attn_w_o):
    bn, t_len, _ = x_prompt.shape
    bd = x_sample.shape[0]
    assert x_sample.shape[1] == 1

    glu_wb = glu_w.astype(BF16)
    w_q, w_kv, w_o = attn_w_q.astype(BF16), attn_w_kv.astype(BF16), attn_w_o.astype(BF16)
    q_scale = HEAD_DIM ** -0.5

    ssm, ssm_pairs = [], []
    for layer in range(N_A_LAYERS):
        ar, ai, wb, wcr, wci, *pair_w = _ssm_weights(
            ssm_log_dt[layer], ssm_lambda_re[layer], ssm_lambda_im[layer],
            ssm_b_re[layer], ssm_b_im[layer], ssm_c_re[layer], ssm_c_im[layer])
        ar, ai = ar.reshape(1, STATE_W), ai.reshape(1, STATE_W)
        tail = (ssm_d[layer].reshape(D_MODEL), glu_wb, glu_b[layer])
        ssm.append((ar, ai, wb, wcr, wci) + tail)
        ssm_pairs.append((ar, ai, *pair_w) + tail)

    def states(parts, n_seq):
        return jnp.stack([p.reshape(n_seq, N_SSM_GROUPS, SSM_STATE) for p in parts])

    s0r = state_ssm_re.reshape(N_A_LAYERS, bd, STATE_W)
    s0i = state_ssm_im.reshape(N_A_LAYERS, bd, STATE_W)
    windows = [_strided_window(cache, dil)
               for cache, dil in zip((cache_kv_w128, cache_kv_w512, cache_kv_w2048), DIL_RATES)]
    sample_bias = _sample_bias()
    ffn1_w, ffn2_w, fin_r, fin_i = [], [], [], []
    hs = x_sample.reshape(bd, D_MODEL)
    for layer in range(DEPTH):
        if layer == N_A_LAYERS:
            kv_s = _proj(hs, norm_kv, w_kv)
        hs, w = _ffn_cast(hs, norm_ffn1[layer], ffn1_w_in, ffn1_w_out, layer)
        ffn1_w.append(w)
        if layer < N_A_LAYERS:
            hs, sr, si = _s5(hs.reshape(bd, 1, D_MODEL), norm_mix[layer], s0r[layer], s0i[layer],
                             *ssm[layer], layer=layer, tt=1)
            hs = hs.reshape(bd, D_MODEL)
            fin_r.append(sr)
            fin_i.append(si)
        else:
            b = layer - N_A_LAYERS
            q = _proj(hs, norm_mix[layer], w_q, layer=b, scale=q_scale)
            hs = _oproj(_attn_sample(q, kv_s, windows, sample_bias), w_o, b, hs)
        hs, w = _ffn_cast(hs, norm_ffn2[layer], ffn2_w_in, ffn2_w_out, layer,
                          final_g=norm_final if layer == DEPTH - 1 else None)
        ffn2_w.append(w)
    y_sample = hs.reshape(bd, 1, D_MODEL)
    ssm_re_s, ssm_im_s = states(fin_r, bd), states(fin_i, bd)
    kv_s = kv_s.reshape(bd, 1, N_DIL, 2, HEADS_PER_GROUP, HEAD_DIM)

    prompt_bias = _prompt_bias()
    fin_r, fin_i = [], []
    h = x_prompt.reshape(bn * t_len, D_MODEL)
    for layer in range(DEPTH):
        if layer == N_A_LAYERS:
            kv_p, *kv_tails = _kv_proj(h.reshape(bn, t_len, D_MODEL), norm_kv, w_kv, w_kv.T)
        if layer < N_A_LAYERS:
            h = _ffn(h, norm_ffn1[layer], ffn1_w[layer])
            h, sr, si = _s5_pairs(h.reshape(bn, t_len, D_MODEL), norm_mix[layer],
                                  *ssm_pairs[layer], layer=layer, tt=64)
            h = h.reshape(bn * t_len, D_MODEL)
            fin_r.append(sr)
            fin_i.append(si)
            h = _ffn(h, norm_ffn2[layer], ffn2_w[layer])
        else:
            b = layer - N_A_LAYERS
            h, q = _ffn(h, norm_ffn1[layer], ffn1_w[layer], q_proj=(norm_mix[layer], w_q, b, q_scale))
            merged = _attn_prompt(q.reshape(bn, t_len, -1), kv_p, prompt_bias)
            h = _ffn(h, norm_ffn2[layer], ffn2_w[layer], attn=(merged.reshape(bn * t_len, ATTN_WIDTH), w_o, b),
                     final_g=norm_final if layer == DEPTH - 1 else None)
    y_prompt = h.reshape(bn, t_len, D_MODEL)
    ssm_re_p, ssm_im_p = states(fin_r, bn), states(fin_i, bn)
    kv_out_p = [jnp.transpose(tail.reshape(bn, 2, HEADS_PER_GROUP, HEAD_DIM, -1), (0, 4, 1, 2, 3))
                for tail in kv_tails]

    return (y_prompt, y_sample, ssm_re_p, ssm_im_p, kv_out_p[0], kv_out_p[1], kv_out_p[2],
            ssm_re_s, ssm_im_s, kv_s[:, :, 0], kv_s[:, :, 1], kv_s[:, :, 2])
```

```python
import functools

import jax
import jax.numpy as jnp
from jax import lax
from jax.experimental import pallas as pl
from jax.experimental.pallas import tpu as pltpu

F32 = jnp.float32
BF16 = jnp.bfloat16

D_MODEL = 1024
DEPTH = 4
N_A_LAYERS = DEPTH // 2
SSM_GROUP = 16
N_SSM_GROUPS = D_MODEL // SSM_GROUP
SSM_STATE = 64
STATE_W = N_SSM_GROUPS * SSM_STATE
DIL_WINDOWS = (128, 512, 2048)
DIL_RATES = (1, 4, 16)
N_DIL = 3
HEADS_PER_GROUP = 8
HEAD_DIM = 64
ATTN_WIDTH = HEADS_PER_GROUP * HEAD_DIM
R_LEN = DIL_WINDOWS[0] // DIL_RATES[0]
D_FF = 2816
EPS = 1e-6
NEG = -1e30

LANES = 128
MXU_TILE = 256
BF16_SUBLANES = 16
CH_BLOCKS = D_MODEL // LANES
GROUPS_PER_BLOCK = LANES // SSM_GROUP
STATE_BLOCK = GROUPS_PER_BLOCK * SSM_STATE
VMEM_LIMIT_BYTES = 56 * 1024 * 1024

assert all(w // d == R_LEN for w, d in zip(DIL_WINDOWS, DIL_RATES))


def _params(*semantics):
    return pltpu.CompilerParams(dimension_semantics=semantics, vmem_limit_bytes=VMEM_LIMIT_BYTES)


def _const_spec(shape):
    zeros = (0,) * len(shape)
    return pl.BlockSpec(shape, lambda *_: zeros, pipeline_mode=pl.Buffered(1))


def _layer_spec(shape, layer):
    if layer is None:
        return _const_spec(shape)
    idx = (layer,) + (0,) * len(shape)
    return pl.BlockSpec((None,) + tuple(shape), lambda *_: idx, pipeline_mode=pl.Buffered(1))


def _rms(x, g):
    return x * lax.rsqrt(jnp.mean(x * x, axis=-1, keepdims=True) + EPS) * g


def _ff_chunks(n_chunks):
    tiles = D_FF // MXU_TILE
    assert tiles * MXU_TILE == D_FF and n_chunks <= tiles
    cuts = [MXU_TILE * ((tiles * c + n_chunks - 1) // n_chunks) for c in range(n_chunks + 1)]
    return list(zip(cuts[:-1], cuts[1:]))


def _swiglu_residual(x, g, wa_ref, wb_ref, wout_ref, n_chunks):
    xn = _rms(x, g).astype(BF16)
    acc = None
    for lo, hi in _ff_chunks(n_chunks):
        a = jnp.dot(xn, wa_ref[:, lo:hi], preferred_element_type=F32)
        b = jnp.dot(xn, wb_ref[:, lo:hi], preferred_element_type=F32)
        act = (a * jax.nn.sigmoid(a) * b).astype(BF16)
        d = jnp.dot(act, wout_ref[lo:hi, :], preferred_element_type=F32)
        acc = d if acc is None else acc + d
    return x + 0.5 * acc


def _ffn_kernel(*refs, n_chunks, attn_in, final, q_scale, cast_next):
    refs = list(refs)
    x = refs.pop(0)[...]
    if attn_in:
        a_ref, wo_ref = refs.pop(0), refs.pop(0)
        x = x + jnp.dot(a_ref[...].astype(BF16), wo_ref[...], preferred_element_type=F32)
    g_ref, wa_ref, wb_ref, wout_ref = (refs.pop(0) for _ in range(4))
    y = _swiglu_residual(x, g_ref[...], wa_ref, wb_ref, wout_ref, n_chunks)
    if final:
        y = _rms(y, refs.pop(0)[...])
    if q_scale is not None:
        gq_ref, wq_ref = refs.pop(0), refs.pop(0)
    if cast_next:
        next_f32 = [refs.pop(0) for _ in range(3)]
    refs.pop(0)[...] = y
    if q_scale is not None:
        q = jnp.dot(_rms(y, gq_ref[...]).astype(BF16), wq_ref[...], preferred_element_type=F32)
        refs.pop(0)[...] = q * q_scale
    if cast_next:
        for src in next_f32:
            refs.pop(0)[...] = src[...].astype(BF16)


def _ffn(x, g, weights, attn=None, final_g=None, q_proj=None, cast_next=None, tm=512, n_chunks=2):
    n = x.shape[0]
    tm = min(tm, n)
    assert n % tm == 0
    rows = lambda width: pl.BlockSpec((tm, width), lambda i: (i, 0))
    in_specs, args = [rows(D_MODEL)], [x]
    if attn is not None:
        merged, w_o, idx = attn
        in_specs += [rows(ATTN_WIDTH), _layer_spec((ATTN_WIDTH, D_MODEL), idx)]
        args += [merged, w_o]
    in_specs += [_const_spec((1, D_MODEL)), _const_spec((D_MODEL, D_FF)), _const_spec((D_MODEL, D_FF)),
                 _const_spec((D_FF, D_MODEL))]
    args += [g.reshape(1, D_MODEL), *weights]
    if final_g is not None:
        in_specs.append(_const_spec((1, D_MODEL)))
        args.append(final_g.reshape(1, D_MODEL))
    out_specs, out_shape = [rows(D_MODEL)], [jax.ShapeDtypeStruct((n, D_MODEL), F32)]
    q_scale = None
    if q_proj is not None:
        gq, w_q, idx, q_scale = q_proj
        nq = w_q.shape[-1]
        in_specs += [_const_spec((1, D_MODEL)), _layer_spec((D_MODEL, nq), idx)]
        args += [gq.reshape(1, D_MODEL), w_q]
        out_specs.append(rows(nq))
        out_shape.append(jax.ShapeDtypeStruct((n, nq), F32))
    if cast_next is not None:
        w_in, w_out, idx = cast_next
        steps = n // tm
        r_in = D_MODEL // steps
        out_blocks =[b for b in range(1, steps + 1) if steps % b == 0 and D_FF % b == 0
                      and (D_FF // b) % BF16_SUBLANES == 0][-1]
        r_out, hold = D_FF // out_blocks, steps // out_blocks
        assert r_in * steps == D_MODEL and r_in % BF16_SUBLANES == 0
        in_specs += [pl.BlockSpec((None, r_in, D_FF), lambda i: (idx, i, 0)),
                     pl.BlockSpec((None, r_in, D_FF), lambda i: (idx, i, 1)),
                     pl.BlockSpec((None, r_out, D_MODEL), lambda i: (idx, i // hold, 0))]
        args += [w_in, w_in, w_out]
        out_specs += [pl.BlockSpec((r_in, D_FF), lambda i: (i, 0)),
                      pl.BlockSpec((r_in, D_FF), lambda i: (i, 0)),
                      pl.BlockSpec((r_out, D_MODEL), lambda i: (i // hold, 0))]
        out_shape += [jax.ShapeDtypeStruct((D_MODEL, D_FF), BF16), jax.ShapeDtypeStruct((D_MODEL, D_FF), BF16),
                      jax.ShapeDtypeStruct((D_FF, D_MODEL), BF16)]
    out = pl.pallas_call(
        functools.partial(_ffn_kernel, n_chunks=n_chunks, attn_in=attn is not None,
                          final=final_g is not None, q_scale=q_scale, cast_next=cast_next is not None),
        grid=(n // tm,),
        in_specs=in_specs,
        out_specs=out_specs,
        out_shape=out_shape,
        compiler_params=_params("arbitrary" if cast_next is not None else "parallel"),
        name="ffn",
    )(*args)
    n_main = 2 if q_proj is not None else 1
    main = out[0] if n_main == 1 else tuple(out[:n_main])
    return (main, tuple(out[n_main:])) if cast_next is not None else main


def _ffn_cast_kernel(x_ref, g_ref, wa_ref, wb_ref, wout_ref, *rest, final):
    if final:
        gf_ref, o_ref, wa_o, wb_o, wout_o, acc_s = rest
    else:
        o_ref, wa_o, wb_o, wout_o, acc_s = rest
    c = pl.program_id(0)
    wa = wa_ref[...].astype(BF16)
    wb = wb_ref[...].astype(BF16)
    wout = wout_ref[...].astype(BF16)
    wa_o[...] = wa
    wb_o[...] = wb
    wout_o[...] = wout
    x = x_ref[...]
    xn = _rms(x, g_ref[...]).astype(BF16)
    a = jnp.dot(xn, wa, preferred_element_type=F32)
    b = jnp.dot(xn, wb, preferred_element_type=F32)
    d = jnp.dot((a * jax.nn.sigmoid(a) * b).astype(BF16), wout, preferred_element_type=F32)

    @pl.when(c == 0)
    def _():
        acc_s[...] = d

    @pl.when(c > 0)
    def _():
        acc_s[...] += d

    @pl.when(c == pl.num_programs(0) - 1)
    def _():
        y = x + 0.5 * acc_s[...]
        o_ref[...] = _rms(y, gf_ref[...]) if final else y


def _ffn_cast(x, g, w_in, w_out, layer, final_g=None):
    n = x.shape[0]
    n_chunks = D_FF // MXU_TILE
    final = final_g is not None
    in_specs = [_const_spec((n, D_MODEL)), _const_spec((1, D_MODEL)),
                pl.BlockSpec((None, D_MODEL, MXU_TILE), lambda c: (layer, 0, c)),
                pl.BlockSpec((None, D_MODEL, MXU_TILE), lambda c: (layer, 0, n_chunks + c)),
                pl.BlockSpec((None, MXU_TILE, D_MODEL), lambda c: (layer, c, 0))]
    args = [x, g.reshape(1, D_MODEL), w_in, w_in, w_out]
    if final:
        in_specs.append(_const_spec((1, D_MODEL)))
        args.append(final_g.reshape(1, D_MODEL))
    y, wa, wb, wo = pl.pallas_call(
        functools.partial(_ffn_cast_kernel, final=final),
        grid=(n_chunks,),
        in_specs=in_specs,
        out_specs=[pl.BlockSpec((n, D_MODEL), lambda c: (0, 0)),
                   pl.BlockSpec((D_MODEL, MXU_TILE), lambda c: (0, c)),
                   pl.BlockSpec((D_MODEL, MXU_TILE), lambda c: (0, c)),
                   pl.BlockSpec((MXU_TILE, D_MODEL), lambda c: (c, 0))],
        out_shape=[jax.ShapeDtypeStruct((n, D_MODEL), F32),
                   jax.ShapeDtypeStruct((D_MODEL, D_FF), BF16),
                   jax.ShapeDtypeStruct((D_MODEL, D_FF), BF16),
                   jax.ShapeDtypeStruct((D_FF, D_MODEL), BF16)],
        scratch_shapes=[pltpu.VMEM((n, D_MODEL), F32)],
        compiler_params=_params("arbitrary"),
        name="ffn_cast",
    )(*args)
    return y, (wa, wb, wo)


def _ssm_weights_kernel(ldt_ref, lr_ref, li_ref, br_ref, bi_ref, cr_ref, ci_ref,
                        ar_ref, ai_ref, wb_ref, wcr_ref, wci_ref, win_ref, wor_ref, woi_ref, wl_ref):
    dt = jnp.exp(ldt_ref[...])
    lr = lr_ref[...]
    li = li_ref[...]
    mag = jnp.exp(lr * dt)
    ar = mag * jnp.cos(li * dt)
    ai = mag * jnp.sin(li * dt)
    den = lr * lr + li * li
    zr = ((ar - 1.0) * lr + ai * li) / den
    zi = (ai * lr - (ar - 1.0) * li) / den
    ar_ref[...] = ar
    ai_ref[...] = ai
    br = br_ref[...]
    bi = bi_ref[...]
    bbr = zr[:, None, :] * br - zi[:, None, :] * bi
    bbi = zr[:, None, :] * bi + zi[:, None, :] * br
    a1r, a1i = ar[:, None, :], ai[:, None, :]
    a2r, a2i = (ar * ar - ai * ai)[:, None, :], (2.0 * ar * ai)[:, None, :]
    abr = a1r * bbr - a1i * bbi
    abi = a1r * bbi + a1i * bbr
    cr = cr_ref[...]
    ci = ci_ref[...]
    car = cr * a1r - ci * a1i
    cai = cr * a1i + ci * a1r
    ca2r = cr * a2r - ci * a2i
    ca2i = cr * a2i + ci * a2r

    def re_dot(xr, xi):
        dims = (((2,), (2,)), ((0,), (0,)))
        return (lax.dot_general(xr, bbr, dims, preferred_element_type=F32)
                - lax.dot_general(xi, bbi, dims, preferred_element_type=F32))

    k0 = re_dot(cr, ci)
    k1 = re_dot(car, cai)

    def same_group(n_rows, row_span, n_cols, col_span):
        r = lax.broadcasted_iota(jnp.int32, (n_rows, n_cols), 0) // row_span
        c = lax.broadcasted_iota(jnp.int32, (n_rows, n_cols), 1) // col_span
        return r == c

    def replicate(n, copies, transpose):
        shape = (copies * n, n) if transpose else (n, copies * n)
        r = lax.broadcasted_iota(jnp.int32, shape, 0)
        c = lax.broadcasted_iota(jnp.int32, shape, 1)
        return jnp.where(r % n == c % n, 1.0, 0.0).astype(BF16)

    gpb = GROUPS_PER_BLOCK
    rep_p = replicate(SSM_STATE, gpb, False)
    rep_pt = replicate(SSM_STATE, gpb, True)
    rep_ht = replicate(SSM_GROUP, gpb, True)
    m_in = same_group(LANES, SSM_GROUP, STATE_BLOCK, SSM_STATE)
    m_out = same_group(STATE_BLOCK, SSM_STATE, LANES, SSM_GROUP)
    m_mix = same_group(LANES, SSM_GROUP, LANES, SSM_GROUP)
    nt = (((1,), (1,)), ((), ()))

    def rows_of(x, k):
        return x[k * gpb:(k + 1) * gpb].reshape(LANES, x.shape[-1]).astype(BF16)

    def bd_in(x, k):
        return jnp.where(m_in, jnp.dot(rows_of(x, k), rep_p, preferred_element_type=F32), 0.0)

    def bd_out(x, k):
        return jnp.where(m_out, lax.dot_general(rep_pt, rows_of(x, k), nt, preferred_element_type=F32), 0.0)

    def bd_mix(x, k):
        return jnp.where(m_mix, lax.dot_general(rep_ht, rows_of(x, k), nt, preferred_element_type=F32), 0.0)

    for k in range(CH_BLOCKS):
        one_step = jnp.concatenate([bd_in(bbr, k), bd_in(bbi, k)], axis=1)
        wb_ref[k] = one_step.astype(BF16)
        wcr_ref[k] = bd_out(cr, k).astype(BF16)
        wci_ref[k] = (-bd_out(ci, k)).astype(BF16)
        win_ref[k] = jnp.concatenate([jnp.concatenate([bd_in(abr, k), bd_in(abi, k)], axis=1), one_step],
                                     axis=0).astype(BF16)
        wor_ref[k] = jnp.concatenate([bd_out(car, k), bd_out(ca2r, k)], axis=1).astype(BF16)
        woi_ref[k] = (-jnp.concatenate([bd_out(cai, k), bd_out(ca2i, k)], axis=1)).astype(BF16)
        m0, m1 = bd_mix(k0, k), bd_mix(k1, k)
        wl_ref[k] = jnp.concatenate([jnp.concatenate([m0, m1], axis=1),
                                     jnp.concatenate([jnp.zeros_like(m0), m0], axis=1)], axis=0).astype(BF16)


def _ssm_weights(log_dt, lam_re, lam_im, b_re, b_im, c_re, c_im):
    g, p = N_SSM_GROUPS, SSM_STATE
    gp = jax.ShapeDtypeStruct((g, p), F32)
    stack = lambda rows, cols: jax.ShapeDtypeStruct((CH_BLOCKS, rows, cols), BF16)
    return pl.pallas_call(
        _ssm_weights_kernel,
        out_shape=(gp, gp,
                   stack(LANES, 2 * STATE_BLOCK), stack(STATE_BLOCK, LANES), stack(STATE_BLOCK, LANES),
                   stack(2 * LANES, 2 * STATE_BLOCK), stack(STATE_BLOCK, 2 * LANES), stack(STATE_BLOCK, 2 * LANES),
                   stack(2 * LANES, 2 * LANES)),
        compiler_params=pltpu.CompilerParams(vmem_limit_bytes=VMEM_LIMIT_BYTES),
        name="s5_weights",
    )(log_dt.reshape(g, 1), lam_re, lam_im, jnp.swapaxes(b_re, 1, 2), jnp.swapaxes(b_im, 1, 2), c_re, c_im)


SCAN_COLS = 512


def _s5_kernel(h_ref, g_ref, s0r_ref, s0i_ref, ar_ref, ai_ref, wb_ref, wcr_ref, wci_ref, d_ref,
               gw_ref, gb_ref, o_ref, sr_ref, si_ref, u_s, xr_s, xi_s, *, rb, tt):
    n_slab, per_slab = h_ref.shape[:2]
    assert n_slab * per_slab == rb * tt and (n_slab == rb or tt == 1)

    def seq_rows(s):
        return pl.ds(s, per_slab, stride=n_slab) if n_slab > 1 else pl.ds(0, per_slab)

    @pl.when(pl.program_id(0) == 0)
    def _():
        sr_ref[...] = s0r_ref[...]
        si_ref[...] = s0i_ref[...]

    for s in range(n_slab):
        un = _rms(h_ref[s], g_ref[...])
        for k in range(CH_BLOCKS):
            u_s[k, seq_rows(s), :] = un[:, k * LANES:(k + 1) * LANES]
    for k in range(CH_BLOCKS):
        bu = jnp.dot(u_s[k].astype(BF16), wb_ref[k], preferred_element_type=F32)
        xr_s[:, k * STATE_BLOCK:(k + 1) * STATE_BLOCK] = bu[:, :STATE_BLOCK]
        xi_s[:, k * STATE_BLOCK:(k + 1) * STATE_BLOCK] = bu[:, STATE_BLOCK:]

    for c in range(STATE_W // SCAN_COLS):
        cs = slice(c * SCAN_COLS, (c + 1) * SCAN_COLS)
        a_r = jnp.broadcast_to(ar_ref[:, cs], (rb, SCAN_COLS))
        a_i = jnp.broadcast_to(ai_ref[:, cs], (rb, SCAN_COLS))

        def step(t, carry, cs=cs, a_r=a_r, a_i=a_i):
            s_r, s_i = carry
            rows = pl.ds(pl.multiple_of(t * rb, rb), rb)
            n_r = a_r * s_r - a_i * s_i + xr_s[rows, cs]
            n_i = a_r * s_i + a_i * s_r + xi_s[rows, cs]
            xr_s[rows, cs] = n_r
            xi_s[rows, cs] = n_i
            return n_r, n_i

        s_r, s_i = lax.fori_loop(0, tt, step, (sr_ref[:, cs], si_ref[:, cs]), unroll=True)
        sr_ref[:, cs] = s_r
        si_ref[:, cs] = s_i

    ys = []
    for k in range(CH_BLOCKS):
        ks = slice(k * STATE_BLOCK, (k + 1) * STATE_BLOCK)
        ys.append(jnp.dot(xr_s[:, ks].astype(BF16), wcr_ref[k], preferred_element_type=F32)
                  + jnp.dot(xi_s[:, ks].astype(BF16), wci_ref[k], preferred_element_type=F32)
                  + d_ref[:, k * LANES:(k + 1) * LANES] * u_s[k])
    act = jax.nn.gelu(jnp.concatenate(ys, axis=1)).astype(BF16)
    z = jnp.dot(act, gw_ref[...], preferred_element_type=F32) + gb_ref[...]
    mix = z[:, :D_MODEL] * jax.nn.sigmoid(z[:, D_MODEL:])
    for k in range(CH_BLOCKS):
        u_s[k] = mix[:, k * LANES:(k + 1) * LANES]
    for s in range(n_slab):
        o_ref[s] = h_ref[s] + jnp.concatenate([u_s[k, seq_rows(s), :] for k in range(CH_BLOCKS)], axis=1)


def _s5(h, g, s0r, s0i, ar, ai, wb, wcr, wci, d_skip, glu_w, glu_b, *, layer, tt):
    rb, t_len, _ = h.shape
    assert t_len % tt == 0 and rb % 8 == 0
    rows = rb * tt
    block = (1, rb, D_MODEL) if t_len == 1 else (rb, tt, D_MODEL)
    h_in = h.reshape(1, rb, D_MODEL) if t_len == 1 else h
    state = jax.ShapeDtypeStruct((rb, STATE_W), F32)
    out, sr, si = pl.pallas_call(
        functools.partial(_s5_kernel, rb=rb, tt=tt),
        grid=(t_len // tt,),
        in_specs=[pl.BlockSpec(block, lambda i: (0, i, 0)),
                  _const_spec((1, D_MODEL)),
                  _const_spec((rb, STATE_W)), _const_spec((rb, STATE_W)),
                  _const_spec((1, STATE_W)), _const_spec((1, STATE_W)),
                  _const_spec((CH_BLOCKS, LANES, 2 * STATE_BLOCK)),
                  _const_spec((CH_BLOCKS, STATE_BLOCK, LANES)),
                  _const_spec((CH_BLOCKS, STATE_BLOCK, LANES)),
                  _const_spec((1, D_MODEL)),
                  _layer_spec((D_MODEL, 2 * D_MODEL), layer),
                  _const_spec((1, 2 * D_MODEL))],
        out_specs=(pl.BlockSpec(block, lambda i: (0, i, 0)),
                   pl.BlockSpec((rb, STATE_W), lambda i: (0, 0)),
                   pl.BlockSpec((rb, STATE_W), lambda i: (0, 0))),
        out_shape=(jax.ShapeDtypeStruct(h_in.shape, F32), state, state),
        scratch_shapes=[pltpu.VMEM((CH_BLOCKS, rows, LANES), F32),
                        pltpu.VMEM((rows, STATE_W), F32), pltpu.VMEM((rows, STATE_W), F32)],
        compiler_params=_params("arbitrary"),
        name="s5",
    )(h_in, g.reshape(1, D_MODEL), s0r, s0i, ar, ai, wb, wcr, wci, d_skip.reshape(1, D_MODEL),
      glu_w, glu_b.reshape(1, 2 * D_MODEL))
    return out.reshape(h.shape), sr, si


def _s5_pair_kernel(h_ref, g_ref, ar_ref, ai_ref, win_ref, wor_ref, woi_ref, wl_ref, d_ref,
                    gw_ref, gb_ref, o_ref, sr_ref, si_ref, u_s, xr_s, xi_s, *, rb, tt):
    pairs = tt // 2
    prow = pairs * rb

    def seq_rows(s):
        return pl.ds(s, tt, stride=rb)

    def split(x):
        x3 = x.reshape(pairs, 2 * rb, x.shape[-1])
        return x3[:, :rb].reshape(prow, x.shape[-1]), x3[:, rb:].reshape(prow, x.shape[-1])

    def interleave(even, odd):
        n = even.shape[-1]
        return jnp.concatenate([even.reshape(pairs, rb, n), odd.reshape(pairs, rb, n)], axis=1).reshape(tt * rb, n)

    @pl.when(pl.program_id(0) == 0)
    def _():
        sr_ref[...] = jnp.zeros_like(sr_ref)
        si_ref[...] = jnp.zeros_like(si_ref)

    for s in range(rb):
        un = _rms(h_ref[s], g_ref[...])
        for k in range(CH_BLOCKS):
            u_s[k, seq_rows(s), :] = un[:, k * LANES:(k + 1) * LANES]

    lhs = []
    for k in range(CH_BLOCKS):
        ks = slice(k * STATE_BLOCK, (k + 1) * STATE_BLOCK)
        lhs.append(jnp.concatenate(split(u_s[k]), axis=1).astype(BF16))
        w = jnp.dot(lhs[k], win_ref[k], preferred_element_type=F32)
        xr_s[:, ks] = w[:, :STATE_BLOCK]
        xi_s[:, ks] = w[:, STATE_BLOCK:]

    for c in range(STATE_W // SCAN_COLS):
        cs = slice(c * SCAN_COLS, (c + 1) * SCAN_COLS)
        a_r, a_i = ar_ref[:, cs], ai_ref[:, cs]
        a2_r = jnp.broadcast_to(a_r * a_r - a_i * a_i, (rb, SCAN_COLS))
        a2_i = jnp.broadcast_to(2.0 * a_r * a_i, (rb, SCAN_COLS))

        def step(j, carry, cs=cs, a2_r=a2_r, a2_i=a2_i):
            s_r, s_i = carry
            rows = pl.ds(pl.multiple_of(j * rb, rb), rb)
            n_r = a2_r * s_r - a2_i * s_i + xr_s[rows, cs]
            n_i = a2_r * s_i + a2_i * s_r + xi_s[rows, cs]
            xr_s[rows, cs] = s_r
            xi_s[rows, cs] = s_i
            return n_r, n_i

        s_r, s_i = lax.fori_loop(0, pairs, step, (sr_ref[:, cs], si_ref[:, cs]), unroll=True)
        sr_ref[:, cs] = s_r
        si_ref[:, cs] = s_i

    ys = []
    for k in range(CH_BLOCKS):
        ks = slice(k * STATE_BLOCK, (k + 1) * STATE_BLOCK)
        y2 = (jnp.dot(xr_s[:, ks].astype(BF16), wor_ref[k], preferred_element_type=F32)
              + jnp.dot(xi_s[:, ks].astype(BF16), woi_ref[k], preferred_element_type=F32)
              + jnp.dot(lhs[k], wl_ref[k], preferred_element_type=F32))
        ys.append(interleave(y2[:, :LANES], y2[:, LANES:]) + d_ref[:, k * LANES:(k + 1) * LANES] * u_s[k])
    act = jax.nn.gelu(jnp.concatenate(ys, axis=1)).astype(BF16)
    z = jnp.dot(act, gw_ref[...], preferred_element_type=F32) + gb_ref[...]
    mix = z[:, :D_MODEL] * jax.nn.sigmoid(z[:, D_MODEL:])
    for k in range(CH_BLOCKS):
        u_s[k] = mix[:, k * LANES:(k + 1) * LANES]
    for s in range(rb):
        o_ref[s] = h_ref[s] + jnp.concatenate([u_s[k, seq_rows(s), :] for k in range(CH_BLOCKS)], axis=1)


def _s5_pairs(h, g, ar, ai, w_in, w_out_r, w_out_i, w_loc, d_skip, glu_w, glu_b, *, layer, tt):
    rb, t_len, _ = h.shape
    assert t_len % tt == 0 and tt % 2 == 0 and rb % 8 == 0
    rows, prow = rb * tt, rb * tt // 2
    block = (rb, tt, D_MODEL)
    state = jax.ShapeDtypeStruct((rb, STATE_W), F32)
    return pl.pallas_call(
        functools.partial(_s5_pair_kernel, rb=rb, tt=tt),
        grid=(t_len // tt,),
        in_specs=[pl.BlockSpec(block, lambda i: (0, i, 0)),
                  _const_spec((1, D_MODEL)),
                  _const_spec((1, STATE_W)), _const_spec((1, STATE_W)),
                  _const_spec((CH_BLOCKS, 2 * LANES, 2 * STATE_BLOCK)),
                  _const_spec((CH_BLOCKS, STATE_BLOCK, 2 * LANES)),
                  _const_spec((CH_BLOCKS, STATE_BLOCK, 2 * LANES)),
                  _const_spec((CH_BLOCKS, 2 * LANES, 2 * LANES)),
                  _const_spec((1, D_MODEL)),
                  _layer_spec((D_MODEL, 2 * D_MODEL), layer),
                  _const_spec((1, 2 * D_MODEL))],
        out_specs=(pl.BlockSpec(block, lambda i: (0, i, 0)),
                   pl.BlockSpec((rb, STATE_W), lambda i: (0, 0)),
                   pl.BlockSpec((rb, STATE_W), lambda i: (0, 0))),
        out_shape=(jax.ShapeDtypeStruct(h.shape, F32), state, state),
        scratch_shapes=[pltpu.VMEM((CH_BLOCKS, rows, LANES), F32),
                        pltpu.VMEM((prow, STATE_W), F32), pltpu.VMEM((prow, STATE_W), F32)],
        compiler_params=_params("arbitrary"),
        name="s5_pairs",
    )(h, g.reshape(1, D_MODEL), ar, ai, w_in, w_out_r, w_out_i, w_loc, d_skip.reshape(1, D_MODEL),
      glu_w, glu_b.reshape(1, 2 * D_MODEL))


def _proj_kernel(x_ref, g_ref, w_ref, o_ref, *, scale):
    xn = _rms(x_ref[...], g_ref[...]).astype(BF16)
    y = jnp.dot(xn, w_ref[...], preferred_element_type=F32)
    o_ref[...] = y if scale == 1.0 else y * scale


def _proj(x, g, w, layer=None, scale=1.0, tm=512):
    n = x.shape[0]
    tm = min(tm, n)
    nout = w.shape[-1]
    return pl.pallas_call(
        functools.partial(_proj_kernel, scale=scale),
        grid=(n // tm,),
        in_specs=[pl.BlockSpec((tm, D_MODEL), lambda i: (i, 0)),
                  _const_spec((1, D_MODEL)),
                  _layer_spec((D_MODEL, nout), layer)],
        out_specs=pl.BlockSpec((tm, nout), lambda i: (i, 0)),
        out_shape=jax.ShapeDtypeStruct((n, nout), F32),
        compiler_params=_params("parallel"),
        name="proj",
    )(x, g.reshape(1, D_MODEL), w)


def _kv_tails(t_len, tm):
    tails = []
    for w in DIL_WINDOWS:
        n_tail = min(w, t_len)
        if n_tail >= tm:
            assert n_tail % tm == 0
            tails.append(((t_len - n_tail) // tm, 0, tm))
        else:
            tails.append((t_len // tm - 1, tm - n_tail, n_tail))
    return tails


def _kv_kernel(x_ref, g_ref, w_ref, kv_ref, *tail_refs, tails):
    j = pl.program_id(1)
    xn = _rms(x_ref[...], g_ref[...]).astype(BF16)
    kv = jnp.dot(xn, w_ref[...], preferred_element_type=F32)
    kv_ref[...] = kv
    for g, (first_tile, row0, n_tok) in enumerate(tails):
        def emit(g=g, row0=row0, n_tok=n_tok):
            tail_refs[g][...] = kv[row0:row0 + n_tok, g * 2 * ATTN_WIDTH:(g + 1) * 2 * ATTN_WIDTH].T

        if first_tile == 0:
            emit()
        else:
            pl.when(j >= first_tile)(emit)


def _kv_proj(x, g, w, tm=512):
    bn, t_len, _ = x.shape
    tm = min(tm, t_len)
    assert t_len % tm == 0
    nout = w.shape[-1]
    tails = _kv_tails(t_len, tm)
    tail_specs, tail_shapes = [], []
    for (first_tile, _, n_tok), win in zip(tails, DIL_WINDOWS):
        tail_specs.append(pl.BlockSpec((None, 2 * ATTN_WIDTH, n_tok),
                                       lambda b, j, f=first_tile: (b, 0, jnp.maximum(j - f, 0))))
        tail_shapes.append(jax.ShapeDtypeStruct((bn, 2 * ATTN_WIDTH, min(win, t_len)), F32))
    return pl.pallas_call(
        functools.partial(_kv_kernel, tails=tails),
        grid=(bn, t_len // tm),
        in_specs=[pl.BlockSpec((None, tm, D_MODEL), lambda b, j: (b, j, 0)),
                  _const_spec((1, D_MODEL)),
                  _const_spec((D_MODEL, nout))],
        out_specs=[pl.BlockSpec((None, tm, nout), lambda b, j: (b, j, 0))] + tail_specs,
        out_shape=[jax.ShapeDtypeStruct((bn, t_len, nout), F32)] + tail_shapes,
        compiler_params=_params("parallel", "arbitrary"),
        name="kv_proj",
    )(x, g.reshape(1, D_MODEL), w)


def _oproj_kernel(a_ref, w_ref, h_ref, o_ref):
    o_ref[...] = h_ref[...] + jnp.dot(a_ref[...].astype(BF16), w_ref[...], preferred_element_type=F32)


def _oproj(a, w, layer, h, tm=512):
    n = h.shape[0]
    tm = min(tm, n)
    return pl.pallas_call(
        _oproj_kernel,
        grid=(n // tm,),
        in_specs=[pl.BlockSpec((tm, ATTN_WIDTH), lambda i: (i, 0)),
                  _layer_spec((ATTN_WIDTH, D_MODEL), layer),
                  pl.BlockSpec((tm, D_MODEL), lambda i: (i, 0))],
        out_specs=pl.BlockSpec((tm, D_MODEL), lambda i: (i, 0)),
        out_shape=jax.ShapeDtypeStruct((n, D_MODEL), F32),
        compiler_params=_params("parallel"),
        name="oproj",
    )(a, w, h)


def _alibi_slopes():
    n = N_DIL * HEADS_PER_GROUP
    s = jnp.exp2(-8.0 * jnp.arange(1, n + 1, dtype=F32) / n)
    return s.reshape(N_DIL, HEADS_PER_GROUP)


def _prompt_bias():
    qi = jnp.arange(R_LEN)[:, None]
    kj = jnp.arange(2 * R_LEN)[None, :]
    diff = R_LEN + qi - kj
    valid = (diff >= 0) & (diff <= R_LEN)
    dil = jnp.asarray(DIL_RATES, jnp.int32)[:, None, None]
    dist = (dil * diff[None]).astype(F32)
    bias = -_alibi_slopes()[:, :, None, None] * dist[:, None]
    return jnp.where(valid[None, None], bias, NEG)


def _sample_bias():
    kk = R_LEN - jnp.arange(R_LEN)
    dil = jnp.asarray(DIL_RATES, jnp.int32)[:, None]
    dist = (dil * kk[None]).astype(F32)
    return -_alibi_slopes()[:, :, None] * dist[:, None]


MERGE_ROWS = 256
INTERLEAVE = 16


def _attn_kernel(q0, q1, q2, k0, k1, k2, v0, v1, v2, bias_ref, o_ref, acc_s, m_s, l_s, *, t_len):
    qs, ks, vs = (q0, q1, q2), (k0, k1, k2), (v0, v1, v2)
    lo = lax.broadcasted_iota(jnp.int32, (R_LEN, LANES), 1) < HEAD_DIM

    def block(g, q_idx, k_idx, first):
        q = qs[g][q_idx, :]
        qb = jnp.concatenate([jnp.where(lo, q, 0.0), jnp.where(lo, 0.0, q)], axis=0).astype(BF16)
        kb = ks[g][k_idx, :].astype(BF16)
        vb = vs[g][k_idx, :].astype(BF16)
        s = lax.dot_general(qb, kb, (((1,), (1,)), ((), ())), preferred_element_type=F32)
        bias = bias_ref[g, :, :, R_LEN:] if first else bias_ref[g]
        s = s + bias.reshape(2 * R_LEN, bias.shape[-1])
        m = jnp.max(s, axis=-1, keepdims=True)
        p = jnp.exp(s - m)
        l = jnp.sum(p, axis=-1, keepdims=True)
        a = jnp.dot(p.astype(BF16), vb, preferred_element_type=F32)
        acc_s[g, q_idx, :] = jnp.where(lo, a[:R_LEN], a[R_LEN:])
        m_s[g, q_idx, :] = jnp.where(lo, m[:R_LEN], m[R_LEN:])
        l_s[g, q_idx, :] = jnp.where(lo, l[:R_LEN], l[R_LEN:])

    for g in range(N_DIL):
        dil = DIL_RATES[g]
        span = R_LEN * dil
        n_blocks = t_len // span

        def rows(start, n, dil=dil):
            if dil > 1:
                return pl.ds(start, n, stride=dil)
            return pl.ds(start if isinstance(start, int) else pl.multiple_of(start, R_LEN), n)

        if dil <= INTERLEAVE:
            for r in range(dil):
                block(g, rows(r, R_LEN), rows(r, R_LEN), True)
        else:
            def first_blocks(i, c, g=g, rows=rows):
                for j in range(INTERLEAVE):
                    r = i * INTERLEAVE + j
                    block(g, rows(r, R_LEN), rows(r, R_LEN), True)
                return c

            lax.fori_loop(0, dil // INTERLEAVE, first_blocks, 0)

        later = n_blocks - 1
        if later == 0:
            continue
        per_trip = max(w for w in range(1, max(INTERLEAVE // dil, 1) + 1) if later % w == 0)

        def later_blocks(i, c, g=g, dil=dil, span=span, rows=rows, per_trip=per_trip):
            for j in range(per_trip):
                nb = 1 + i * per_trip + j
                for r in range(dil):
                    start = nb * span + r
                    block(g, rows(start, R_LEN), rows(start - span, 2 * R_LEN), False)
            return c

        if later == per_trip:
            later_blocks(0, 0)
        else:
            lax.fori_loop(0, later // per_trip, later_blocks, 0)

    def merge(i, c):
        rows = pl.ds(pl.multiple_of(i * MERGE_ROWS, MERGE_ROWS), MERGE_ROWS)
        ms = [m_s[g, rows, :] for g in range(N_DIL)]
        mx = jnp.maximum(jnp.maximum(ms[0], ms[1]), ms[2])
        ws = [jnp.exp(m - mx) for m in ms]
        den = ws[0] * l_s[0, rows, :] + ws[1] * l_s[1, rows, :] + ws[2] * l_s[2, rows, :]
        num = ws[0] * acc_s[0, rows, :] + ws[1] * acc_s[1, rows, :] + ws[2] * acc_s[2, rows, :]
        o_ref[rows, :] = num / den
        return c

    lax.fori_loop(0, t_len // MERGE_ROWS, merge, 0)


def _attn_prompt(q, kv, bias):
    bn, t_len, _ = q.shape
    pairs = ATTN_WIDTH // LANES
    per_group = 2 * ATTN_WIDTH // LANES

    def col(off):
        return pl.BlockSpec((None, t_len, LANES), lambda b, hp: (b, 0, off + hp))

    in_specs = ([col(g * pairs) for g in range(N_DIL)]
                + [col(g * per_group) for g in range(N_DIL)]
                + [col(g * per_group + pairs) for g in range(N_DIL)]
                + [pl.BlockSpec((N_DIL, 2, R_LEN, 2 * R_LEN), lambda b, hp: (0, hp, 0, 0))])
    scratch = pltpu.VMEM((N_DIL, t_len, LANES), F32)
    return pl.pallas_call(
        functools.partial(_attn_kernel, t_len=t_len),
        grid=(bn, pairs),
        in_specs=in_specs,
        out_specs=pl.BlockSpec((None, t_len, LANES), lambda b, hp: (b, 0, hp)),
        out_shape=jax.ShapeDtypeStruct((bn, t_len, ATTN_WIDTH), F32),
        scratch_shapes=[scratch, scratch, scratch],
        compiler_params=_params("parallel", "parallel"),
        name="attn_prompt",
    )(q, q, q, kv, kv, kv, kv, kv, kv, bias)


SAMPLE_BLOCK = 8


def _window_kernel(c_ref, o_ref, *, dil):
    w = c_ref.shape[-1]
    pos = lax.broadcasted_iota(jnp.int32, (w, R_LEN), 0)
    col = lax.broadcasted_iota(jnp.int32, (w, R_LEN), 1)
    pick = jnp.where(pos == col * dil, 1.0, 0.0).astype(BF16)
    o_ref[...] = jnp.dot(c_ref[...].astype(BF16), pick, preferred_element_type=F32).astype(BF16)


def _strided_window(cache, dil):
    bd, w = cache.shape[:2]
    assert w == R_LEN * dil
    slab = jnp.transpose(cache, (0, 2, 3, 4, 1)).reshape(bd, 2 * ATTN_WIDTH, w)
    if dil == 1:
        return slab
    return pl.pallas_call(
        functools.partial(_window_kernel, dil=dil),
        grid=(bd,),
        in_specs=[pl.BlockSpec((None, 2 * ATTN_WIDTH, w), lambda i: (i, 0, 0))],
        out_specs=pl.BlockSpec((None, 2 * ATTN_WIDTH, R_LEN), lambda i: (i, 0, 0)),
        out_shape=jax.ShapeDtypeStruct((bd, 2 * ATTN_WIDTH, R_LEN), BF16),
        compiler_params=_params("parallel"),
        name="cache_window",
    )(slab)


def _attn_sample_kernel(q_ref, kvn_ref, c0, c1, c2, bias_ref, o_ref):
    windows = (c0, c1, c2)
    sub = lax.broadcasted_iota(jnp.int32, (HEADS_PER_GROUP, ATTN_WIDTH), 0)
    lane = lax.broadcasted_iota(jnp.int32, (HEADS_PER_GROUP, ATTN_WIDTH), 1)
    own = (lane // HEAD_DIM) == sub
    for b in range(SAMPLE_BLOCK):
        os_, ms, ls = [], [], []
        for g in range(N_DIL):
            base = g * 2 * ATTN_WIDTH
            qbd = jnp.where(own, q_ref[b:b + 1, g * ATTN_WIDTH:(g + 1) * ATTN_WIDTH], 0.0)
            kt = windows[g][b, :ATTN_WIDTH, :].astype(BF16)
            vt = windows[g][b, ATTN_WIDTH:, :].astype(BF16)
            kn = kvn_ref[b:b + 1, base:base + ATTN_WIDTH]
            vn = kvn_ref[b:b + 1, base + ATTN_WIDTH:base + 2 * ATTN_WIDTH]
            s = jnp.dot(qbd.astype(BF16), kt, preferred_element_type=F32) + bias_ref[g]
            s_new = jnp.sum(qbd * kn, axis=-1, keepdims=True)
            m = jnp.maximum(jnp.max(s, axis=-1, keepdims=True), s_new)
            p = jnp.exp(s - m)
            p_new = jnp.exp(s_new - m)
            ls.append(jnp.sum(p, axis=-1, keepdims=True) + p_new)
            os_.append(lax.dot_general(p.astype(BF16), vt, (((1,), (1,)), ((), ())),
                                       preferred_element_type=F32) + p_new * vn)
            ms.append(m)
        mx = jnp.maximum(jnp.maximum(ms[0], ms[1]), ms[2])
        ws = [jnp.exp(m - mx) for m in ms]
        den = ws[0] * ls[0] + ws[1] * ls[1] + ws[2] * ls[2]
        o = (ws[0] * os_[0] + ws[1] * os_[1] + ws[2] * os_[2]) / den
        o_ref[b:b + 1, :] = jnp.sum(jnp.where(own, o, 0.0), axis=0, keepdims=True)


def _attn_sample(q, kv_new, windows, bias):
    bd = q.shape[0]
    assert bd % SAMPLE_BLOCK == 0
    window_spec = pl.BlockSpec((SAMPLE_BLOCK, 2 * ATTN_WIDTH, R_LEN), lambda i: (i, 0, 0))
    return pl.pallas_call(
        _attn_sample_kernel,
        grid=(bd // SAMPLE_BLOCK,),
        in_specs=[pl.BlockSpec((SAMPLE_BLOCK, N_DIL * ATTN_WIDTH), lambda i: (i, 0)),
                  pl.BlockSpec((SAMPLE_BLOCK, N_DIL * 2 * ATTN_WIDTH), lambda i: (i, 0))]
                 + [window_spec] * N_DIL + [_const_spec((N_DIL, HEADS_PER_GROUP, R_LEN))],
        out_specs=pl.BlockSpec((SAMPLE_BLOCK, ATTN_WIDTH), lambda i: (i, 0)),
        out_shape=jax.ShapeDtypeStruct((bd, ATTN_WIDTH), F32),
        compiler_params=_params("parallel"),
        name="attn_sample",
    )(q, kv_new, *windows, bias)


def kernel(x_prompt, x_sample, state_ssm_re, state_ssm_im, cache_kv_w128, cache_kv_w512, cache_kv_w2048, norm_ffn1, ffn1_w_in, ffn1_w_out, norm_mix, norm_ffn2, ffn2_w_in, ffn2_w_out, norm_kv, norm_final, ssm_log_dt, ssm_lambda_re, ssm_lambda_im, ssm_b_re, ssm_b_im, ssm_c_re, ssm_c_im, ssm_d, glu_w, glu_b, attn_w_q, attn_w_kv, attn_w_o):
    bn, t_len, _ = x_prompt.shape
    bd = x_sample.shape[0]
    assert x_sample.shape[1] == 1

    glu_wb = glu_w.astype(BF16)
    w_q, w_kv, w_o = attn_w_q.astype(BF16), attn_w_kv.astype(BF16), attn_w_o.astype(BF16)
    q_scale = HEAD_DIM ** -0.5

    ssm, ssm_pairs = [], []
    for layer in range(N_A_LAYERS):
        ar, ai, wb, wcr, wci, *pair_w = _ssm_weights(
            ssm_log_dt[layer], ssm_lambda_re[layer], ssm_lambda_im[layer],
            ssm_b_re[layer], ssm_b_im[layer], ssm_c_re[layer], ssm_c_im[layer])
        ar, ai = ar.reshape(1, STATE_W), ai.reshape(1, STATE_W)
        tail = (ssm_d[layer].reshape(D_MODEL), glu_wb, glu_b[layer])
        ssm.append((ar, ai, wb, wcr, wci) + tail)
        ssm_pairs.append((ar, ai, *pair_w) + tail)

    def states(parts, n_seq):
        return jnp.stack([p.reshape(n_seq, N_SSM_GROUPS, SSM_STATE) for p in parts])

    s0r = state_ssm_re.reshape(N_A_LAYERS, bd, STATE_W)
    s0i = state_ssm_im.reshape(N_A_LAYERS, bd, STATE_W)
    windows = [_strided_window(cache, dil)
               for cache, dil in zip((cache_kv_w128, cache_kv_w512, cache_kv_w2048), DIL_RATES)]
    sample_bias, prompt_bias = _sample_bias(), _prompt_bias()
    fin_r, fin_i, fin_rs, fin_is = [], [], [], []
    h = x_prompt.reshape(bn * t_len, D_MODEL)
    hs = x_sample.reshape(bd, D_MODEL)
    hs, w1 = _ffn_cast(hs, norm_ffn1[0], ffn1_w_in, ffn1_w_out, 0)
    for layer in range(DEPTH):
        last = layer == DEPTH - 1
        if layer == N_A_LAYERS:
            kv_p, *kv_tails = _kv_proj(h.reshape(bn, t_len, D_MODEL), norm_kv, w_kv)
            kv_s = _proj(hs, norm_kv, w_kv)
        if layer > 0:
            hs = _ffn(hs, norm_ffn1[layer], w1)
        cast2 = (ffn2_w_in, ffn2_w_out, layer)
        if layer < N_A_LAYERS:
            h, w2 = _ffn(h, norm_ffn1[layer], w1, cast_next=cast2)
            h, sr, si = _s5_pairs(h.reshape(bn, t_len, D_MODEL), norm_mix[layer],
                                  *ssm_pairs[layer], layer=layer, tt=64)
            h = h.reshape(bn * t_len, D_MODEL)
            fin_r.append(sr)
            fin_i.append(si)
            hs, sr, si = _s5(hs.reshape(bd, 1, D_MODEL), norm_mix[layer], s0r[layer], s0i[layer],
                             *ssm[layer], layer=layer, tt=1)
            hs = hs.reshape(bd, D_MODEL)
            fin_rs.append(sr)
            fin_is.append(si)
            attn = None
        else:
            b = layer - N_A_LAYERS
            (h, q), w2 = _ffn(h, norm_ffn1[layer], w1, q_proj=(norm_mix[layer], w_q, b, q_scale), cast_next=cast2)
            merged = _attn_prompt(q.reshape(bn, t_len, -1), kv_p, prompt_bias)
            attn = (merged.reshape(bn * t_len, ATTN_WIDTH), w_o, b)
            qs = _proj(hs, norm_mix[layer], w_q, layer=b, scale=q_scale)
            hs = _oproj(_attn_sample(qs, kv_s, windows, sample_bias), w_o, b, hs)
        final_g = norm_final if last else None
        hs = _ffn(hs, norm_ffn2[layer], w2, final_g=final_g)
        if last:
            h = _ffn(h, norm_ffn2[layer], w2, attn=attn, final_g=final_g)
        else:
            h, w1 = _ffn(h, norm_ffn2[layer], w2, attn=attn, cast_next=(ffn1_w_in, ffn1_w_out, layer + 1))
    y_prompt = h.reshape(bn, t_len, D_MODEL)
    y_sample = hs.reshape(bd, 1, D_MODEL)
    ssm_re_p, ssm_im_p = states(fin_r, bn), states(fin_i, bn)
    ssm_re_s, ssm_im_s = states(fin_rs, bd), states(fin_is, bd)
    kv_s = kv_s.reshape(bd, 1, N_DIL, 2, HEADS_PER_GROUP, HEAD_DIM)
    kv_out_p = [jnp.transpose(tail.reshape(bn, 2, HEADS_PER_GROUP, HEAD_DIM, -1), (0, 4, 1, 2, 3))
                for tail in kv_tails]

    return (y_prompt, y_sample, ssm_re_p, ssm_im_p, kv_out_p[0], kv_out_p[1], kv_out_p[2],
            ssm_re_s, ssm_im_s, kv_s[:, :, 0], kv_s[:, :, 1], kv_s[:, :, 2])
```

```python
import functools

import jax
import jax.numpy as jnp
from jax import lax
from jax.experimental import pallas as pl
from jax.experimental.pallas import tpu as pltpu

F32 = jnp.float32
BF16 = jnp.bfloat16

D_MODEL = 1024
DEPTH = 4
N_A_LAYERS = DEPTH // 2
SSM_GROUP = 16
N_SSM_GROUPS = D_MODEL // SSM_GROUP
SSM_STATE = 64
STATE_W = N_SSM_GROUPS * SSM_STATE
DIL_WINDOWS = (128, 512, 2048)
DIL_RATES = (1, 4, 16)
N_DIL = 3
HEADS_PER_GROUP = 8
HEAD_DIM = 64
ATTN_WIDTH = HEADS_PER_GROUP * HEAD_DIM
R_LEN = DIL_WINDOWS[0] // DIL_RATES[0]
D_FF = 2816
EPS = 1e-6
NEG = -1e30

LANES = 128
MXU_TILE = 256
BF16_SUBLANES = 16
CH_BLOCKS = D_MODEL // LANES
GROUPS_PER_BLOCK = LANES // SSM_GROUP
STATE_BLOCK = GROUPS_PER_BLOCK * SSM_STATE
VMEM_LIMIT_BYTES = 56 * 1024 * 1024

assert all(w // d == R_LEN for w, d in zip(DIL_WINDOWS, DIL_RATES))


def _params(*semantics):
    return pltpu.CompilerParams(dimension_semantics=semantics, vmem_limit_bytes=VMEM_LIMIT_BYTES)


def _const_spec(shape):
    zeros = (0,) * len(shape)
    return pl.BlockSpec(shape, lambda *_: zeros, pipeline_mode=pl.Buffered(1))


def _layer_spec(shape, layer):
    if layer is None:
        return _const_spec(shape)
    idx = (layer,) + (0,) * len(shape)
    return pl.BlockSpec((None,) + tuple(shape), lambda *_: idx, pipeline_mode=pl.Buffered(1))


def _rms(x, g):
    return x * lax.rsqrt(jnp.mean(x * x, axis=-1, keepdims=True) + EPS) * g


def _ff_chunks(n_chunks):
    tiles = D_FF // MXU_TILE
    assert tiles * MXU_TILE == D_FF and n_chunks <= tiles
    cuts = [MXU_TILE * ((tiles * c + n_chunks - 1) // n_chunks) for c in range(n_chunks + 1)]
    return list(zip(cuts[:-1], cuts[1:]))


def _swiglu_residual(x, g, wa_ref, wb_ref, wout_ref, n_chunks):
    xn = _rms(x, g).astype(BF16)
    acc = None
    for lo, hi in _ff_chunks(n_chunks):
        a = jnp.dot(xn, wa_ref[:, lo:hi], preferred_element_type=F32)
        b = jnp.dot(xn, wb_ref[:, lo:hi], preferred_element_type=F32)
        act = (a * jax.nn.sigmoid(a) * b).astype(BF16)
        d = jnp.dot(act, wout_ref[lo:hi, :], preferred_element_type=F32)
        acc = d if acc is None else acc + d
    return x + 0.5 * acc


def _ffn_kernel(*refs, n_chunks, attn_in, final, q_scale, cast_next, extra_steps):
    refs = list(refs)
    x_ref = refs.pop(0)
    if attn_in:
        a_ref, wo_ref = refs.pop(0), refs.pop(0)
    g_ref, wa_ref, wb_ref, wout_ref = (refs.pop(0) for _ in range(4))
    if final:
        gf_ref = refs.pop(0)
    if q_scale is not None:
        gq_ref, wq_ref = refs.pop(0), refs.pop(0)
    if cast_next:
        next_f32 = [refs.pop(0) for _ in range(3)]
    if extra_steps:
        xs_ref = refs.pop(0)
    o_ref = refs.pop(0)
    if q_scale is not None:
        q_ref = refs.pop(0)
    if cast_next:
        for src in next_f32:
            refs.pop(0)[...] = src[...].astype(BF16)

    def main():
        x = x_ref[...]
        if attn_in:
            x = x + jnp.dot(a_ref[...].astype(BF16), wo_ref[...], preferred_element_type=F32)
        y = _swiglu_residual(x, g_ref[...], wa_ref, wb_ref, wout_ref, n_chunks)
        if final:
            y = _rms(y, gf_ref[...])
        o_ref[...] = y
        if q_scale is not None:
            q = jnp.dot(_rms(y, gq_ref[...]).astype(BF16), wq_ref[...], preferred_element_type=F32)
            q_ref[...] = q * q_scale

    if not extra_steps:
        main()
        return
    ys_ref = refs.pop(0)
    pl.when(pl.program_id(0) < extra_steps)(main)

    @pl.when(pl.program_id(0) == extra_steps)
    def _():
        ys = _swiglu_residual(xs_ref[...], g_ref[...], wa_ref, wb_ref, wout_ref, n_chunks)
        ys_ref[...] = _rms(ys, gf_ref[...]) if final else ys


def _ffn(x, g, weights, attn=None, final_g=None, q_proj=None, cast_next=None, extra=None, tm=512, n_chunks=2):
    n = x.shape[0]
    tm = min(tm, n)
    assert n % tm == 0
    steps = n // tm
    step = (lambda i: jnp.minimum(i, steps - 1)) if extra is not None else (lambda i: i)
    rows = lambda width: pl.BlockSpec((tm, width), lambda i: (step(i), 0))
    in_specs, args = [rows(D_MODEL)], [x]
    if attn is not None:
        merged, w_o, idx = attn
        in_specs += [rows(ATTN_WIDTH), _layer_spec((ATTN_WIDTH, D_MODEL), idx)]
        args += [merged, w_o]
    in_specs += [_const_spec((1, D_MODEL)), _const_spec((D_MODEL, D_FF)), _const_spec((D_MODEL, D_FF)),
                 _const_spec((D_FF, D_MODEL))]
    args += [g.reshape(1, D_MODEL), *weights]
    if final_g is not None:
        in_specs.append(_const_spec((1, D_MODEL)))
        args.append(final_g.reshape(1, D_MODEL))
    out_specs, out_shape = [rows(D_MODEL)], [jax.ShapeDtypeStruct((n, D_MODEL), F32)]
    q_scale = None
    if q_proj is not None:
        gq, w_q, idx, q_scale = q_proj
        nq = w_q.shape[-1]
        in_specs += [_const_spec((1, D_MODEL)), _layer_spec((D_MODEL, nq), idx)]
        args += [gq.reshape(1, D_MODEL), w_q]
        out_specs.append(rows(nq))
        out_shape.append(jax.ShapeDtypeStruct((n, nq), F32))
    if cast_next is not None:
        w_in, w_out, idx = cast_next
        r_in = D_MODEL // steps
        out_blocks = [b for b in range(1, steps + 1) if steps % b == 0 and D_FF % b == 0
                      and (D_FF // b) % BF16_SUBLANES == 0][-1]
        r_out, hold = D_FF // out_blocks, steps // out_blocks
        assert r_in * steps == D_MODEL and r_in % BF16_SUBLANES == 0
        in_specs += [pl.BlockSpec((None, r_in, D_FF), lambda i: (idx, step(i), 0)),
                     pl.BlockSpec((None, r_in, D_FF), lambda i: (idx, step(i), 1)),
                     pl.BlockSpec((None, r_out, D_MODEL), lambda i: (idx, step(i) // hold, 0))]
        args += [w_in, w_in, w_out]
        out_specs += [pl.BlockSpec((r_in, D_FF), lambda i: (step(i), 0)),
                      pl.BlockSpec((r_in, D_FF), lambda i: (step(i), 0)),
                      pl.BlockSpec((r_out, D_MODEL), lambda i: (step(i) // hold, 0))]
        out_shape += [jax.ShapeDtypeStruct((D_MODEL, D_FF), BF16), jax.ShapeDtypeStruct((D_MODEL, D_FF), BF16),
                      jax.ShapeDtypeStruct((D_FF, D_MODEL), BF16)]
    if extra is not None:
        in_specs.append(_const_spec(extra.shape))
        args.append(extra)
        out_specs.append(pl.BlockSpec(extra.shape, lambda i: (0, 0)))
        out_shape.append(jax.ShapeDtypeStruct(extra.shape, F32))
    out = pl.pallas_call(
        functools.partial(_ffn_kernel, n_chunks=n_chunks, attn_in=attn is not None,
                          final=final_g is not None, q_scale=q_scale, cast_next=cast_next is not None,
                          extra_steps=steps if extra is not None else 0),
        grid=(steps + (extra is not None),),
        in_specs=in_specs,
        out_specs=out_specs,
        out_shape=out_shape,
        compiler_params=_params("arbitrary"),
        name="ffn",
    )(*args)
    if len(out) == 1:
        return out[0]
    parts = [out[0]]
    k = 1
    if q_proj is not None:
        parts.append(out[k])
        k += 1
    if cast_next is not None:
        parts.append(tuple(out[k:k + 3]))
        k += 3
    if extra is not None:
        parts.append(out[k])
    return tuple(parts)


def _ffn_cast_kernel(x_ref, g_ref, wa_ref, wb_ref, wout_ref, *rest, final):
    if final:
        gf_ref, o_ref, wa_o, wb_o, wout_o, acc_s = rest
    else:
        o_ref, wa_o, wb_o, wout_o, acc_s = rest
    c = pl.program_id(0)
    wa = wa_ref[...].astype(BF16)
    wb = wb_ref[...].astype(BF16)
    wout = wout_ref[...].astype(BF16)
    wa_o[...] = wa
    wb_o[...] = wb
    wout_o[...] = wout
    x = x_ref[...]
    xn = _rms(x, g_ref[...]).astype(BF16)
    a = jnp.dot(xn, wa, preferred_element_type=F32)
    b = jnp.dot(xn, wb, preferred_element_type=F32)
    d = jnp.dot((a * jax.nn.sigmoid(a) * b).astype(BF16), wout, preferred_element_type=F32)

    @pl.when(c == 0)
    def _():
        acc_s[...] = d

    @pl.when(c > 0)
    def _():
        acc_s[...] += d

    @pl.when(c == pl.num_programs(0) - 1)
    def _():
        y = x + 0.5 * acc_s[...]
        o_ref[...] = _rms(y, gf_ref[...]) if final else y


def _ffn_cast(x, g, w_in, w_out, layer, final_g=None):
    n = x.shape[0]
    n_chunks = D_FF // MXU_TILE
    final = final_g is not None
    in_specs = [_const_spec((n, D_MODEL)), _const_spec((1, D_MODEL)),
                pl.BlockSpec((None, D_MODEL, MXU_TILE), lambda c: (layer, 0, c)),
                pl.BlockSpec((None, D_MODEL, MXU_TILE), lambda c: (layer, 0, n_chunks + c)),
                pl.BlockSpec((None, MXU_TILE, D_MODEL), lambda c: (layer, c, 0))]
    args = [x, g.reshape(1, D_MODEL), w_in, w_in, w_out]
    if final:
        in_specs.append(_const_spec((1, D_MODEL)))
        args.append(final_g.reshape(1, D_MODEL))
    y, wa, wb, wo = pl.pallas_call(
        functools.partial(_ffn_cast_kernel, final=final),
        grid=(n_chunks,),
        in_specs=in_specs,
        out_specs=[pl.BlockSpec((n, D_MODEL), lambda c: (0, 0)),
                   pl.BlockSpec((D_MODEL, MXU_TILE), lambda c: (0, c)),
                   pl.BlockSpec((D_MODEL, MXU_TILE), lambda c: (0, c)),
                   pl.BlockSpec((MXU_TILE, D_MODEL), lambda c: (c, 0))],
        out_shape=[jax.ShapeDtypeStruct((n, D_MODEL), F32),
                   jax.ShapeDtypeStruct((D_MODEL, D_FF), BF16),
                   jax.ShapeDtypeStruct((D_MODEL, D_FF), BF16),
                   jax.ShapeDtypeStruct((D_FF, D_MODEL), BF16)],
        scratch_shapes=[pltpu.VMEM((n, D_MODEL), F32)],
        compiler_params=_params("arbitrary"),
        name="ffn_cast",
    )(*args)
    return y, (wa, wb, wo)


def _ssm_weights_kernel(ldt_ref, lr_ref, li_ref, br_ref, bi_ref, cr_ref, ci_ref,
                        ar_ref, ai_ref, wb_ref, wcr_ref, wci_ref, win_ref, wor_ref, woi_ref, wl_ref):
    dt = jnp.exp(ldt_ref[...])
    lr = lr_ref[...]
    li = li_ref[...]
    mag = jnp.exp(lr * dt)
    ar = mag * jnp.cos(li * dt)
    ai = mag * jnp.sin(li * dt)
    den = lr * lr + li * li
    zr = ((ar - 1.0) * lr + ai * li) / den
    zi = (ai * lr - (ar - 1.0) * li) / den
    ar_ref[...] = ar
    ai_ref[...] = ai
    br = br_ref[...]
    bi = bi_ref[...]
    bbr = zr[:, None, :] * br - zi[:, None, :] * bi
    bbi = zr[:, None, :] * bi + zi[:, None, :] * br
    a1r, a1i = ar[:, None, :], ai[:, None, :]
    a2r, a2i = (ar * ar - ai * ai)[:, None, :], (2.0 * ar * ai)[:, None, :]
    abr = a1r * bbr - a1i * bbi
    abi = a1r * bbi + a1i * bbr
    cr = cr_ref[...]
    ci = ci_ref[...]
    car = cr * a1r - ci * a1i
    cai = cr * a1i + ci * a1r
    ca2r = cr * a2r - ci * a2i
    ca2i = cr * a2i + ci * a2r

    def re_dot(xr, xi):
        dims = (((2,), (2,)), ((0,), (0,)))
        return (lax.dot_general(xr, bbr, dims, preferred_element_type=F32)
                - lax.dot_general(xi, bbi, dims, preferred_element_type=F32))

    k0 = re_dot(cr, ci)
    k1 = re_dot(car, cai)

    def same_group(n_rows, row_span, n_cols, col_span):
        r = lax.broadcasted_iota(jnp.int32, (n_rows, n_cols), 0) // row_span
        c = lax.broadcasted_iota(jnp.int32, (n_rows, n_cols), 1) // col_span
        return r == c

    def replicate(n, copies, transpose):
        shape = (copies * n, n) if transpose else (n, copies * n)
        r = lax.broadcasted_iota(jnp.int32, shape, 0)
        c = lax.broadcasted_iota(jnp.int32, shape, 1)
        return jnp.where(r % n == c % n, 1.0, 0.0).astype(BF16)

    gpb = GROUPS_PER_BLOCK
    rep_p = replicate(SSM_STATE, gpb, False)
    rep_pt = replicate(SSM_STATE, gpb, True)
    rep_ht = replicate(SSM_GROUP, gpb, True)
    m_in = same_group(LANES, SSM_GROUP, STATE_BLOCK, SSM_STATE)
    m_out = same_group(STATE_BLOCK, SSM_STATE, LANES, SSM_GROUP)
    m_mix = same_group(LANES, SSM_GROUP, LANES, SSM_GROUP)
    nt = (((1,), (1,)), ((), ()))

    def rows_of(x, k):
        return x[k * gpb:(k + 1) * gpb].reshape(LANES, x.shape[-1]).astype(BF16)

    def bd_in(x, k):
        return jnp.where(m_in, jnp.dot(rows_of(x, k), rep_p, preferred_element_type=F32), 0.0)

    def bd_out(x, k):
        return jnp.where(m_out, lax.dot_general(rep_pt, rows_of(x, k), nt, preferred_element_type=F32), 0.0)

    def bd_mix(x, k):
        return jnp.where(m_mix, lax.dot_general(rep_ht, rows_of(x, k), nt, preferred_element_type=F32), 0.0)

    for k in range(CH_BLOCKS):
        one_step = jnp.concatenate([bd_in(bbr, k), bd_in(bbi, k)], axis=1)
        wb_ref[k] = one_step.astype(BF16)
        wcr_ref[k] = bd_out(cr, k).astype(BF16)
        wci_ref[k] = (-bd_out(ci, k)).astype(BF16)
        win_ref[k] = jnp.concatenate([jnp.concatenate([bd_in(abr, k), bd_in(abi, k)], axis=1), one_step],
                                     axis=0).astype(BF16)
        wor_ref[k] = jnp.concatenate([bd_out(car, k), bd_out(ca2r, k)], axis=1).astype(BF16)
        woi_ref[k] = (-jnp.concatenate([bd_out(cai, k), bd_out(ca2i, k)], axis=1)).astype(BF16)
        m0, m1 = bd_mix(k0, k), bd_mix(k1, k)
        wl_ref[k] = jnp.concatenate([jnp.concatenate([m0, m1], axis=1),
                                     jnp.concatenate([jnp.zeros_like(m0), m0], axis=1)], axis=0).astype(BF16)


def _ssm_weights(log_dt, lam_re, lam_im, b_re, b_im, c_re, c_im):
    g, p = N_SSM_GROUPS, SSM_STATE
    gp = jax.ShapeDtypeStruct((g, p), F32)
    stack = lambda rows, cols: jax.ShapeDtypeStruct((CH_BLOCKS, rows, cols), BF16)
    return pl.pallas_call(
        _ssm_weights_kernel,
        out_shape=(gp, gp,
                   stack(LANES, 2 * STATE_BLOCK), stack(STATE_BLOCK, LANES), stack(STATE_BLOCK, LANES),
                   stack(2 * LANES, 2 * STATE_BLOCK), stack(STATE_BLOCK, 2 * LANES), stack(STATE_BLOCK, 2 * LANES),
                   stack(2 * LANES, 2 * LANES)),
        compiler_params=pltpu.CompilerParams(vmem_limit_bytes=VMEM_LIMIT_BYTES),
        name="s5_weights",
    )(log_dt.reshape(g, 1), lam_re, lam_im, jnp.swapaxes(b_re, 1, 2), jnp.swapaxes(b_im, 1, 2), c_re, c_im)


SCAN_COLS = 512


def _s5_kernel(h_ref, g_ref, s0r_ref, s0i_ref, ar_ref, ai_ref, wb_ref, wcr_ref, wci_ref, d_ref,
               gw_ref, gb_ref, o_ref, sr_ref, si_ref, u_s, xr_s, xi_s, *, rb, tt):
    n_slab, per_slab = h_ref.shape[:2]
    assert n_slab * per_slab == rb * tt and (n_slab == rb or tt == 1)

    def seq_rows(s):
        return pl.ds(s, per_slab, stride=n_slab) if n_slab > 1 else pl.ds(0, per_slab)

    @pl.when(pl.program_id(0) == 0)
    def _():
        sr_ref[...] = s0r_ref[...]
        si_ref[...] = s0i_ref[...]

    for s in range(n_slab):
        un = _rms(h_ref[s], g_ref[...])
        for k in range(CH_BLOCKS):
            u_s[k, seq_rows(s), :] = un[:, k * LANES:(k + 1) * LANES]
    for k in range(CH_BLOCKS):
        bu = jnp.dot(u_s[k].astype(BF16), wb_ref[k], preferred_element_type=F32)
        xr_s[:, k * STATE_BLOCK:(k + 1) * STATE_BLOCK] = bu[:, :STATE_BLOCK]
        xi_s[:, k * STATE_BLOCK:(k + 1) * STATE_BLOCK] = bu[:, STATE_BLOCK:]

    for c in range(STATE_W // SCAN_COLS):
        cs = slice(c * SCAN_COLS, (c + 1) * SCAN_COLS)
        a_r = jnp.broadcast_to(ar_ref[:, cs], (rb, SCAN_COLS))
        a_i = jnp.broadcast_to(ai_ref[:, cs], (rb, SCAN_COLS))

        def step(t, carry, cs=cs, a_r=a_r, a_i=a_i):
            s_r, s_i = carry
            rows = pl.ds(pl.multiple_of(t * rb, rb), rb)
            n_r = a_r * s_r - a_i * s_i + xr_s[rows, cs]
            n_i = a_r * s_i + a_i * s_r + xi_s[rows, cs]
            xr_s[rows, cs] = n_r
            xi_s[rows, cs] = n_i
            return n_r, n_i

        s_r, s_i = lax.fori_loop(0, tt, step, (sr_ref[:, cs], si_ref[:, cs]), unroll=True)
        sr_ref[:, cs] = s_r
        si_ref[:, cs] = s_i

    ys = []
    for k in range(CH_BLOCKS):
        ks = slice(k * STATE_BLOCK, (k + 1) * STATE_BLOCK)
        ys.append(jnp.dot(xr_s[:, ks].astype(BF16), wcr_ref[k], preferred_element_type=F32)
                  + jnp.dot(xi_s[:, ks].astype(BF16), wci_ref[k], preferred_element_type=F32)
                  + d_ref[:, k * LANES:(k + 1) * LANES] * u_s[k])
    act = jax.nn.gelu(jnp.concatenate(ys, axis=1)).astype(BF16)
    z = jnp.dot(act, gw_ref[...], preferred_element_type=F32) + gb_ref[...]
    mix = z[:, :D_MODEL] * jax.nn.sigmoid(z[:, D_MODEL:])
    for k in range(CH_BLOCKS):
        u_s[k] = mix[:, k * LANES:(k + 1) * LANES]
    for s in range(n_slab):
        o_ref[s] = h_ref[s] + jnp.concatenate([u_s[k, seq_rows(s), :] for k in range(CH_BLOCKS)], axis=1)


def _s5(h, g, s0r, s0i, ar, ai, wb, wcr, wci, d_skip, glu_w, glu_b, *, layer, tt):
    rb, t_len, _ = h.shape
    assert t_len % tt == 0 and rb % 8 == 0
    rows = rb * tt
    block = (1, rb, D_MODEL) if t_len == 1 else (rb, tt, D_MODEL)
    h_in = h.reshape(1, rb, D_MODEL) if t_len == 1 else h
    state = jax.ShapeDtypeStruct((rb, STATE_W), F32)
    out, sr, si = pl.pallas_call(
        functools.partial(_s5_kernel, rb=rb, tt=tt),
        grid=(t_len // tt,),
        in_specs=[pl.BlockSpec(block, lambda i: (0, i, 0)),
                  _const_spec((1, D_MODEL)),
                  _const_spec((rb, STATE_W)), _const_spec((rb, STATE_W)),
                  _const_spec((1, STATE_W)), _const_spec((1, STATE_W)),
                  _const_spec((CH_BLOCKS, LANES, 2 * STATE_BLOCK)),
                  _const_spec((CH_BLOCKS, STATE_BLOCK, LANES)),
                  _const_spec((CH_BLOCKS, STATE_BLOCK, LANES)),
                  _const_spec((1, D_MODEL)),
                  _layer_spec((D_MODEL, 2 * D_MODEL), layer),
                  _const_spec((1, 2 * D_MODEL))],
        out_specs=(pl.BlockSpec(block, lambda i: (0, i, 0)),
                   pl.BlockSpec((rb, STATE_W), lambda i: (0, 0)),
                   pl.BlockSpec((rb, STATE_W), lambda i: (0, 0))),
        out_shape=(jax.ShapeDtypeStruct(h_in.shape, F32), state, state),
        scratch_shapes=[pltpu.VMEM((CH_BLOCKS, rows, LANES), F32),
                        pltpu.VMEM((rows, STATE_W), F32), pltpu.VMEM((rows, STATE_W), F32)],
        compiler_params=_params("arbitrary"),
        name="s5",
    )(h_in, g.reshape(1, D_MODEL), s0r, s0i, ar, ai, wb, wcr, wci, d_skip.reshape(1, D_MODEL),
      glu_w, glu_b.reshape(1, 2 * D_MODEL))
    return out.reshape(h.shape), sr, si


def _s5_pair_kernel(h_ref, g_ref, ar_ref, ai_ref, win_ref, wor_ref, woi_ref, wl_ref, d_ref,
                    gw_ref, gb_ref, o_ref, sr_ref, si_ref, u_s, xr_s, xi_s, *, rb, tt):
    pairs = tt // 2
    prow = pairs * rb

    def seq_rows(s):
        return pl.ds(s, tt, stride=rb)

    def split(x):
        x3 = x.reshape(pairs, 2 * rb, x.shape[-1])
        return x3[:, :rb].reshape(prow, x.shape[-1]), x3[:, rb:].reshape(prow, x.shape[-1])

    def interleave(even, odd):
        n = even.shape[-1]
        return jnp.concatenate([even.reshape(pairs, rb, n), odd.reshape(pairs, rb, n)], axis=1).reshape(tt * rb, n)

    @pl.when(pl.program_id(0) == 0)
    def _():
        sr_ref[...] = jnp.zeros_like(sr_ref)
        si_ref[...] = jnp.zeros_like(si_ref)

    for s in range(rb):
        un = _rms(h_ref[s], g_ref[...])
        for k in range(CH_BLOCKS):
            u_s[k, seq_rows(s), :] = un[:, k * LANES:(k + 1) * LANES]

    lhs = []
    for k in range(CH_BLOCKS):
        ks = slice(k * STATE_BLOCK, (k + 1) * STATE_BLOCK)
        lhs.append(jnp.concatenate(split(u_s[k]), axis=1).astype(BF16))
        w = jnp.dot(lhs[k], win_ref[k], preferred_element_type=F32)
        xr_s[:, ks] = w[:, :STATE_BLOCK]
        xi_s[:, ks] = w[:, STATE_BLOCK:]

    for c in range(STATE_W // SCAN_COLS):
        cs = slice(c * SCAN_COLS, (c + 1) * SCAN_COLS)
        a_r, a_i = ar_ref[:, cs], ai_ref[:, cs]
        a2_r = jnp.broadcast_to(a_r * a_r - a_i * a_i, (rb, SCAN_COLS))
        a2_i = jnp.broadcast_to(2.0 * a_r * a_i, (rb, SCAN_COLS))

        def step(j, carry, cs=cs, a2_r=a2_r, a2_i=a2_i):
            s_r, s_i = carry
            rows = pl.ds(pl.multiple_of(j * rb, rb), rb)
            n_r = a2_r * s_r - a2_i * s_i + xr_s[rows, cs]
            n_i = a2_r * s_i + a2_i * s_r + xi_s[rows, cs]
            xr_s[rows, cs] = s_r
            xi_s[rows, cs] = s_i
            return n_r, n_i

        s_r, s_i = lax.fori_loop(0, pairs, step, (sr_ref[:, cs], si_ref[:, cs]), unroll=True)
        sr_ref[:, cs] = s_r
        si_ref[:, cs] = s_i

    ys = []
    for k in range(CH_BLOCKS):
        ks = slice(k * STATE_BLOCK, (k + 1) * STATE_BLOCK)
        y2 = (jnp.dot(xr_s[:, ks].astype(BF16), wor_ref[k], preferred_element_type=F32)
              + jnp.dot(xi_s[:, ks].astype(BF16), woi_ref[k], preferred_element_type=F32)
              + jnp.dot(lhs[k], wl_ref[k], preferred_element_type=F32))
        ys.append(interleave(y2[:, :LANES], y2[:, LANES:]) + d_ref[:, k * LANES:(k + 1) * LANES] * u_s[k])
    act = jax.nn.gelu(jnp.concatenate(ys, axis=1)).astype(BF16)
    z = jnp.dot(act, gw_ref[...], preferred_element_type=F32) + gb_ref[...]
    mix = z[:, :D_MODEL] * jax.nn.sigmoid(z[:, D_MODEL:])
    for k in range(CH_BLOCKS):
        u_s[k] = mix[:, k * LANES:(k + 1) * LANES]
    for s in range(rb):
        o_ref[s] = h_ref[s] + jnp.concatenate([u_s[k, seq_rows(s), :] for k in range(CH_BLOCKS)], axis=1)


def _s5_pairs(h, g, ar, ai, w_in, w_out_r, w_out_i, w_loc, d_skip, glu_w, glu_b, *, layer, tt):
    rb, t_len, _ = h.shape
    assert t_len % tt == 0 and tt % 2 == 0 and rb % 8 == 0
    rows, prow = rb * tt, rb * tt // 2
    block = (rb, tt, D_MODEL)
    state = jax.ShapeDtypeStruct((rb, STATE_W), F32)
    return pl.pallas_call(
        functools.partial(_s5_pair_kernel, rb=rb, tt=tt),
        grid=(t_len // tt,),
        in_specs=[pl.BlockSpec(block, lambda i: (0, i, 0)),
                  _const_spec((1, D_MODEL)),
                  _const_spec((1, STATE_W)), _const_spec((1, STATE_W)),
                  _const_spec((CH_BLOCKS, 2 * LANES, 2 * STATE_BLOCK)),
                  _const_spec((CH_BLOCKS, STATE_BLOCK, 2 * LANES)),
                  _const_spec((CH_BLOCKS, STATE_BLOCK, 2 * LANES)),
                  _const_spec((CH_BLOCKS, 2 * LANES, 2 * LANES)),
                  _const_spec((1, D_MODEL)),
                  _layer_spec((D_MODEL, 2 * D_MODEL), layer),
                  _const_spec((1, 2 * D_MODEL))],
        out_specs=(pl.BlockSpec(block, lambda i: (0, i, 0)),
                   pl.BlockSpec((rb, STATE_W), lambda i: (0, 0)),
                   pl.BlockSpec((rb, STATE_W), lambda i: (0, 0))),
        out_shape=(jax.ShapeDtypeStruct(h.shape, F32), state, state),
        scratch_shapes=[pltpu.VMEM((CH_BLOCKS, rows, LANES), F32),
                        pltpu.VMEM((prow, STATE_W), F32), pltpu.VMEM((prow, STATE_W), F32)],
        compiler_params=_params("arbitrary"),
        name="s5_pairs",
    )(h, g.reshape(1, D_MODEL), ar, ai, w_in, w_out_r, w_out_i, w_loc, d_skip.reshape(1, D_MODEL),
      glu_w, glu_b.reshape(1, 2 * D_MODEL))


def _proj_kernel(x_ref, g_ref, w_ref, o_ref, *, scale):
    xn = _rms(x_ref[...], g_ref[...]).astype(BF16)
    y = jnp.dot(xn, w_ref[...], preferred_element_type=F32)
    o_ref[...] = y if scale == 1.0 else y * scale


def _proj(x, g, w, layer=None, scale=1.0, tm=512):
    n = x.shape[0]
    tm = min(tm, n)
    nout = w.shape[-1]
    return pl.pallas_call(
        functools.partial(_proj_kernel, scale=scale),
        grid=(n // tm,),
        in_specs=[pl.BlockSpec((tm, D_MODEL), lambda i: (i, 0)),
                  _const_spec((1, D_MODEL)),
                  _layer_spec((D_MODEL, nout), layer)],
        out_specs=pl.BlockSpec((tm, nout), lambda i: (i, 0)),
        out_shape=jax.ShapeDtypeStruct((n, nout), F32),
        compiler_params=_params("parallel"),
        name="proj",
    )(x, g.reshape(1, D_MODEL), w)


def _kv_tails(t_len, tm):
    tails = []
    for w in DIL_WINDOWS:
        n_tail = min(w, t_len)
        if n_tail >= tm:
            assert n_tail % tm == 0
            tails.append(((t_len - n_tail) // tm, 0, tm))
        else:
            tails.append((t_len // tm - 1, tm - n_tail, n_tail))
    return tails


def _kv_kernel(x_ref, g_ref, w_ref, kv_ref, *tail_refs, tails):
    j = pl.program_id(1)
    xn = _rms(x_ref[...], g_ref[...]).astype(BF16)
    kv = jnp.dot(xn, w_ref[...], preferred_element_type=F32)
    kv_ref[...] = kv
    for g, (first_tile, row0, n_tok) in enumerate(tails):
        def emit(g=g, row0=row0, n_tok=n_tok):
            tail_refs[g][...] = kv[row0:row0 + n_tok, g * 2 * ATTN_WIDTH:(g + 1) * 2 * ATTN_WIDTH].T

        if first_tile == 0:
            emit()
        else:
            pl.when(j >= first_tile)(emit)


def _kv_proj(x, g, w, tm=512):
    bn, t_len, _ = x.shape
    tm = min(tm, t_len)
    assert t_len % tm == 0
    nout = w.shape[-1]
    tails = _kv_tails(t_len, tm)
    tail_specs, tail_shapes = [], []
    for (first_tile, _, n_tok), win in zip(tails, DIL_WINDOWS):
        tail_specs.append(pl.BlockSpec((None, 2 * ATTN_WIDTH, n_tok),
                                       lambda b, j, f=first_tile: (b, 0, jnp.maximum(j - f, 0))))
        tail_shapes.append(jax.ShapeDtypeStruct((bn, 2 * ATTN_WIDTH, min(win, t_len)), F32))
    return pl.pallas_call(
        functools.partial(_kv_kernel, tails=tails),
        grid=(bn, t_len // tm),
        in_specs=[pl.BlockSpec((None, tm, D_MODEL), lambda b, j: (b, j, 0)),
                  _const_spec((1, D_MODEL)),
                  _const_spec((D_MODEL, nout))],
        out_specs=[pl.BlockSpec((None, tm, nout), lambda b, j: (b, j, 0))] + tail_specs,
        out_shape=[jax.ShapeDtypeStruct((bn, t_len, nout), F32)] + tail_shapes,
        compiler_params=_params("parallel", "arbitrary"),
        name="kv_proj",
    )(x, g.reshape(1, D_MODEL), w)


def _oproj_kernel(a_ref, w_ref, h_ref, o_ref):
    o_ref[...] = h_ref[...] + jnp.dot(a_ref[...].astype(BF16), w_ref[...], preferred_element_type=F32)


def _oproj(a, w, layer, h, tm=512):
    n = h.shape[0]
    tm = min(tm, n)
    return pl.pallas_call(
        _oproj_kernel,
        grid=(n // tm,),
        in_specs=[pl.BlockSpec((tm, ATTN_WIDTH), lambda i: (i, 0)),
                  _layer_spec((ATTN_WIDTH, D_MODEL), layer),
                  pl.BlockSpec((tm, D_MODEL), lambda i: (i, 0))],
        out_specs=pl.BlockSpec((tm, D_MODEL), lambda i: (i, 0)),
        out_shape=jax.ShapeDtypeStruct((n, D_MODEL), F32),
        compiler_params=_params("parallel"),
        name="oproj",
    )(a, w, h)


def _alibi_slopes():
    n = N_DIL * HEADS_PER_GROUP
    s = jnp.exp2(-8.0 * jnp.arange(1, n + 1, dtype=F32) / n)
    return s.reshape(N_DIL, HEADS_PER_GROUP)


def _prompt_bias():
    qi = jnp.arange(R_LEN)[:, None]
    kj = jnp.arange(2 * R_LEN)[None, :]
    diff = R_LEN + qi - kj
    valid = (diff >= 0) & (diff <= R_LEN)
    dil = jnp.asarray(DIL_RATES, jnp.int32)[:, None, None]
    dist = (dil * diff[None]).astype(F32)
    bias = -_alibi_slopes()[:, :, None, None] * dist[:, None]
    return jnp.where(valid[None, None], bias, NEG)


def _sample_bias():
    kk = R_LEN - jnp.arange(R_LEN)
    dil = jnp.asarray(DIL_RATES, jnp.int32)[:, None]
    dist = (dil * kk[None]).astype(F32)
    return -_alibi_slopes()[:, :, None] * dist[:, None]


MERGE_ROWS = 256
INTERLEAVE = 16


def _attn_kernel(q0, q1, q2, k0, k1, k2, v0, v1, v2, bias_ref, o_ref, acc_s, m_s, l_s, *, t_len):
    qs, ks, vs = (q0, q1, q2), (k0, k1, k2), (v0, v1, v2)
    lo = lax.broadcasted_iota(jnp.int32, (R_LEN, LANES), 1) < HEAD_DIM

    def block(g, q_idx, k_idx, first):
        q = qs[g][q_idx, :]
        qb = jnp.concatenate([jnp.where(lo, q, 0.0), jnp.where(lo, 0.0, q)], axis=0).astype(BF16)
        kb = ks[g][k_idx, :].astype(BF16)
        vb = vs[g][k_idx, :].astype(BF16)
        s = lax.dot_general(qb, kb, (((1,), (1,)), ((), ())), preferred_element_type=F32)
        bias = bias_ref[g, :, :, R_LEN:] if first else bias_ref[g]
        s = s + bias.reshape(2 * R_LEN, bias.shape[-1])
        m = jnp.max(s, axis=-1, keepdims=True)
        p = jnp.exp(s - m)
        l = jnp.sum(p, axis=-1, keepdims=True)
        a = jnp.dot(p.astype(BF16), vb, preferred_element_type=F32)
        acc_s[g, q_idx, :] = jnp.where(lo, a[:R_LEN], a[R_LEN:])
        m_s[g, q_idx, :] = jnp.where(lo, m[:R_LEN], m[R_LEN:])
        l_s[g, q_idx, :] = jnp.where(lo, l[:R_LEN], l[R_LEN:])

    for g in range(N_DIL):
        dil = DIL_RATES[g]
        span = R_LEN * dil
        n_blocks = t_len // span

        def rows(start, n, dil=dil):
            if dil > 1:
                return pl.ds(start, n, stride=dil)
            return pl.ds(start if isinstance(start, int) else pl.multiple_of(start, R_LEN), n)

        if dil <= INTERLEAVE:
            for r in range(dil):
                block(g, rows(r, R_LEN), rows(r, R_LEN), True)
        else:
            def first_blocks(i, c, g=g, rows=rows):
                for j in range(INTERLEAVE):
                    r = i * INTERLEAVE + j
                    block(g, rows(r, R_LEN), rows(r, R_LEN), True)
                return c

            lax.fori_loop(0, dil // INTERLEAVE, first_blocks, 0)

        later = n_blocks - 1
        if later == 0:
            continue
        per_trip = max(w for w in range(1, max(INTERLEAVE // dil, 1) + 1) if later % w == 0)

        def later_blocks(i, c, g=g, dil=dil, span=span, rows=rows, per_trip=per_trip):
            for j in range(per_trip):
                nb = 1 + i * per_trip + j
                for r in range(dil):
                    start = nb * span + r
                    block(g, rows(start, R_LEN), rows(start - span, 2 * R_LEN), False)
            return c

        if later == per_trip:
            later_blocks(0, 0)
        else:
            lax.fori_loop(0, later // per_trip, later_blocks, 0)

    def merge(i, c):
        rows = pl.ds(pl.multiple_of(i * MERGE_ROWS, MERGE_ROWS), MERGE_ROWS)
        ms = [m_s[g, rows, :] for g in range(N_DIL)]
        mx = jnp.maximum(jnp.maximum(ms[0], ms[1]), ms[2])
        ws = [jnp.exp(m - mx) for m in ms]
        den = ws[0] * l_s[0, rows, :] + ws[1] * l_s[1, rows, :] + ws[2] * l_s[2, rows, :]
        num = ws[0] * acc_s[0, rows, :] + ws[1] * acc_s[1, rows, :] + ws[2] * acc_s[2, rows, :]
        o_ref[rows, :] = num / den
        return c

    lax.fori_loop(0, t_len // MERGE_ROWS, merge, 0)


def _attn_prompt(q, kv, bias):
    bn, t_len, _ = q.shape
    pairs = ATTN_WIDTH // LANES
    per_group = 2 * ATTN_WIDTH // LANES

    def col(off):
        return pl.BlockSpec((None, t_len, LANES), lambda b, hp: (b, 0, off + hp))

    in_specs = ([col(g * pairs) for g in range(N_DIL)]
                + [col(g * per_group) for g in range(N_DIL)]
                + [col(g * per_group + pairs) for g in range(N_DIL)]
                + [pl.BlockSpec((N_DIL, 2, R_LEN, 2 * R_LEN), lambda b, hp: (0, hp, 0, 0))])
    scratch = pltpu.VMEM((N_DIL, t_len, LANES), F32)
    return pl.pallas_call(
        functools.partial(_attn_kernel, t_len=t_len),
        grid=(bn, pairs),
        in_specs=in_specs,
        out_specs=pl.BlockSpec((None, t_len, LANES), lambda b, hp: (b, 0, hp)),
        out_shape=jax.ShapeDtypeStruct((bn, t_len, ATTN_WIDTH), F32),
        scratch_shapes=[scratch, scratch, scratch],
        compiler_params=_params("parallel", "parallel"),
        name="attn_prompt",
    )(q, q, q, kv, kv, kv, kv, kv, kv, bias)


SAMPLE_BLOCK = 8


def _window_kernel(c_ref, o_ref, *, dil):
    w = c_ref.shape[-1]
    pos = lax.broadcasted_iota(jnp.int32, (w, R_LEN), 0)
    col = lax.broadcasted_iota(jnp.int32, (w, R_LEN), 1)
    pick = jnp.where(pos == col * dil, 1.0, 0.0).astype(BF16)
    o_ref[...] = jnp.dot(c_ref[...].astype(BF16), pick, preferred_element_type=F32).astype(BF16)


def _strided_window(cache, dil):
    bd, w = cache.shape[:2]
    assert w == R_LEN * dil
    slab = jnp.transpose(cache, (0, 2, 3, 4, 1)).reshape(bd, 2 * ATTN_WIDTH, w)
    if dil == 1:
        return slab
    return pl.pallas_call(
        functools.partial(_window_kernel, dil=dil),
        grid=(bd,),
        in_specs=[pl.BlockSpec((None, 2 * ATTN_WIDTH, w), lambda i: (i, 0, 0))],
        out_specs=pl.BlockSpec((None, 2 * ATTN_WIDTH, R_LEN), lambda i: (i, 0, 0)),
        out_shape=jax.ShapeDtypeStruct((bd, 2 * ATTN_WIDTH, R_LEN), BF16),
        compiler_params=_params("parallel"),
        name="cache_window",
    )(slab)


def _attn_sample_kernel(q_ref, kvn_ref, c0, c1, c2, bias_ref, o_ref):
    windows = (c0, c1, c2)
    sub = lax.broadcasted_iota(jnp.int32, (HEADS_PER_GROUP, ATTN_WIDTH), 0)
    lane = lax.broadcasted_iota(jnp.int32, (HEADS_PER_GROUP, ATTN_WIDTH), 1)
    own = (lane // HEAD_DIM) == sub
    for b in range(SAMPLE_BLOCK):
        os_, ms, ls = [], [], []
        for g in range(N_DIL):
            base = g * 2 * ATTN_WIDTH
            qbd = jnp.where(own, q_ref[b:b + 1, g * ATTN_WIDTH:(g + 1) * ATTN_WIDTH], 0.0)
            kt = windows[g][b, :ATTN_WIDTH, :].astype(BF16)
            vt = windows[g][b, ATTN_WIDTH:, :].astype(BF16)
            kn = kvn_ref[b:b + 1, base:base + ATTN_WIDTH]
            vn = kvn_ref[b:b + 1, base + ATTN_WIDTH:base + 2 * ATTN_WIDTH]
            s = jnp.dot(qbd.astype(BF16), kt, preferred_element_type=F32) + bias_ref[g]
            s_new = jnp.sum(qbd * kn, axis=-1, keepdims=True)
            m = jnp.maximum(jnp.max(s, axis=-1, keepdims=True), s_new)
            p = jnp.exp(s - m)
            p_new = jnp.exp(s_new - m)
            ls.append(jnp.sum(p, axis=-1, keepdims=True) + p_new)
            os_.append(lax.dot_general(p.astype(BF16), vt, (((1,), (1,)), ((), ())),
                                       preferred_element_type=F32) + p_new * vn)
            ms.append(m)
        mx = jnp.maximum(jnp.maximum(ms[0], ms[1]), ms[2])
        ws = [jnp.exp(m - mx) for m in ms]
        den = ws[0] * ls[0] + ws[1] * ls[1] + ws[2] * ls[2]
        o = (ws[0] * os_[0] + ws[1] * os_[1] + ws[2] * os_[2]) / den
        o_ref[b:b + 1, :] = jnp.sum(jnp.where(own, o, 0.0), axis=0, keepdims=True)


def _attn_sample(q, kv_new, windows, bias):
    bd = q.shape[0]
    assert bd % SAMPLE_BLOCK == 0
    window_spec = pl.BlockSpec((SAMPLE_BLOCK, 2 * ATTN_WIDTH, R_LEN), lambda i: (i, 0, 0))
    return pl.pallas_call(
        _attn_sample_kernel,
        grid=(bd // SAMPLE_BLOCK,),
        in_specs=[pl.BlockSpec((SAMPLE_BLOCK, N_DIL * ATTN_WIDTH), lambda i: (i, 0)),
                  pl.BlockSpec((SAMPLE_BLOCK, N_DIL * 2 * ATTN_WIDTH), lambda i: (i, 0))]
                 + [window_spec] * N_DIL + [_const_spec((N_DIL, HEADS_PER_GROUP, R_LEN))],
        out_specs=pl.BlockSpec((SAMPLE_BLOCK, ATTN_WIDTH), lambda i: (i, 0)),
        out_shape=jax.ShapeDtypeStruct((bd, ATTN_WIDTH), F32),
        compiler_params=_params("parallel"),
        name="attn_sample",
    )(q, kv_new, *windows, bias)


def kernel(x_prompt, x_sample, state_ssm_re, state_ssm_im, cache_kv_w128, cache_kv_w512, cache_kv_w2048, norm_ffn1, ffn1_w_in, ffn1_w_out, norm_mix, norm_ffn2, ffn2_w_in, ffn2_w_out, norm_kv, norm_final, ssm_log_dt, ssm_lambda_re, ssm_lambda_im, ssm_b_re, ssm_b_im, ssm_c_re, ssm_c_im, ssm_d, glu_w, glu_b, attn_w_q, attn_w_kv, attn_w_o):
    bn, t_len, _ = x_prompt.shape
    bd = x_sample.shape[0]
    assert x_sample.shape[1] == 1

    glu_wb = glu_w.astype(BF16)
    w_q, w_kv, w_o = attn_w_q.astype(BF16), attn_w_kv.astype(BF16), attn_w_o.astype(BF16)
    q_scale = HEAD_DIM ** -0.5

    ssm, ssm_pairs = [], []
    for layer in range(N_A_LAYERS):
        ar, ai, wb, wcr, wci, *pair_w = _ssm_weights(
            ssm_log_dt[layer], ssm_lambda_re[layer], ssm_lambda_im[layer],
            ssm_b_re[layer], ssm_b_im[layer], ssm_c_re[layer], ssm_c_im[layer])
        ar, ai = ar.reshape(1, STATE_W), ai.reshape(1, STATE_W)
        tail = (ssm_d[layer].reshape(D_MODEL), glu_wb, glu_b[layer])
        ssm.append((ar, ai, wb, wcr, wci) + tail)
        ssm_pairs.append((ar, ai, *pair_w) + tail)

    def states(parts, n_seq):
        return jnp.stack([p.reshape(n_seq, N_SSM_GROUPS, SSM_STATE) for p in parts])

    s0r = state_ssm_re.reshape(N_A_LAYERS, bd, STATE_W)
    s0i = state_ssm_im.reshape(N_A_LAYERS, bd, STATE_W)
    windows = [_strided_window(cache, dil)
               for cache, dil in zip((cache_kv_w128, cache_kv_w512, cache_kv_w2048), DIL_RATES)]
    sample_bias, prompt_bias = _sample_bias(), _prompt_bias()
    fin_r, fin_i, fin_rs, fin_is = [], [], [], []
    h = x_prompt.reshape(bn * t_len, D_MODEL)
    hs = x_sample.reshape(bd, D_MODEL)
    hs, w1 = _ffn_cast(hs, norm_ffn1[0], ffn1_w_in, ffn1_w_out, 0)
    for layer in range(DEPTH):
        last = layer == DEPTH - 1
        if layer == N_A_LAYERS:
            kv_p, *kv_tails = _kv_proj(h.reshape(bn, t_len, D_MODEL), norm_kv, w_kv)
            kv_s = _proj(hs, norm_kv, w_kv)
        ffn1 = _ffn(h, norm_ffn1[layer], w1, cast_next=(ffn2_w_in, ffn2_w_out, layer),
                    q_proj=(norm_mix[layer], w_q, layer - N_A_LAYERS, q_scale) if layer >= N_A_LAYERS else None,
                    extra=hs if layer > 0 else None)
        if layer > 0:
            hs = ffn1[-1]
        h, w2 = ffn1[0], ffn1[2 if layer >= N_A_LAYERS else 1]
        if layer < N_A_LAYERS:
            h, sr, si = _s5_pairs(h.reshape(bn, t_len, D_MODEL), norm_mix[layer],
                                  *ssm_pairs[layer], layer=layer, tt=64)
            h = h.reshape(bn * t_len, D_MODEL)
            fin_r.append(sr)
            fin_i.append(si)
            hs, sr, si = _s5(hs.reshape(bd, 1, D_MODEL), norm_mix[layer], s0r[layer], s0i[layer],
                             *ssm[layer], layer=layer, tt=1)
            hs = hs.reshape(bd, D_MODEL)
            fin_rs.append(sr)
            fin_is.append(si)
            attn = None
        else:
            b = layer - N_A_LAYERS
            merged = _attn_prompt(ffn1[1].reshape(bn, t_len, -1), kv_p, prompt_bias)
            attn = (merged.reshape(bn * t_len, ATTN_WIDTH), w_o, b)
            qs = _proj(hs, norm_mix[layer], w_q, layer=b, scale=q_scale)
            hs = _oproj(_attn_sample(qs, kv_s, windows, sample_bias), w_o, b, hs)
        if last:
            h, hs = _ffn(h, norm_ffn2[layer], w2, attn=attn, final_g=norm_final, extra=hs)
        else:
            h, w1, hs = _ffn(h, norm_ffn2[layer], w2, attn=attn, extra=hs,
                             cast_next=(ffn1_w_in, ffn1_w_out, layer + 1))
    y_prompt = h.reshape(bn, t_len, D_MODEL)
    y_sample = hs.reshape(bd, 1, D_MODEL)
    ssm_re_p, ssm_im_p = states(fin_r, bn), states(fin_i, bn)
    ssm_re_s, ssm_im_s = states(fin_rs, bd), states(fin_is, bd)
    kv_s = kv_s.reshape(bd, 1, N_DIL, 2, HEADS_PER_GROUP, HEAD_DIM)
    kv_out_p = [jnp.transpose(tail.reshape(bn, 2, HEADS_PER_GROUP, HEAD_DIM, -1), (0, 4, 1, 2, 3))
                for tail in kv_tails]

    return (y_prompt, y_sample, ssm_re_p, ssm_im_p, kv_out_p[0], kv_out_p[1], kv_out_p[2],
            ssm_re_s, ssm_im_s, kv_s[:, :, 0], kv_s[:, :, 1], kv_s[:, :, 2])
```

```python
import functools

import jax
import jax.numpy as jnp
from jax import lax
from jax.experimental import pallas as pl
from jax.experimental.pallas import tpu as pltpu

F32 = jnp.float32
BF16 = jnp.bfloat16

D_MODEL = 1024
DEPTH = 4
N_A_LAYERS = DEPTH // 2
SSM_GROUP = 16
N_SSM_GROUPS = D_MODEL // SSM_GROUP
SSM_STATE = 64
STATE_W = N_SSM_GROUPS * SSM_STATE
DIL_WINDOWS = (128, 512, 2048)
DIL_RATES = (1, 4, 16)
N_DIL = 3
HEADS_PER_GROUP = 8
HEAD_DIM = 64
ATTN_WIDTH = HEADS_PER_GROUP * HEAD_DIM
R_LEN = DIL_WINDOWS[0] // DIL_RATES[0]
D_FF = 2816
EPS = 1e-6
NEG = -1e30

LANES = 128
MXU_TILE = 256
BF16_SUBLANES = 16
CH_BLOCKS = D_MODEL // LANES
GROUPS_PER_BLOCK = LANES // SSM_GROUP
STATE_BLOCK = GROUPS_PER_BLOCK * SSM_STATE
VMEM_LIMIT_BYTES = 56 * 1024 * 1024

assert all(w // d == R_LEN for w, d in zip(DIL_WINDOWS, DIL_RATES))


def _params(*semantics):
    return pltpu.CompilerParams(dimension_semantics=semantics, vmem_limit_bytes=VMEM_LIMIT_BYTES)


def _const_spec(shape):
    zeros = (0,) * len(shape)
    return pl.BlockSpec(shape, lambda *_: zeros, pipeline_mode=pl.Buffered(1))


def _layer_spec(shape, layer):
    if layer is None:
        return _const_spec(shape)
    idx = (layer,) + (0,) * len(shape)
    return pl.BlockSpec((None,) + tuple(shape), lambda *_: idx, pipeline_mode=pl.Buffered(1))


def _rms(x, g):
    return x * lax.rsqrt(jnp.mean(x * x, axis=-1, keepdims=True) + EPS) * g


FFN_SUB_ROWS = 512


def _ff_chunks(n_chunks):
    tiles = D_FF // MXU_TILE
    assert tiles * MXU_TILE == D_FF and n_chunks <= tiles
    cuts = [MXU_TILE * ((tiles * c + n_chunks - 1) // n_chunks) for c in range(n_chunks + 1)]
    return list(zip(cuts[:-1], cuts[1:]))


def _swiglu_residual(x, g, wa_ref, wb_ref, wout_ref, n_chunks):
    xn = _rms(x, g).astype(BF16)
    acc = None
    for lo, hi in _ff_chunks(n_chunks):
        a = jnp.dot(xn, wa_ref[:, lo:hi], preferred_element_type=F32)
        b = jnp.dot(xn, wb_ref[:, lo:hi], preferred_element_type=F32)
        act = (a * jax.nn.sigmoid(a) * b).astype(BF16)
        d = jnp.dot(act, wout_ref[lo:hi, :], preferred_element_type=F32)
        acc = d if acc is None else acc + d
    return x + 0.5 * acc


def _ffn_kernel(*refs, n_chunks, attn_in, final, q_scale, cast_next, extra_steps):
    refs = list(refs)
    x_ref = refs.pop(0)
    if attn_in:
        a_ref, wo_ref = refs.pop(0), refs.pop(0)
    g_ref, wa_ref, wb_ref, wout_ref = (refs.pop(0) for _ in range(4))
    if final:
        gf_ref = refs.pop(0)
    if q_scale is not None:
        gq_ref, wq_ref = refs.pop(0), refs.pop(0)
    if cast_next:
        next_f32 = [refs.pop(0) for _ in range(3)]
    if extra_steps:
        xs_ref = refs.pop(0)
    o_ref = refs.pop(0)
    if q_scale is not None:
        q_ref = refs.pop(0)
    if cast_next:
        for src in next_f32:
            refs.pop(0)[...] = src[...].astype(BF16)

    def main():
        tm = x_ref.shape[0]
        for r0 in range(0, tm, min(tm, FFN_SUB_ROWS)):
            rows = slice(r0, r0 + min(tm, FFN_SUB_ROWS))
            x = x_ref[rows, :]
            if attn_in:
                x = x + jnp.dot(a_ref[rows, :].astype(BF16), wo_ref[...], preferred_element_type=F32)
            y = _swiglu_residual(x, g_ref[...], wa_ref, wb_ref, wout_ref, n_chunks)
            if final:
                y = _rms(y, gf_ref[...])
            o_ref[rows, :] = y
            if q_scale is not None:
                q = jnp.dot(_rms(y, gq_ref[...]).astype(BF16), wq_ref[...], preferred_element_type=F32)
                q_ref[rows, :] = q * q_scale

    if not extra_steps:
        main()
        return
    ys_ref = refs.pop(0)
    pl.when(pl.program_id(0) < extra_steps)(main)

    @pl.when(pl.program_id(0) == extra_steps)
    def _():
        ys = _swiglu_residual(xs_ref[...], g_ref[...], wa_ref, wb_ref, wout_ref, n_chunks)
        ys_ref[...] = _rms(ys, gf_ref[...]) if final else ys


def _ffn(x, g, weights, attn=None, final_g=None, q_proj=None, cast_next=None, extra=None, n_chunks=2):
    n = x.shape[0]
    tm = min(FFN_SUB_ROWS * (1 if q_proj is not None else 2), n)
    assert n % tm == 0
    steps = n // tm
    step = (lambda i: jnp.minimum(i, steps - 1)) if extra is not None else (lambda i: i)
    rows = lambda width: pl.BlockSpec((tm, width), lambda i: (step(i), 0))
    in_specs, args = [rows(D_MODEL)], [x]
    if attn is not None:
        merged, w_o, idx = attn
        in_specs += [rows(ATTN_WIDTH), _layer_spec((ATTN_WIDTH, D_MODEL), idx)]
        args += [merged, w_o]
    in_specs += [_const_spec((1, D_MODEL)), _const_spec((D_MODEL, D_FF)), _const_spec((D_MODEL, D_FF)),
                 _const_spec((D_FF, D_MODEL))]
    args += [g.reshape(1, D_MODEL), *weights]
    if final_g is not None:
        in_specs.append(_const_spec((1, D_MODEL)))
        args.append(final_g.reshape(1, D_MODEL))
    out_specs, out_shape = [rows(D_MODEL)], [jax.ShapeDtypeStruct((n, D_MODEL), F32)]
    q_scale = None
    if q_proj is not None:
        gq, w_q, idx, q_scale = q_proj
        nq = w_q.shape[-1]
        in_specs += [_const_spec((1, D_MODEL)), _layer_spec((D_MODEL, nq), idx)]
        args += [gq.reshape(1, D_MODEL), w_q]
        out_specs.append(rows(nq))
        out_shape.append(jax.ShapeDtypeStruct((n, nq), F32))
    if cast_next is not None:
        w_in, w_out, idx = cast_next
        r_in = D_MODEL // steps
        out_blocks = [b for b in range(1, steps + 1) if steps % b == 0 and D_FF % b == 0
                      and (D_FF // b) % BF16_SUBLANES == 0][-1]
        r_out, hold = D_FF // out_blocks, steps // out_blocks
        assert r_in * steps == D_MODEL and r_in % BF16_SUBLANES == 0
        in_specs += [pl.BlockSpec((None, r_in, D_FF), lambda i: (idx, step(i), 0)),
                     pl.BlockSpec((None, r_in, D_FF), lambda i: (idx, step(i), 1)),
                     pl.BlockSpec((None, r_out, D_MODEL), lambda i: (idx, step(i) // hold, 0))]
        args += [w_in, w_in, w_out]
        out_specs += [pl.BlockSpec((r_in, D_FF), lambda i: (step(i), 0)),
                      pl.BlockSpec((r_in, D_FF), lambda i: (step(i), 0)),
                      pl.BlockSpec((r_out, D_MODEL), lambda i: (step(i) // hold, 0))]
        out_shape += [jax.ShapeDtypeStruct((D_MODEL, D_FF), BF16), jax.ShapeDtypeStruct((D_MODEL, D_FF), BF16),
                      jax.ShapeDtypeStruct((D_FF, D_MODEL), BF16)]
    if extra is not None:
        in_specs.append(_const_spec(extra.shape))
        args.append(extra)
        out_specs.append(pl.BlockSpec(extra.shape, lambda i: (0, 0)))
        out_shape.append(jax.ShapeDtypeStruct(extra.shape, F32))
    out = pl.pallas_call(
        functools.partial(_ffn_kernel, n_chunks=n_chunks, attn_in=attn is not None,
                          final=final_g is not None, q_scale=q_scale, cast_next=cast_next is not None,
                          extra_steps=steps if extra is not None else 0),
        grid=(steps + (extra is not None),),
        in_specs=in_specs,
        out_specs=out_specs,
        out_shape=out_shape,
        compiler_params=_params("arbitrary"),
        name="ffn",
    )(*args)
    if len(out) == 1:
        return out[0]
    parts = [out[0]]
    k = 1
    if q_proj is not None:
        parts.append(out[k])
        k += 1
    if cast_next is not None:
        parts.append(tuple(out[k:k + 3]))
        k += 3
    if extra is not None:
        parts.append(out[k])
    return tuple(parts)


def _ffn_cast_kernel(x_ref, g_ref, wa_ref, wb_ref, wout_ref, *rest, final):
    if final:
        gf_ref, o_ref, wa_o, wb_o, wout_o, acc_s = rest
    else:
        o_ref, wa_o, wb_o, wout_o, acc_s = rest
    c = pl.program_id(0)
    wa = wa_ref[...].astype(BF16)
    wb = wb_ref[...].astype(BF16)
    wout = wout_ref[...].astype(BF16)
    wa_o[...] = wa
    wb_o[...] = wb
    wout_o[...] = wout
    x = x_ref[...]
    xn = _rms(x, g_ref[...]).astype(BF16)
    a = jnp.dot(xn, wa, preferred_element_type=F32)
    b = jnp.dot(xn, wb, preferred_element_type=F32)
    d = jnp.dot((a * jax.nn.sigmoid(a) * b).astype(BF16), wout, preferred_element_type=F32)

    @pl.when(c == 0)
    def _():
        acc_s[...] = d

    @pl.when(c > 0)
    def _():
        acc_s[...] += d

    @pl.when(c == pl.num_programs(0) - 1)
    def _():
        y = x + 0.5 * acc_s[...]
        o_ref[...] = _rms(y, gf_ref[...]) if final else y


def _ffn_cast(x, g, w_in, w_out, layer, final_g=None):
    n = x.shape[0]
    n_chunks = D_FF // MXU_TILE
    final = final_g is not None
    in_specs = [_const_spec((n, D_MODEL)), _const_spec((1, D_MODEL)),
                pl.BlockSpec((None, D_MODEL, MXU_TILE), lambda c: (layer, 0, c)),
                pl.BlockSpec((None, D_MODEL, MXU_TILE), lambda c: (layer, 0, n_chunks + c)),
                pl.BlockSpec((None, MXU_TILE, D_MODEL), lambda c: (layer, c, 0))]
    args = [x, g.reshape(1, D_MODEL), w_in, w_in, w_out]
    if final:
        in_specs.append(_const_spec((1, D_MODEL)))
        args.append(final_g.reshape(1, D_MODEL))
    y, wa, wb, wo = pl.pallas_call(
        functools.partial(_ffn_cast_kernel, final=final),
        grid=(n_chunks,),
        in_specs=in_specs,
        out_specs=[pl.BlockSpec((n, D_MODEL), lambda c: (0, 0)),
                   pl.BlockSpec((D_MODEL, MXU_TILE), lambda c: (0, c)),
                   pl.BlockSpec((D_MODEL, MXU_TILE), lambda c: (0, c)),
                   pl.BlockSpec((MXU_TILE, D_MODEL), lambda c: (c, 0))],
        out_shape=[jax.ShapeDtypeStruct((n, D_MODEL), F32),
                   jax.ShapeDtypeStruct((D_MODEL, D_FF), BF16),
                   jax.ShapeDtypeStruct((D_MODEL, D_FF), BF16),
                   jax.ShapeDtypeStruct((D_FF, D_MODEL), BF16)],
        scratch_shapes=[pltpu.VMEM((n, D_MODEL), F32)],
        compiler_params=_params("arbitrary"),
        name="ffn_cast",
    )(*args)
    return y, (wa, wb, wo)


def _ssm_weights_kernel(ldt_ref, lr_ref, li_ref, br_ref, bi_ref, cr_ref, ci_ref,
                        ar_ref, ai_ref, wb_ref, wcr_ref, wci_ref, win_ref, wor_ref, woi_ref, wl_ref):
    dt = jnp.exp(ldt_ref[...])
    lr = lr_ref[...]
    li = li_ref[...]
    mag = jnp.exp(lr * dt)
    ar = mag * jnp.cos(li * dt)
    ai = mag * jnp.sin(li * dt)
    den = lr * lr + li * li
    zr = ((ar - 1.0) * lr + ai * li) / den
    zi = (ai * lr - (ar - 1.0) * li) / den
    ar_ref[...] = ar
    ai_ref[...] = ai
    br = br_ref[...]
    bi = bi_ref[...]
    bbr = zr[:, None, :] * br - zi[:, None, :] * bi
    bbi = zr[:, None, :] * bi + zi[:, None, :] * br
    a1r, a1i = ar[:, None, :], ai[:, None, :]
    a2r, a2i = (ar * ar - ai * ai)[:, None, :], (2.0 * ar * ai)[:, None, :]
    abr = a1r * bbr - a1i * bbi
    abi = a1r * bbi + a1i * bbr
    cr = cr_ref[...]
    ci = ci_ref[...]
    car = cr * a1r - ci * a1i
    cai = cr * a1i + ci * a1r
    ca2r = cr * a2r - ci * a2i
    ca2i = cr * a2i + ci * a2r

    def re_dot(xr, xi):
        dims = (((2,), (2,)), ((0,), (0,)))
        return (lax.dot_general(xr, bbr, dims, preferred_element_type=F32)
                - lax.dot_general(xi, bbi, dims, preferred_element_type=F32))

    k0 = re_dot(cr, ci)
    k1 = re_dot(car, cai)

    def same_group(n_rows, row_span, n_cols, col_span):
        r = lax.broadcasted_iota(jnp.int32, (n_rows, n_cols), 0) // row_span
        c = lax.broadcasted_iota(jnp.int32, (n_rows, n_cols), 1) // col_span
        return r == c

    def replicate(n, copies, transpose):
        shape = (copies * n, n) if transpose else (n, copies * n)
        r = lax.broadcasted_iota(jnp.int32, shape, 0)
        c = lax.broadcasted_iota(jnp.int32, shape, 1)
        return jnp.where(r % n == c % n, 1.0, 0.0).astype(BF16)

    gpb = GROUPS_PER_BLOCK
    rep_p = replicate(SSM_STATE, gpb, False)
    rep_pt = replicate(SSM_STATE, gpb, True)
    rep_ht = replicate(SSM_GROUP, gpb, True)
    m_in = same_group(LANES, SSM_GROUP, STATE_BLOCK, SSM_STATE)
    m_out = same_group(STATE_BLOCK, SSM_STATE, LANES, SSM_GROUP)
    m_mix = same_group(LANES, SSM_GROUP, LANES, SSM_GROUP)
    nt = (((1,), (1,)), ((), ()))

    def rows_of(x, k):
        return x[k * gpb:(k + 1) * gpb].reshape(LANES, x.shape[-1]).astype(BF16)

    def bd_in(x, k):
        return jnp.where(m_in, jnp.dot(rows_of(x, k), rep_p, preferred_element_type=F32), 0.0)

    def bd_out(x, k):
        return jnp.where(m_out, lax.dot_general(rep_pt, rows_of(x, k), nt, preferred_element_type=F32), 0.0)

    def bd_mix(x, k):
        return jnp.where(m_mix, lax.dot_general(rep_ht, rows_of(x, k), nt, preferred_element_type=F32), 0.0)

    for k in range(CH_BLOCKS):
        one_step = jnp.concatenate([bd_in(bbr, k), bd_in(bbi, k)], axis=1)
        wb_ref[k] = one_step.astype(BF16)
        wcr_ref[k] = bd_out(cr, k).astype(BF16)
        wci_ref[k] = (-bd_out(ci, k)).astype(BF16)
        win_ref[k] = jnp.concatenate([jnp.concatenate([bd_in(abr, k), bd_in(abi, k)], axis=1), one_step],
                                     axis=0).astype(BF16)
        wor_ref[k] = jnp.concatenate([bd_out(car, k), bd_out(ca2r, k)], axis=1).astype(BF16)
        woi_ref[k] = (-jnp.concatenate([bd_out(cai, k), bd_out(ca2i, k)], axis=1)).astype(BF16)
        m0, m1 = bd_mix(k0, k), bd_mix(k1, k)
        wl_ref[k] = jnp.concatenate([jnp.concatenate([m0, m1], axis=1),
                                     jnp.concatenate([jnp.zeros_like(m0), m0], axis=1)], axis=0).astype(BF16)


def _ssm_weights(log_dt, lam_re, lam_im, b_re, b_im, c_re, c_im):
    g, p = N_SSM_GROUPS, SSM_STATE
    gp = jax.ShapeDtypeStruct((g, p), F32)
    stack = lambda rows, cols: jax.ShapeDtypeStruct((CH_BLOCKS, rows, cols), BF16)
    return pl.pallas_call(
        _ssm_weights_kernel,
        out_shape=(gp, gp,
                   stack(LANES, 2 * STATE_BLOCK), stack(STATE_BLOCK, LANES), stack(STATE_BLOCK, LANES),
                   stack(2 * LANES, 2 * STATE_BLOCK), stack(STATE_BLOCK, 2 * LANES), stack(STATE_BLOCK, 2 * LANES),
                   stack(2 * LANES, 2 * LANES)),
        compiler_params=pltpu.CompilerParams(vmem_limit_bytes=VMEM_LIMIT_BYTES),
        name="s5_weights",
    )(log_dt.reshape(g, 1), lam_re, lam_im, jnp.swapaxes(b_re, 1, 2), jnp.swapaxes(b_im, 1, 2), c_re, c_im)


SCAN_COLS = 512


def _s5_kernel(h_ref, g_ref, s0r_ref, s0i_ref, ar_ref, ai_ref, wb_ref, wcr_ref, wci_ref, d_ref,
               gw_ref, gb_ref, o_ref, sr_ref, si_ref, u_s, xr_s, xi_s, *, rb, tt):
    n_slab, per_slab = h_ref.shape[:2]
    assert n_slab * per_slab == rb * tt and (n_slab == rb or tt == 1)

    def seq_rows(s):
        return pl.ds(s, per_slab, stride=n_slab) if n_slab > 1 else pl.ds(0, per_slab)

    @pl.when(pl.program_id(0) == 0)
    def _():
        sr_ref[...] = s0r_ref[...]
        si_ref[...] = s0i_ref[...]

    for s in range(n_slab):
        un = _rms(h_ref[s], g_ref[...])
        for k in range(CH_BLOCKS):
            u_s[k, seq_rows(s), :] = un[:, k * LANES:(k + 1) * LANES]
    for k in range(CH_BLOCKS):
        bu = jnp.dot(u_s[k].astype(BF16), wb_ref[k], preferred_element_type=F32)
        xr_s[:, k * STATE_BLOCK:(k + 1) * STATE_BLOCK] = bu[:, :STATE_BLOCK]
        xi_s[:, k * STATE_BLOCK:(k + 1) * STATE_BLOCK] = bu[:, STATE_BLOCK:]

    for c in range(STATE_W // SCAN_COLS):
        cs = slice(c * SCAN_COLS, (c + 1) * SCAN_COLS)
        a_r = jnp.broadcast_to(ar_ref[:, cs], (rb, SCAN_COLS))
        a_i = jnp.broadcast_to(ai_ref[:, cs], (rb, SCAN_COLS))

        def step(t, carry, cs=cs, a_r=a_r, a_i=a_i):
            s_r, s_i = carry
            rows = pl.ds(pl.multiple_of(t * rb, rb), rb)
            n_r = a_r * s_r - a_i * s_i + xr_s[rows, cs]
            n_i = a_r * s_i + a_i * s_r + xi_s[rows, cs]
            xr_s[rows, cs] = n_r
            xi_s[rows, cs] = n_i
            return n_r, n_i

        s_r, s_i = lax.fori_loop(0, tt, step, (sr_ref[:, cs], si_ref[:, cs]), unroll=True)
        sr_ref[:, cs] = s_r
        si_ref[:, cs] = s_i

    ys = []
    for k in range(CH_BLOCKS):
        ks = slice(k * STATE_BLOCK, (k + 1) * STATE_BLOCK)
        ys.append(jnp.dot(xr_s[:, ks].astype(BF16), wcr_ref[k], preferred_element_type=F32)
                  + jnp.dot(xi_s[:, ks].astype(BF16), wci_ref[k], preferred_element_type=F32)
                  + d_ref[:, k * LANES:(k + 1) * LANES] * u_s[k])
    act = jax.nn.gelu(jnp.concatenate(ys, axis=1)).astype(BF16)
    z = jnp.dot(act, gw_ref[...], preferred_element_type=F32) + gb_ref[...]
    mix = z[:, :D_MODEL] * jax.nn.sigmoid(z[:, D_MODEL:])
    for k in range(CH_BLOCKS):
        u_s[k] = mix[:, k * LANES:(k + 1) * LANES]
    for s in range(n_slab):
        o_ref[s] = h_ref[s] + jnp.concatenate([u_s[k, seq_rows(s), :] for k in range(CH_BLOCKS)], axis=1)


def _s5(h, g, s0r, s0i, ar, ai, wb, wcr, wci, d_skip, glu_w, glu_b, *, layer, tt):
    rb, t_len, _ = h.shape
    assert t_len % tt == 0 and rb % 8 == 0
    rows = rb * tt
    block = (1, rb, D_MODEL) if t_len == 1 else (rb, tt, D_MODEL)
    h_in = h.reshape(1, rb, D_MODEL) if t_len == 1 else h
    state = jax.ShapeDtypeStruct((rb, STATE_W), F32)
    out, sr, si = pl.pallas_call(
        functools.partial(_s5_kernel, rb=rb, tt=tt),
        grid=(t_len // tt,),
        in_specs=[pl.BlockSpec(block, lambda i: (0, i, 0)),
                  _const_spec((1, D_MODEL)),
                  _const_spec((rb, STATE_W)), _const_spec((rb, STATE_W)),
                  _const_spec((1, STATE_W)), _const_spec((1, STATE_W)),
                  _const_spec((CH_BLOCKS, LANES, 2 * STATE_BLOCK)),
                  _const_spec((CH_BLOCKS, STATE_BLOCK, LANES)),
                  _const_spec((CH_BLOCKS, STATE_BLOCK, LANES)),
                  _const_spec((1, D_MODEL)),
                  _layer_spec((D_MODEL, 2 * D_MODEL), layer),
                  _const_spec((1, 2 * D_MODEL))],
        out_specs=(pl.BlockSpec(block, lambda i: (0, i, 0)),
                   pl.BlockSpec((rb, STATE_W), lambda i: (0, 0)),
                   pl.BlockSpec((rb, STATE_W), lambda i: (0, 0))),
        out_shape=(jax.ShapeDtypeStruct(h_in.shape, F32), state, state),
        scratch_shapes=[pltpu.VMEM((CH_BLOCKS, rows, LANES), F32),
                        pltpu.VMEM((rows, STATE_W), F32), pltpu.VMEM((rows, STATE_W), F32)],
        compiler_params=_params("arbitrary"),
        name="s5",
    )(h_in, g.reshape(1, D_MODEL), s0r, s0i, ar, ai, wb, wcr, wci, d_skip.reshape(1, D_MODEL),
      glu_w, glu_b.reshape(1, 2 * D_MODEL))
    return out.reshape(h.shape), sr, si


def _s5_pair_kernel(h_ref, g_ref, ar_ref, ai_ref, win_ref, wor_ref, woi_ref, wl_ref, d_ref,
                    gw_ref, gb_ref, o_ref, sr_ref, si_ref, u_s, xr_s, xi_s, *, rb, tt):
    pairs = tt // 2
    prow = pairs * rb

    def seq_rows(s):
        return pl.ds(s, tt, stride=rb)

    def split(x):
        x3 = x.reshape(pairs, 2 * rb, x.shape[-1])
        return x3[:, :rb].reshape(prow, x.shape[-1]), x3[:, rb:].reshape(prow, x.shape[-1])

    def interleave(even, odd):
        n = even.shape[-1]
        return jnp.concatenate([even.reshape(pairs, rb, n), odd.reshape(pairs, rb, n)], axis=1).reshape(tt * rb, n)

    @pl.when(pl.program_id(0) == 0)
    def _():
        sr_ref[...] = jnp.zeros_like(sr_ref)
        si_ref[...] = jnp.zeros_like(si_ref)

    for s in range(rb):
        un = _rms(h_ref[s], g_ref[...])
        for k in range(CH_BLOCKS):
            u_s[k, seq_rows(s), :] = un[:, k * LANES:(k + 1) * LANES]

    lhs = []
    for k in range(CH_BLOCKS):
        ks = slice(k * STATE_BLOCK, (k + 1) * STATE_BLOCK)
        lhs.append(jnp.concatenate(split(u_s[k]), axis=1).astype(BF16))
        w = jnp.dot(lhs[k], win_ref[k], preferred_element_type=F32)
        xr_s[:, ks] = w[:, :STATE_BLOCK]
        xi_s[:, ks] = w[:, STATE_BLOCK:]

    for c in range(STATE_W // SCAN_COLS):
        cs = slice(c * SCAN_COLS, (c + 1) * SCAN_COLS)
        a_r, a_i = ar_ref[:, cs], ai_ref[:, cs]
        a2_r = jnp.broadcast_to(a_r * a_r - a_i * a_i, (rb, SCAN_COLS))
        a2_i = jnp.broadcast_to(2.0 * a_r * a_i, (rb, SCAN_COLS))

        def step(j, carry, cs=cs, a2_r=a2_r, a2_i=a2_i):
            s_r, s_i = carry
            rows = pl.ds(pl.multiple_of(j * rb, rb), rb)
            n_r = a2_r * s_r - a2_i * s_i + xr_s[rows, cs]
            n_i = a2_r * s_i + a2_i * s_r + xi_s[rows, cs]
            xr_s[rows, cs] = s_r
            xi_s[rows, cs] = s_i
            return n_r, n_i

        s_r, s_i = lax.fori_loop(0, pairs, step, (sr_ref[:, cs], si_ref[:, cs]), unroll=True)
        sr_ref[:, cs] = s_r
        si_ref[:, cs] = s_i

    ys = []
    for k in range(CH_BLOCKS):
        ks = slice(k * STATE_BLOCK, (k + 1) * STATE_BLOCK)
        y2 = (jnp.dot(xr_s[:, ks].astype(BF16), wor_ref[k], preferred_element_type=F32)
              + jnp.dot(xi_s[:, ks].astype(BF16), woi_ref[k], preferred_element_type=F32)
              + jnp.dot(lhs[k], wl_ref[k], preferred_element_type=F32))
        ys.append(interleave(y2[:, :LANES], y2[:, LANES:]) + d_ref[:, k * LANES:(k + 1) * LANES] * u_s[k])
    act = jax.nn.gelu(jnp.concatenate(ys, axis=1)).astype(BF16)
    z = jnp.dot(act, gw_ref[...], preferred_element_type=F32) + gb_ref[...]
    mix = z[:, :D_MODEL] * jax.nn.sigmoid(z[:, D_MODEL:])
    for k in range(CH_BLOCKS):
        u_s[k] = mix[:, k * LANES:(k + 1) * LANES]
    for s in range(rb):
        o_ref[s] = h_ref[s] + jnp.concatenate([u_s[k, seq_rows(s), :] for k in range(CH_BLOCKS)], axis=1)


def _s5_pairs(h, g, ar, ai, w_in, w_out_r, w_out_i, w_loc, d_skip, glu_w, glu_b, *, layer, tt):
    rb, t_len, _ = h.shape
    assert t_len % tt == 0 and tt % 2 == 0 and rb % 8 == 0
    rows, prow = rb * tt, rb * tt // 2
    block = (rb, tt, D_MODEL)
    state = jax.ShapeDtypeStruct((rb, STATE_W), F32)
    return pl.pallas_call(
        functools.partial(_s5_pair_kernel, rb=rb, tt=tt),
        grid=(t_len // tt,),
        in_specs=[pl.BlockSpec(block, lambda i: (0, i, 0)),
                  _const_spec((1, D_MODEL)),
                  _const_spec((1, STATE_W)), _const_spec((1, STATE_W)),
                  _const_spec((CH_BLOCKS, 2 * LANES, 2 * STATE_BLOCK)),
                  _const_spec((CH_BLOCKS, STATE_BLOCK, 2 * LANES)),
                  _const_spec((CH_BLOCKS, STATE_BLOCK, 2 * LANES)),
                  _const_spec((CH_BLOCKS, 2 * LANES, 2 * LANES)),
                  _const_spec((1, D_MODEL)),
                  _layer_spec((D_MODEL, 2 * D_MODEL), layer),
                  _const_spec((1, 2 * D_MODEL))],
        out_specs=(pl.BlockSpec(block, lambda i: (0, i, 0)),
                   pl.BlockSpec((rb, STATE_W), lambda i: (0, 0)),
                   pl.BlockSpec((rb, STATE_W), lambda i: (0, 0))),
        out_shape=(jax.ShapeDtypeStruct(h.shape, F32), state, state),
        scratch_shapes=[pltpu.VMEM((CH_BLOCKS, rows, LANES), F32),
                        pltpu.VMEM((prow, STATE_W), F32), pltpu.VMEM((prow, STATE_W), F32)],
        compiler_params=_params("arbitrary"),
        name="s5_pairs",
    )(h, g.reshape(1, D_MODEL), ar, ai, w_in, w_out_r, w_out_i, w_loc, d_skip.reshape(1, D_MODEL),
      glu_w, glu_b.reshape(1, 2 * D_MODEL))


def _proj_kernel(x_ref, g_ref, w_ref, o_ref, *, scale):
    xn = _rms(x_ref[...], g_ref[...]).astype(BF16)
    y = jnp.dot(xn, w_ref[...], preferred_element_type=F32)
    o_ref[...] = y if scale == 1.0 else y * scale


def _proj(x, g, w, layer=None, scale=1.0, tm=512):
    n = x.shape[0]
    tm = min(tm, n)
    nout = w.shape[-1]
    return pl.pallas_call(
        functools.partial(_proj_kernel, scale=scale),
        grid=(n // tm,),
        in_specs=[pl.BlockSpec((tm, D_MODEL), lambda i: (i, 0)),
                  _const_spec((1, D_MODEL)),
                  _layer_spec((D_MODEL, nout), layer)],
        out_specs=pl.BlockSpec((tm, nout), lambda i: (i, 0)),
        out_shape=jax.ShapeDtypeStruct((n, nout), F32),
        compiler_params=_params("parallel"),
        name="proj",
    )(x, g.reshape(1, D_MODEL), w)


def _kv_tails(t_len, tm):
    tails = []
    for w in DIL_WINDOWS:
        n_tail = min(w, t_len)
        if n_tail >= tm:
            assert n_tail % tm == 0
            tails.append(((t_len - n_tail) // tm, 0, tm))
        else:
            tails.append((t_len // tm - 1, tm - n_tail, n_tail))
    return tails


def _kv_kernel(x_ref, g_ref, w_ref, kv_ref, *tail_refs, tails):
    j = pl.program_id(1)
    xn = _rms(x_ref[...], g_ref[...]).astype(BF16)
    kv = jnp.dot(xn, w_ref[...], preferred_element_type=F32)
    kv_ref[...] = kv
    for g, (first_tile, row0, n_tok) in enumerate(tails):
        def emit(g=g, row0=row0, n_tok=n_tok):
            tail_refs[g][...] = kv[row0:row0 + n_tok, g * 2 * ATTN_WIDTH:(g + 1) * 2 * ATTN_WIDTH].T

        if first_tile == 0:
            emit()
        else:
            pl.when(j >= first_tile)(emit)


def _kv_proj(x, g, w, tm=512):
    bn, t_len, _ = x.shape
    tm = min(tm, t_len)
    assert t_len % tm == 0
    nout = w.shape[-1]
    tails = _kv_tails(t_len, tm)
    tail_specs, tail_shapes = [], []
    for (first_tile, _, n_tok), win in zip(tails, DIL_WINDOWS):
        tail_specs.append(pl.BlockSpec((None, 2 * ATTN_WIDTH, n_tok),
                                       lambda b, j, f=first_tile: (b, 0, jnp.maximum(j - f, 0))))
        tail_shapes.append(jax.ShapeDtypeStruct((bn, 2 * ATTN_WIDTH, min(win, t_len)), F32))
    return pl.pallas_call(
        functools.partial(_kv_kernel, tails=tails),
        grid=(bn, t_len // tm),
        in_specs=[pl.BlockSpec((None, tm, D_MODEL), lambda b, j: (b, j, 0)),
                  _const_spec((1, D_MODEL)),
                  _const_spec((D_MODEL, nout))],
        out_specs=[pl.BlockSpec((None, tm, nout), lambda b, j: (b, j, 0))] + tail_specs,
        out_shape=[jax.ShapeDtypeStruct((bn, t_len, nout), F32)] + tail_shapes,
        compiler_params=_params("parallel", "arbitrary"),
        name="kv_proj",
    )(x, g.reshape(1, D_MODEL), w)


def _oproj_kernel(a_ref, w_ref, h_ref, o_ref):
    o_ref[...] = h_ref[...] + jnp.dot(a_ref[...].astype(BF16), w_ref[...], preferred_element_type=F32)


def _oproj(a, w, layer, h, tm=512):
    n = h.shape[0]
    tm = min(tm, n)
    return pl.pallas_call(
        _oproj_kernel,
        grid=(n // tm,),
        in_specs=[pl.BlockSpec((tm, ATTN_WIDTH), lambda i: (i, 0)),
                  _layer_spec((ATTN_WIDTH, D_MODEL), layer),
                  pl.BlockSpec((tm, D_MODEL), lambda i: (i, 0))],
        out_specs=pl.BlockSpec((tm, D_MODEL), lambda i: (i, 0)),
        out_shape=jax.ShapeDtypeStruct((n, D_MODEL), F32),
        compiler_params=_params("parallel"),
        name="oproj",
    )(a, w, h)


def _alibi_slopes():
    n = N_DIL * HEADS_PER_GROUP
    s = jnp.exp2(-8.0 * jnp.arange(1, n + 1, dtype=F32) / n)
    return s.reshape(N_DIL, HEADS_PER_GROUP)


def _prompt_bias():
    qi = jnp.arange(R_LEN)[:, None]
    kj = jnp.arange(2 * R_LEN)[None, :]
    diff = R_LEN + qi - kj
    valid = (diff >= 0) & (diff <= R_LEN)
    dil = jnp.asarray(DIL_RATES, jnp.int32)[:, None, None]
    dist = (dil * diff[None]).astype(F32)
    bias = -_alibi_slopes()[:, :, None, None] * dist[:, None]
    return jnp.where(valid[None, None], bias, NEG)


def _sample_bias():
    kk = R_LEN - jnp.arange(R_LEN)
    dil = jnp.asarray(DIL_RATES, jnp.int32)[:, None]
    dist = (dil * kk[None]).astype(F32)
    return -_alibi_slopes()[:, :, None] * dist[:, None]


MERGE_ROWS = 256
INTERLEAVE = 16


def _attn_kernel(q0, q1, q2, k0, k1, k2, v0, v1, v2, bias_ref, o_ref, acc_s, m_s, l_s, *, t_len):
    qs, ks, vs = (q0, q1, q2), (k0, k1, k2), (v0, v1, v2)
    lo = lax.broadcasted_iota(jnp.int32, (R_LEN, LANES), 1) < HEAD_DIM

    def block(g, q_idx, k_idx, first):
        q = qs[g][q_idx, :]
        qb = jnp.concatenate([jnp.where(lo, q, 0.0), jnp.where(lo, 0.0, q)], axis=0).astype(BF16)
        kb = ks[g][k_idx, :].astype(BF16)
        vb = vs[g][k_idx, :].astype(BF16)
        s = lax.dot_general(qb, kb, (((1,), (1,)), ((), ())), preferred_element_type=F32)
        bias = bias_ref[g, :, :, R_LEN:] if first else bias_ref[g]
        s = s + bias.reshape(2 * R_LEN, bias.shape[-1])
        m = jnp.max(s, axis=-1, keepdims=True)
        p = jnp.exp(s - m)
        l = jnp.sum(p, axis=-1, keepdims=True)
        a = jnp.dot(p.astype(BF16), vb, preferred_element_type=F32)
        acc_s[g, q_idx, :] = jnp.where(lo, a[:R_LEN], a[R_LEN:])
        m_s[g, q_idx, :] = jnp.where(lo, m[:R_LEN], m[R_LEN:])
        l_s[g, q_idx, :] = jnp.where(lo, l[:R_LEN], l[R_LEN:])

    for g in range(N_DIL):
        dil = DIL_RATES[g]
        span = R_LEN * dil
        n_blocks = t_len // span

        def rows(start, n, dil=dil):
            if dil > 1:
                return pl.ds(start, n, stride=dil)
            return pl.ds(start if isinstance(start, int) else pl.multiple_of(start, R_LEN), n)

        if dil <= INTERLEAVE:
            for r in range(dil):
                block(g, rows(r, R_LEN), rows(r, R_LEN), True)
        else:
            def first_blocks(i, c, g=g, rows=rows):
                for j in range(INTERLEAVE):
                    r = i * INTERLEAVE + j
                    block(g, rows(r, R_LEN), rows(r, R_LEN), True)
                return c

            lax.fori_loop(0, dil // INTERLEAVE, first_blocks, 0)

        later = n_blocks - 1
        if later == 0:
            continue
        per_trip = max(w for w in range(1, max(INTERLEAVE // dil, 1) + 1) if later % w == 0)

        def later_blocks(i, c, g=g, dil=dil, span=span, rows=rows, per_trip=per_trip):
            for j in range(per_trip):
                nb = 1 + i * per_trip + j
                for r in range(dil):
                    start = nb * span + r
                    block(g, rows(start, R_LEN), rows(start - span, 2 * R_LEN), False)
            return c

        if later == per_trip:
            later_blocks(0, 0)
        else:
            lax.fori_loop(0, later // per_trip, later_blocks, 0)

    def merge(i, c):
        rows = pl.ds(pl.multiple_of(i * MERGE_ROWS, MERGE_ROWS), MERGE_ROWS)
        ms = [m_s[g, rows, :] for g in range(N_DIL)]
        mx = jnp.maximum(jnp.maximum(ms[0], ms[1]), ms[2])
        ws = [jnp.exp(m - mx) for m in ms]
        den = ws[0] * l_s[0, rows, :] + ws[1] * l_s[1, rows, :] + ws[2] * l_s[2, rows, :]
        num = ws[0] * acc_s[0, rows, :] + ws[1] * acc_s[1, rows, :] + ws[2] * acc_s[2, rows, :]
        o_ref[rows, :] = num / den
        return c

    lax.fori_loop(0, t_len // MERGE_ROWS, merge, 0)


def _attn_prompt(q, kv, bias):
    bn, t_len, _ = q.shape
    pairs = ATTN_WIDTH // LANES
    per_group = 2 * ATTN_WIDTH // LANES

    def col(off):
        return pl.BlockSpec((None, t_len, LANES), lambda b, hp: (b, 0, off + hp))

    in_specs = ([col(g * pairs) for g in range(N_DIL)]
                + [col(g * per_group) for g in range(N_DIL)]
                + [col(g * per_group + pairs) for g in range(N_DIL)]
                + [pl.BlockSpec((N_DIL, 2, R_LEN, 2 * R_LEN), lambda b, hp: (0, hp, 0, 0))])
    scratch = pltpu.VMEM((N_DIL, t_len, LANES), F32)
    return pl.pallas_call(
        functools.partial(_attn_kernel, t_len=t_len),
        grid=(bn, pairs),
        in_specs=in_specs,
        out_specs=pl.BlockSpec((None, t_len, LANES), lambda b, hp: (b, 0, hp)),
        out_shape=jax.ShapeDtypeStruct((bn, t_len, ATTN_WIDTH), F32),
        scratch_shapes=[scratch, scratch, scratch],
        compiler_params=_params("parallel", "parallel"),
        name="attn_prompt",
    )(q, q, q, kv, kv, kv, kv, kv, kv, bias)


SAMPLE_BLOCK = 8


def _window_kernel(c_ref, o_ref, *, dil):
    w = c_ref.shape[-1]
    pos = lax.broadcasted_iota(jnp.int32, (w, R_LEN), 0)
    col = lax.broadcasted_iota(jnp.int32, (w, R_LEN), 1)
    pick = jnp.where(pos == col * dil, 1.0, 0.0).astype(BF16)
    o_ref[...] = jnp.dot(c_ref[...].astype(BF16), pick, preferred_element_type=F32).astype(BF16)


def _strided_window(cache, dil):
    bd, w = cache.shape[:2]
    assert w == R_LEN * dil
    slab = jnp.transpose(cache, (0, 2, 3, 4, 1)).reshape(bd, 2 * ATTN_WIDTH, w)
    if dil == 1:
        return slab
    return pl.pallas_call(
        functools.partial(_window_kernel, dil=dil),
        grid=(bd,),
        in_specs=[pl.BlockSpec((None, 2 * ATTN_WIDTH, w), lambda i: (i, 0, 0))],
        out_specs=pl.BlockSpec((None, 2 * ATTN_WIDTH, R_LEN), lambda i: (i, 0, 0)),
        out_shape=jax.ShapeDtypeStruct((bd, 2 * ATTN_WIDTH, R_LEN), BF16),
        compiler_params=_params("parallel"),
        name="cache_window",
    )(slab)


def _attn_sample_kernel(q_ref, kvn_ref, c0, c1, c2, bias_ref, o_ref):
    windows = (c0, c1, c2)
    sub = lax.broadcasted_iota(jnp.int32, (HEADS_PER_GROUP, ATTN_WIDTH), 0)
    lane = lax.broadcasted_iota(jnp.int32, (HEADS_PER_GROUP, ATTN_WIDTH), 1)
    own = (lane // HEAD_DIM) == sub
    for b in range(SAMPLE_BLOCK):
        os_, ms, ls = [], [], []
        for g in range(N_DIL):
            base = g * 2 * ATTN_WIDTH
            qbd = jnp.where(own, q_ref[b:b + 1, g * ATTN_WIDTH:(g + 1) * ATTN_WIDTH], 0.0)
            kt = windows[g][b, :ATTN_WIDTH, :].astype(BF16)
            vt = windows[g][b, ATTN_WIDTH:, :].astype(BF16)
            kn = kvn_ref[b:b + 1, base:base + ATTN_WIDTH]
            vn = kvn_ref[b:b + 1, base + ATTN_WIDTH:base + 2 * ATTN_WIDTH]
            s = jnp.dot(qbd.astype(BF16), kt, preferred_element_type=F32) + bias_ref[g]
            s_new = jnp.sum(qbd * kn, axis=-1, keepdims=True)
            m = jnp.maximum(jnp.max(s, axis=-1, keepdims=True), s_new)
            p = jnp.exp(s - m)
            p_new = jnp.exp(s_new - m)
            ls.append(jnp.sum(p, axis=-1, keepdims=True) + p_new)
            os_.append(lax.dot_general(p.astype(BF16), vt, (((1,), (1,)), ((), ())),
                                       preferred_element_type=F32) + p_new * vn)
            ms.append(m)
        mx = jnp.maximum(jnp.maximum(ms[0], ms[1]), ms[2])
        ws = [jnp.exp(m - mx) for m in ms]
        den = ws[0] * ls[0] + ws[1] * ls[1] + ws[2] * ls[2]
        o = (ws[0] * os_[0] + ws[1] * os_[1] + ws[2] * os_[2]) / den
        o_ref[b:b + 1, :] = jnp.sum(jnp.where(own, o, 0.0), axis=0, keepdims=True)


def _attn_sample(q, kv_new, windows, bias):
    bd = q.shape[0]
    assert bd % SAMPLE_BLOCK == 0
    window_spec = pl.BlockSpec((SAMPLE_BLOCK, 2 * ATTN_WIDTH, R_LEN), lambda i: (i, 0, 0))
    return pl.pallas_call(
        _attn_sample_kernel,
        grid=(bd // SAMPLE_BLOCK,),
        in_specs=[pl.BlockSpec((SAMPLE_BLOCK, N_DIL * ATTN_WIDTH), lambda i: (i, 0)),
                  pl.BlockSpec((SAMPLE_BLOCK, N_DIL * 2 * ATTN_WIDTH), lambda i: (i, 0))]
                 + [window_spec] * N_DIL + [_const_spec((N_DIL, HEADS_PER_GROUP, R_LEN))],
        out_specs=pl.BlockSpec((SAMPLE_BLOCK, ATTN_WIDTH), lambda i: (i, 0)),
        out_shape=jax.ShapeDtypeStruct((bd, ATTN_WIDTH), F32),
        compiler_params=_params("parallel"),
        name="attn_sample",
    )(q, kv_new, *windows, bias)


def kernel(x_prompt, x_sample, state_ssm_re, state_ssm_im, cache_kv_w128, cache_kv_w512, cache_kv_w2048, norm_ffn1, ffn1_w_in, ffn1_w_out, norm_mix, norm_ffn2, ffn2_w_in, ffn2_w_out, norm_kv, norm_final, ssm_log_dt, ssm_lambda_re, ssm_lambda_im, ssm_b_re, ssm_b_im, ssm_c_re, ssm_c_im, ssm_d, glu_w, glu_b, attn_w_q, attn_w_kv, attn_w_o):
    bn, t_len, _ = x_prompt.shape
    bd = x_sample.shape[0]
    assert x_sample.shape[1] == 1

    glu_wb = glu_w.astype(BF16)
    w_q, w_kv, w_o = attn_w_q.astype(BF16), attn_w_kv.astype(BF16), attn_w_o.astype(BF16)
    q_scale = HEAD_DIM ** -0.5

    ssm, ssm_pairs = [], []
    for layer in range(N_A_LAYERS):
        ar, ai, wb, wcr, wci, *pair_w = _ssm_weights(
            ssm_log_dt[layer], ssm_lambda_re[layer], ssm_lambda_im[layer],
            ssm_b_re[layer], ssm_b_im[layer], ssm_c_re[layer], ssm_c_im[layer])
        ar, ai = ar.reshape(1, STATE_W), ai.reshape(1, STATE_W)
        tail = (ssm_d[layer].reshape(D_MODEL), glu_wb, glu_b[layer])
        ssm.append((ar, ai, wb, wcr, wci) + tail)
        ssm_pairs.append((ar, ai, *pair_w) + tail)

    def states(parts, n_seq):
        return jnp.stack([p.reshape(n_seq, N_SSM_GROUPS, SSM_STATE) for p in parts])

    s0r = state_ssm_re.reshape(N_A_LAYERS, bd, STATE_W)
    s0i = state_ssm_im.reshape(N_A_LAYERS, bd, STATE_W)
    windows = [_strided_window(cache, dil)
               for cache, dil in zip((cache_kv_w128, cache_kv_w512, cache_kv_w2048), DIL_RATES)]
    sample_bias, prompt_bias = _sample_bias(), _prompt_bias()
    fin_r, fin_i, fin_rs, fin_is = [], [], [], []
    h = x_prompt.reshape(bn * t_len, D_MODEL)
    hs = x_sample.reshape(bd, D_MODEL)
    hs, w1 = _ffn_cast(hs, norm_ffn1[0], ffn1_w_in, ffn1_w_out, 0)
    for layer in range(DEPTH):
        last = layer == DEPTH - 1
        if layer == N_A_LAYERS:
            kv_p, *kv_tails = _kv_proj(h.reshape(bn, t_len, D_MODEL), norm_kv, w_kv)
            kv_s = _proj(hs, norm_kv, w_kv)
        ffn1 = _ffn(h, norm_ffn1[layer], w1, cast_next=(ffn2_w_in, ffn2_w_out, layer),
                    q_proj=(norm_mix[layer], w_q, layer - N_A_LAYERS, q_scale) if layer >= N_A_LAYERS else None,
                    extra=hs if layer > 0 else None)
        if layer > 0:
            hs = ffn1[-1]
        h, w2 = ffn1[0], ffn1[2 if layer >= N_A_LAYERS else 1]
        if layer < N_A_LAYERS:
            h, sr, si = _s5_pairs(h.reshape(bn, t_len, D_MODEL), norm_mix[layer],
                                  *ssm_pairs[layer], layer=layer, tt=64)
            h = h.reshape(bn * t_len, D_MODEL)
            fin_r.append(sr)
            fin_i.append(si)
            hs, sr, si = _s5(hs.reshape(bd, 1, D_MODEL), norm_mix[layer], s0r[layer], s0i[layer],
                             *ssm[layer], layer=layer, tt=1)
            hs = hs.reshape(bd, D_MODEL)
            fin_rs.append(sr)
            fin_is.append(si)
            attn = None
        else:
            b = layer - N_A_LAYERS
            merged = _attn_prompt(ffn1[1].reshape(bn, t_len, -1), kv_p, prompt_bias)
            attn = (merged.reshape(bn * t_len, ATTN_WIDTH), w_o, b)
            qs = _proj(hs, norm_mix[layer], w_q, layer=b, scale=q_scale)
            hs = _oproj(_attn_sample(qs, kv_s, windows, sample_bias), w_o, b, hs)
        if last:
            h, hs = _ffn(h, norm_ffn2[layer], w2, attn=attn, final_g=norm_final, extra=hs)
        else:
            h, w1, hs = _ffn(h, norm_ffn2[layer], w2, attn=attn, extra=hs,
                             cast_next=(ffn1_w_in, ffn1_w_out, layer + 1))
    y_prompt = h.reshape(bn, t_len, D_MODEL)
    y_sample = hs.reshape(bd, 1, D_MODEL)
    ssm_re_p, ssm_im_p = states(fin_r, bn), states(fin_i, bn)
    ssm_re_s, ssm_im_s = states(fin_rs, bd), states(fin_is, bd)
    kv_s = kv_s.reshape(bd, 1, N_DIL, 2, HEADS_PER_GROUP, HEAD_DIM)
    kv_out_p = [jnp.transpose(tail.reshape(bn, 2, HEADS_PER_GROUP, HEAD_DIM, -1), (0, 4, 1, 2, 3))
                for tail in kv_tails]

    return (y_prompt, y_sample, ssm_re_p, ssm_im_p, kv_out_p[0], kv_out_p[1], kv_out_p[2],
            ssm_re_s, ssm_im_s, kv_s[:, :, 0], kv_s[:, :, 1], kv_s[:, :, 2])
```

```python
import functools

import jax
import jax.numpy as jnp
from jax import lax
from jax.experimental import pallas as pl
from jax.experimental.pallas import tpu as pltpu

F32 = jnp.float32
BF16 = jnp.bfloat16

D_MODEL = 1024
DEPTH = 4
N_A_LAYERS = DEPTH // 2
SSM_GROUP = 16
N_SSM_GROUPS = D_MODEL // SSM_GROUP
SSM_STATE = 64
STATE_W = N_SSM_GROUPS * SSM_STATE
DIL_WINDOWS = (128, 512, 2048)
DIL_RATES = (1, 4, 16)
N_DIL = 3
HEADS_PER_GROUP = 8
HEAD_DIM = 64
ATTN_WIDTH = HEADS_PER_GROUP * HEAD_DIM
R_LEN = DIL_WINDOWS[0] // DIL_RATES[0]
D_FF = 2816
EPS = 1e-6
NEG = -1e30
LOG2_E = 1.4426950408889634

LANES = 128
MXU_TILE = 256
BF16_SUBLANES = 16
CH_BLOCKS = D_MODEL // LANES
GROUPS_PER_BLOCK = LANES // SSM_GROUP
STATE_BLOCK = GROUPS_PER_BLOCK * SSM_STATE
VMEM_LIMIT_BYTES = 56 * 1024 * 1024

assert all(w // d == R_LEN for w, d in zip(DIL_WINDOWS, DIL_RATES))


def _params(*semantics):
    return pltpu.CompilerParams(dimension_semantics=semantics, vmem_limit_bytes=VMEM_LIMIT_BYTES)


def _const_spec(shape):
    zeros = (0,) * len(shape)
    return pl.BlockSpec(shape, lambda *_: zeros, pipeline_mode=pl.Buffered(1))


def _layer_spec(shape, layer):
    if layer is None:
        return _const_spec(shape)
    idx = (layer,) + (0,) * len(shape)
    return pl.BlockSpec((None,) + tuple(shape), lambda *_: idx, pipeline_mode=pl.Buffered(1))


def _rms(x, g):
    return x * lax.rsqrt(jnp.mean(x * x, axis=-1, keepdims=True) + EPS) * g


FFN_SUB_ROWS = 512


def _ff_chunks(n_chunks):
    tiles = D_FF // MXU_TILE
    assert tiles * MXU_TILE == D_FF and n_chunks <= tiles
    cuts = [MXU_TILE * ((tiles * c + n_chunks - 1) // n_chunks) for c in range(n_chunks + 1)]
    return list(zip(cuts[:-1], cuts[1:]))


def _swiglu_residual(x, g, wa_ref, wb_ref, wout_ref, n_chunks):
    xn = _rms(x, g).astype(BF16)
    acc = None
    for lo, hi in _ff_chunks(n_chunks):
        a = jnp.dot(xn, wa_ref[:, lo:hi], preferred_element_type=F32)
        b = jnp.dot(xn, wb_ref[:, lo:hi], preferred_element_type=F32)
        act = (a * jax.nn.sigmoid(a) * b).astype(BF16)
        d = jnp.dot(act, wout_ref[lo:hi, :], preferred_element_type=F32)
        acc = d if acc is None else acc + d
    return x + 0.5 * acc


def _ffn_kernel(*refs, n_chunks, attn_in, final, q_scale, cast_next, extra_steps):
    refs = list(refs)
    x_ref = refs.pop(0)
    if attn_in:
        a_ref, wo_ref = refs.pop(0), refs.pop(0)
    g_ref, wa_ref, wb_ref, wout_ref = (refs.pop(0) for _ in range(4))
    if final:
        gf_ref = refs.pop(0)
    if q_scale is not None:
        gq_ref, wq_ref = refs.pop(0), refs.pop(0)
    if cast_next:
        next_f32 = [refs.pop(0) for _ in range(3)]
    if extra_steps:
        xs_ref = refs.pop(0)
    o_ref = refs.pop(0)
    if q_scale is not None:
        q_ref = refs.pop(0)
    if cast_next:
        for src in next_f32:
            refs.pop(0)[...] = src[...].astype(BF16)

    def main():
        tm = x_ref.shape[0]
        for r0 in range(0, tm, min(tm, FFN_SUB_ROWS)):
            rows = slice(r0, r0 + min(tm, FFN_SUB_ROWS))
            x = x_ref[rows, :]
            if attn_in:
                x = x + jnp.dot(a_ref[rows, :].astype(BF16), wo_ref[...], preferred_element_type=F32)
            y = _swiglu_residual(x, g_ref[...], wa_ref, wb_ref, wout_ref, n_chunks)
            if final:
                y = _rms(y, gf_ref[...])
            o_ref[rows, :] = y
            if q_scale is not None:
                q = jnp.dot(_rms(y, gq_ref[...]).astype(BF16), wq_ref[...], preferred_element_type=F32)
                q_ref[rows, :] = q * q_scale

    if not extra_steps:
        main()
        return
    ys_ref = refs.pop(0)
    pl.when(pl.program_id(0) < extra_steps)(main)

    @pl.when(pl.program_id(0) == extra_steps)
    def _():
        ys = _swiglu_residual(xs_ref[...], g_ref[...], wa_ref, wb_ref, wout_ref, n_chunks)
        ys_ref[...] = _rms(ys, gf_ref[...]) if final else ys


def _ffn(x, g, weights, attn=None, final_g=None, q_proj=None, cast_next=None, extra=None, n_chunks=2):
    n = x.shape[0]
    tm = min(FFN_SUB_ROWS * (1 if q_proj is not None else 2), n)
    assert n % tm == 0
    steps = n // tm
    step = (lambda i: jnp.minimum(i, steps - 1)) if extra is not None else (lambda i: i)
    rows = lambda width: pl.BlockSpec((tm, width), lambda i: (step(i), 0))
    in_specs, args = [rows(D_MODEL)], [x]
    if attn is not None:
        merged, w_o, idx = attn
        in_specs += [rows(ATTN_WIDTH), _layer_spec((ATTN_WIDTH, D_MODEL), idx)]
        args += [merged, w_o]
    in_specs += [_const_spec((1, D_MODEL)), _const_spec((D_MODEL, D_FF)), _const_spec((D_MODEL, D_FF)),
                 _const_spec((D_FF, D_MODEL))]
    args += [g.reshape(1, D_MODEL), *weights]
    if final_g is not None:
        in_specs.append(_const_spec((1, D_MODEL)))
        args.append(final_g.reshape(1, D_MODEL))
    out_specs, out_shape = [rows(D_MODEL)], [jax.ShapeDtypeStruct((n, D_MODEL), F32)]
    q_scale = None
    if q_proj is not None:
        gq, w_q, idx, q_scale = q_proj
        nq = w_q.shape[-1]
        in_specs += [_const_spec((1, D_MODEL)), _layer_spec((D_MODEL, nq), idx)]
        args += [gq.reshape(1, D_MODEL), w_q]
        out_specs.append(rows(nq))
        out_shape.append(jax.ShapeDtypeStruct((n, nq), F32))
    if cast_next is not None:
        w_in, w_out, idx = cast_next
        r_in = D_MODEL // steps
        out_blocks = [b for b in range(1, steps + 1) if steps % b == 0 and D_FF % b == 0
                      and (D_FF // b) % BF16_SUBLANES == 0][-1]
        r_out, hold = D_FF // out_blocks, steps // out_blocks
        assert r_in * steps == D_MODEL and r_in % BF16_SUBLANES == 0
        in_specs += [pl.BlockSpec((None, r_in, D_FF), lambda i: (idx, step(i), 0)),
                     pl.BlockSpec((None, r_in, D_FF), lambda i: (idx, step(i), 1)),
                     pl.BlockSpec((None, r_out, D_MODEL), lambda i: (idx, step(i) // hold, 0))]
        args += [w_in, w_in, w_out]
        out_specs += [pl.BlockSpec((r_in, D_FF), lambda i: (step(i), 0)),
                      pl.BlockSpec((r_in, D_FF), lambda i: (step(i), 0)),
                      pl.BlockSpec((r_out, D_MODEL), lambda i: (step(i) // hold, 0))]
        out_shape += [jax.ShapeDtypeStruct((D_MODEL, D_FF), BF16), jax.ShapeDtypeStruct((D_MODEL, D_FF), BF16),
                      jax.ShapeDtypeStruct((D_FF, D_MODEL), BF16)]
    if extra is not None:
        in_specs.append(_const_spec(extra.shape))
        args.append(extra)
        out_specs.append(pl.BlockSpec(extra.shape, lambda i: (0, 0)))
        out_shape.append(jax.ShapeDtypeStruct(extra.shape, F32))
    out = pl.pallas_call(
        functools.partial(_ffn_kernel, n_chunks=n_chunks, attn_in=attn is not None,
                          final=final_g is not None, q_scale=q_scale, cast_next=cast_next is not None,
                          extra_steps=steps if extra is not None else 0),
        grid=(steps + (extra is not None),),
        in_specs=in_specs,
        out_specs=out_specs,
        out_shape=out_shape,
        compiler_params=_params("arbitrary"),
        name="ffn",
    )(*args)
    if len(out) == 1:
        return out[0]
    parts = [out[0]]
    k = 1
    if q_proj is not None:
        parts.append(out[k])
        k += 1
    if cast_next is not None:
        parts.append(tuple(out[k:k + 3]))
        k += 3
    if extra is not None:
        parts.append(out[k])
    return tuple(parts)


def _ffn_cast_kernel(x_ref, g_ref, wa_ref, wb_ref, wout_ref, *rest, final):
    if final:
        gf_ref, o_ref, wa_o, wb_o, wout_o, acc_s = rest
    else:
        o_ref, wa_o, wb_o, wout_o, acc_s = rest
    c = pl.program_id(0)
    wa = wa_ref[...].astype(BF16)
    wb = wb_ref[...].astype(BF16)
    wout = wout_ref[...].astype(BF16)
    wa_o[...] = wa
    wb_o[...] = wb
    wout_o[...] = wout
    x = x_ref[...]
    xn = _rms(x, g_ref[...]).astype(BF16)
    a = jnp.dot(xn, wa, preferred_element_type=F32)
    b = jnp.dot(xn, wb, preferred_element_type=F32)
    d = jnp.dot((a * jax.nn.sigmoid(a) * b).astype(BF16), wout, preferred_element_type=F32)

    @pl.when(c == 0)
    def _():
        acc_s[...] = d

    @pl.when(c > 0)
    def _():
        acc_s[...] += d

    @pl.when(c == pl.num_programs(0) - 1)
    def _():
        y = x + 0.5 * acc_s[...]
        o_ref[...] = _rms(y, gf_ref[...]) if final else y


def _ffn_cast(x, g, w_in, w_out, layer, final_g=None):
    n = x.shape[0]
    n_chunks = D_FF // MXU_TILE
    final = final_g is not None
    in_specs = [_const_spec((n, D_MODEL)), _const_spec((1, D_MODEL)),
                pl.BlockSpec((None, D_MODEL, MXU_TILE), lambda c: (layer, 0, c)),
                pl.BlockSpec((None, D_MODEL, MXU_TILE), lambda c: (layer, 0, n_chunks + c)),
                pl.BlockSpec((None, MXU_TILE, D_MODEL), lambda c: (layer, c, 0))]
    args = [x, g.reshape(1, D_MODEL), w_in, w_in, w_out]
    if final:
        in_specs.append(_const_spec((1, D_MODEL)))
        args.append(final_g.reshape(1, D_MODEL))
    y, wa, wb, wo = pl.pallas_call(
        functools.partial(_ffn_cast_kernel, final=final),
        grid=(n_chunks,),
        in_specs=in_specs,
        out_specs=[pl.BlockSpec((n, D_MODEL), lambda c: (0, 0)),
                   pl.BlockSpec((D_MODEL, MXU_TILE), lambda c: (0, c)),
                   pl.BlockSpec((D_MODEL, MXU_TILE), lambda c: (0, c)),
                   pl.BlockSpec((MXU_TILE, D_MODEL), lambda c: (c, 0))],
        out_shape=[jax.ShapeDtypeStruct((n, D_MODEL), F32),
                   jax.ShapeDtypeStruct((D_MODEL, D_FF), BF16),
                   jax.ShapeDtypeStruct((D_MODEL, D_FF), BF16),
                   jax.ShapeDtypeStruct((D_FF, D_MODEL), BF16)],
        scratch_shapes=[pltpu.VMEM((n, D_MODEL), F32)],
        compiler_params=_params("arbitrary"),
        name="ffn_cast",
    )(*args)
    return y, (wa, wb, wo)


def _ssm_weights_kernel(ldt_ref, lr_ref, li_ref, br_ref, bi_ref, cr_ref, ci_ref,
                        ar_ref, ai_ref, wb_ref, wcr_ref, wci_ref, win_ref, wor_ref, woi_ref, wl_ref):
    dt = jnp.exp(ldt_ref[...])
    lr = lr_ref[...]
    li = li_ref[...]
    mag = jnp.exp(lr * dt)
    ar = mag * jnp.cos(li * dt)
    ai = mag * jnp.sin(li * dt)
    den = lr * lr + li * li
    zr = ((ar - 1.0) * lr + ai * li) / den
    zi = (ai * lr - (ar - 1.0) * li) / den
    ar_ref[...] = ar
    ai_ref[...] = ai
    br = br_ref[...]
    bi = bi_ref[...]
    bbr = zr[:, None, :] * br - zi[:, None, :] * bi
    bbi = zr[:, None, :] * bi + zi[:, None, :] * br
    a1r, a1i = ar[:, None, :], ai[:, None, :]
    a2r, a2i = (ar * ar - ai * ai)[:, None, :], (2.0 * ar * ai)[:, None, :]
    abr = a1r * bbr - a1i * bbi
    abi = a1r * bbi + a1i * bbr
    cr = cr_ref[...]
    ci = ci_ref[...]
    car = cr * a1r - ci * a1i
    cai = cr * a1i + ci * a1r
    ca2r = cr * a2r - ci * a2i
    ca2i = cr * a2i + ci * a2r

    def re_dot(xr, xi):
        dims = (((2,), (2,)), ((0,), (0,)))
        return (lax.dot_general(xr, bbr, dims, preferred_element_type=F32)
                - lax.dot_general(xi, bbi, dims, preferred_element_type=F32))

    k0 = re_dot(cr, ci)
    k1 = re_dot(car, cai)

    def same_group(n_rows, row_span, n_cols, col_span):
        r = lax.broadcasted_iota(jnp.int32, (n_rows, n_cols), 0) // row_span
        c = lax.broadcasted_iota(jnp.int32, (n_rows, n_cols), 1) // col_span
        return r == c

    def replicate(n, copies, transpose):
        shape = (copies * n, n) if transpose else (n, copies * n)
        r = lax.broadcasted_iota(jnp.int32, shape, 0)
        c = lax.broadcasted_iota(jnp.int32, shape, 1)
        return jnp.where(r % n == c % n, 1.0, 0.0).astype(BF16)

    gpb = GROUPS_PER_BLOCK
    rep_p = replicate(SSM_STATE, gpb, False)
    rep_pt = replicate(SSM_STATE, gpb, True)
    rep_ht = replicate(SSM_GROUP, gpb, True)
    m_in = same_group(LANES, SSM_GROUP, STATE_BLOCK, SSM_STATE)
    m_out = same_group(STATE_BLOCK, SSM_STATE, LANES, SSM_GROUP)
    m_mix = same_group(LANES, SSM_GROUP, LANES, SSM_GROUP)
    nt = (((1,), (1,)), ((), ()))

    def rows_of(x, k):
        return x[k * gpb:(k + 1) * gpb].reshape(LANES, x.shape[-1]).astype(BF16)

    def bd_in(x, k):
        return jnp.where(m_in, jnp.dot(rows_of(x, k), rep_p, preferred_element_type=F32), 0.0)

    def bd_out(x, k):
        return jnp.where(m_out, lax.dot_general(rep_pt, rows_of(x, k), nt, preferred_element_type=F32), 0.0)

    def bd_mix(x, k):
        return jnp.where(m_mix, lax.dot_general(rep_ht, rows_of(x, k), nt, preferred_element_type=F32), 0.0)

    for k in range(CH_BLOCKS):
        one_step = jnp.concatenate([bd_in(bbr, k), bd_in(bbi, k)], axis=1)
        wb_ref[k] = one_step.astype(BF16)
        wcr_ref[k] = bd_out(cr, k).astype(BF16)
        wci_ref[k] = (-bd_out(ci, k)).astype(BF16)
        win_ref[k] = jnp.concatenate([jnp.concatenate([bd_in(abr, k), bd_in(abi, k)], axis=1), one_step],
                                     axis=0).astype(BF16)
        wor_ref[k] = jnp.concatenate([bd_out(car, k), bd_out(ca2r, k)], axis=1).astype(BF16)
        woi_ref[k] = (-jnp.concatenate([bd_out(cai, k), bd_out(ca2i, k)], axis=1)).astype(BF16)
        m0, m1 = bd_mix(k0, k), bd_mix(k1, k)
        wl_ref[k] = jnp.concatenate([jnp.concatenate([m0, m1], axis=1),
                                     jnp.concatenate([jnp.zeros_like(m0), m0], axis=1)], axis=0).astype(BF16)


def _ssm_weights(log_dt, lam_re, lam_im, b_re, b_im, c_re, c_im):
    g, p = N_SSM_GROUPS, SSM_STATE
    gp = jax.ShapeDtypeStruct((g, p), F32)
    stack = lambda rows, cols: jax.ShapeDtypeStruct((CH_BLOCKS, rows, cols), BF16)
    return pl.pallas_call(
        _ssm_weights_kernel,
        out_shape=(gp, gp,
                   stack(LANES, 2 * STATE_BLOCK), stack(STATE_BLOCK, LANES), stack(STATE_BLOCK, LANES),
                   stack(2 * LANES, 2 * STATE_BLOCK), stack(STATE_BLOCK, 2 * LANES), stack(STATE_BLOCK, 2 * LANES),
                   stack(2 * LANES, 2 * LANES)),
        compiler_params=pltpu.CompilerParams(vmem_limit_bytes=VMEM_LIMIT_BYTES),
        name="s5_weights",
    )(log_dt.reshape(g, 1), lam_re, lam_im, jnp.swapaxes(b_re, 1, 2), jnp.swapaxes(b_im, 1, 2), c_re, c_im)


SCAN_COLS = 512


def _s5_kernel(h_ref, g_ref, s0r_ref, s0i_ref, ar_ref, ai_ref, wb_ref, wcr_ref, wci_ref, d_ref,
               gw_ref, gb_ref, o_ref, sr_ref, si_ref, u_s, xr_s, xi_s, *, rb, tt):
    n_slab, per_slab = h_ref.shape[:2]
    assert n_slab * per_slab == rb * tt and (n_slab == rb or tt == 1)

    def seq_rows(s):
        return pl.ds(s, per_slab, stride=n_slab) if n_slab > 1 else pl.ds(0, per_slab)

    @pl.when(pl.program_id(0) == 0)
    def _():
        sr_ref[...] = s0r_ref[...]
        si_ref[...] = s0i_ref[...]

    for s in range(n_slab):
        un = _rms(h_ref[s], g_ref[...])
        for k in range(CH_BLOCKS):
            u_s[k, seq_rows(s), :] = un[:, k * LANES:(k + 1) * LANES]
    for k in range(CH_BLOCKS):
        bu = jnp.dot(u_s[k].astype(BF16), wb_ref[k], preferred_element_type=F32)
        xr_s[:, k * STATE_BLOCK:(k + 1) * STATE_BLOCK] = bu[:, :STATE_BLOCK]
        xi_s[:, k * STATE_BLOCK:(k + 1) * STATE_BLOCK] = bu[:, STATE_BLOCK:]

    for c in range(STATE_W // SCAN_COLS):
        cs = slice(c * SCAN_COLS, (c + 1) * SCAN_COLS)
        a_r = jnp.broadcast_to(ar_ref[:, cs], (rb, SCAN_COLS))
        a_i = jnp.broadcast_to(ai_ref[:, cs], (rb, SCAN_COLS))

        def step(t, carry, cs=cs, a_r=a_r, a_i=a_i):
            s_r, s_i = carry
            rows = pl.ds(pl.multiple_of(t * rb, rb), rb)
            n_r = a_r * s_r - a_i * s_i + xr_s[rows, cs]
            n_i = a_r * s_i + a_i * s_r + xi_s[rows, cs]
            xr_s[rows, cs] = n_r
            xi_s[rows, cs] = n_i
            return n_r, n_i

        s_r, s_i = lax.fori_loop(0, tt, step, (sr_ref[:, cs], si_ref[:, cs]), unroll=True)
        sr_ref[:, cs] = s_r
        si_ref[:, cs] = s_i

    ys = []
    for k in range(CH_BLOCKS):
        ks = slice(k * STATE_BLOCK, (k + 1) * STATE_BLOCK)
        ys.append(jnp.dot(xr_s[:, ks].astype(BF16), wcr_ref[k], preferred_element_type=F32)
                  + jnp.dot(xi_s[:, ks].astype(BF16), wci_ref[k], preferred_element_type=F32)
                  + d_ref[:, k * LANES:(k + 1) * LANES] * u_s[k])
    act = jax.nn.gelu(jnp.concatenate(ys, axis=1)).astype(BF16)
    z = jnp.dot(act, gw_ref[...], preferred_element_type=F32) + gb_ref[...]
    mix = z[:, :D_MODEL] * jax.nn.sigmoid(z[:, D_MODEL:])
    for k in range(CH_BLOCKS):
        u_s[k] = mix[:, k * LANES:(k + 1) * LANES]
    for s in range(n_slab):
        o_ref[s] = h_ref[s] + jnp.concatenate([u_s[k, seq_rows(s), :] for k in range(CH_BLOCKS)], axis=1)


def _s5(h, g, s0r, s0i, ar, ai, wb, wcr, wci, d_skip, glu_w, glu_b, *, layer, tt):
    rb, t_len, _ = h.shape
    assert t_len % tt == 0 and rb % 8 == 0
    rows = rb * tt
    block = (1, rb, D_MODEL) if t_len == 1 else (rb, tt, D_MODEL)
    h_in = h.reshape(1, rb, D_MODEL) if t_len == 1 else h
    state = jax.ShapeDtypeStruct((rb, STATE_W), F32)
    out, sr, si = pl.pallas_call(
        functools.partial(_s5_kernel, rb=rb, tt=tt),
        grid=(t_len // tt,),
        in_specs=[pl.BlockSpec(block, lambda i: (0, i, 0)),
                  _const_spec((1, D_MODEL)),
                  _const_spec((rb, STATE_W)), _const_spec((rb, STATE_W)),
                  _const_spec((1, STATE_W)), _const_spec((1, STATE_W)),
                  _const_spec((CH_BLOCKS, LANES, 2 * STATE_BLOCK)),
                  _const_spec((CH_BLOCKS, STATE_BLOCK, LANES)),
                  _const_spec((CH_BLOCKS, STATE_BLOCK, LANES)),
                  _const_spec((1, D_MODEL)),
                  _layer_spec((D_MODEL, 2 * D_MODEL), layer),
                  _const_spec((1, 2 * D_MODEL))],
        out_specs=(pl.BlockSpec(block, lambda i: (0, i, 0)),
                   pl.BlockSpec((rb, STATE_W), lambda i: (0, 0)),
                   pl.BlockSpec((rb, STATE_W), lambda i: (0, 0))),
        out_shape=(jax.ShapeDtypeStruct(h_in.shape, F32), state, state),
        scratch_shapes=[pltpu.VMEM((CH_BLOCKS, rows, LANES), F32),
                        pltpu.VMEM((rows, STATE_W), F32), pltpu.VMEM((rows, STATE_W), F32)],
        compiler_params=_params("arbitrary"),
        name="s5",
    )(h_in, g.reshape(1, D_MODEL), s0r, s0i, ar, ai, wb, wcr, wci, d_skip.reshape(1, D_MODEL),
      glu_w, glu_b.reshape(1, 2 * D_MODEL))
    return out.reshape(h.shape), sr, si


def _s5_pair_kernel(h_ref, g_ref, ar_ref, ai_ref, win_ref, wor_ref, woi_ref, wl_ref, d_ref,
                    gw_ref, gb_ref, o_ref, sr_ref, si_ref, u_s, xr_s, xi_s, *, rb, tt):
    pairs = tt // 2
    prow = pairs * rb

    def seq_rows(s):
        return pl.ds(s, tt, stride=rb)

    def split(x):
        x3 = x.reshape(pairs, 2 * rb, x.shape[-1])
        return x3[:, :rb].reshape(prow, x.shape[-1]), x3[:, rb:].reshape(prow, x.shape[-1])

    def interleave(even, odd):
        n = even.shape[-1]
        return jnp.concatenate([even.reshape(pairs, rb, n), odd.reshape(pairs, rb, n)], axis=1).reshape(tt * rb, n)

    @pl.when(pl.program_id(0) == 0)
    def _():
        sr_ref[...] = jnp.zeros_like(sr_ref)
        si_ref[...] = jnp.zeros_like(si_ref)

    for s in range(rb):
        un = _rms(h_ref[s], g_ref[...])
        for k in range(CH_BLOCKS):
            u_s[k, seq_rows(s), :] = un[:, k * LANES:(k + 1) * LANES]

    lhs = []
    for k in range(CH_BLOCKS):
        ks = slice(k * STATE_BLOCK, (k + 1) * STATE_BLOCK)
        lhs.append(jnp.concatenate(split(u_s[k]), axis=1).astype(BF16))
        w = jnp.dot(lhs[k], win_ref[k], preferred_element_type=F32)
        xr_s[:, ks] = w[:, :STATE_BLOCK]
        xi_s[:, ks] = w[:, STATE_BLOCK:]

    for c in range(STATE_W // SCAN_COLS):
        cs = slice(c * SCAN_COLS, (c + 1) * SCAN_COLS)
        a_r, a_i = ar_ref[:, cs], ai_ref[:, cs]
        a2_r = jnp.broadcast_to(a_r * a_r - a_i * a_i, (rb, SCAN_COLS))
        a2_i = jnp.broadcast_to(2.0 * a_r * a_i, (rb, SCAN_COLS))

        def step(j, carry, cs=cs, a2_r=a2_r, a2_i=a2_i):
            s_r, s_i = carry
            rows = pl.ds(pl.multiple_of(j * rb, rb), rb)
            n_r = a2_r * s_r - a2_i * s_i + xr_s[rows, cs]
            n_i = a2_r * s_i + a2_i * s_r + xi_s[rows, cs]
            xr_s[rows, cs] = s_r
            xi_s[rows, cs] = s_i
            return n_r, n_i

        s_r, s_i = lax.fori_loop(0, pairs, step, (sr_ref[:, cs], si_ref[:, cs]), unroll=True)
        sr_ref[:, cs] = s_r
        si_ref[:, cs] = s_i

    ys = []
    for k in range(CH_BLOCKS):
        ks = slice(k * STATE_BLOCK, (k + 1) * STATE_BLOCK)
        y2 = (jnp.dot(xr_s[:, ks].astype(BF16), wor_ref[k], preferred_element_type=F32)
              + jnp.dot(xi_s[:, ks].astype(BF16), woi_ref[k], preferred_element_type=F32)
              + jnp.dot(lhs[k], wl_ref[k], preferred_element_type=F32))
        ys.append(interleave(y2[:, :LANES], y2[:, LANES:]) + d_ref[:, k * LANES:(k + 1) * LANES] * u_s[k])
    act = jax.nn.gelu(jnp.concatenate(ys, axis=1)).astype(BF16)
    z = jnp.dot(act, gw_ref[...], preferred_element_type=F32) + gb_ref[...]
    mix = z[:, :D_MODEL] * jax.nn.sigmoid(z[:, D_MODEL:])
    for k in range(CH_BLOCKS):
        u_s[k] = mix[:, k * LANES:(k + 1) * LANES]
    for s in range(rb):
        o_ref[s] = h_ref[s] + jnp.concatenate([u_s[k, seq_rows(s), :] for k in range(CH_BLOCKS)], axis=1)


def _s5_pairs(h, g, ar, ai, w_in, w_out_r, w_out_i, w_loc, d_skip, glu_w, glu_b, *, layer, tt):
    rb, t_len, _ = h.shape
    assert t_len % tt == 0 and tt % 2 == 0 and rb % 8 == 0
    rows, prow = rb * tt, rb * tt // 2
    block = (rb, tt, D_MODEL)
    state = jax.ShapeDtypeStruct((rb, STATE_W), F32)
    return pl.pallas_call(
        functools.partial(_s5_pair_kernel, rb=rb, tt=tt),
        grid=(t_len // tt,),
        in_specs=[pl.BlockSpec(block, lambda i: (0, i, 0)),
                  _const_spec((1, D_MODEL)),
                  _const_spec((1, STATE_W)), _const_spec((1, STATE_W)),
                  _const_spec((CH_BLOCKS, 2 * LANES, 2 * STATE_BLOCK)),
                  _const_spec((CH_BLOCKS, STATE_BLOCK, 2 * LANES)),
                  _const_spec((CH_BLOCKS, STATE_BLOCK, 2 * LANES)),
                  _const_spec((CH_BLOCKS, 2 * LANES, 2 * LANES)),
                  _const_spec((1, D_MODEL)),
                  _layer_spec((D_MODEL, 2 * D_MODEL), layer),
                  _const_spec((1, 2 * D_MODEL))],
        out_specs=(pl.BlockSpec(block, lambda i: (0, i, 0)),
                   pl.BlockSpec((rb, STATE_W), lambda i: (0, 0)),
                   pl.BlockSpec((rb, STATE_W), lambda i: (0, 0))),
        out_shape=(jax.ShapeDtypeStruct(h.shape, F32), state, state),
        scratch_shapes=[pltpu.VMEM((CH_BLOCKS, rows, LANES), F32),
                        pltpu.VMEM((prow, STATE_W), F32), pltpu.VMEM((prow, STATE_W), F32)],
        compiler_params=_params("arbitrary"),
        name="s5_pairs",
    )(h, g.reshape(1, D_MODEL), ar, ai, w_in, w_out_r, w_out_i, w_loc, d_skip.reshape(1, D_MODEL),
      glu_w, glu_b.reshape(1, 2 * D_MODEL))


def _proj_kernel(x_ref, g_ref, w_ref, o_ref, *, scale):
    xn = _rms(x_ref[...], g_ref[...]).astype(BF16)
    y = jnp.dot(xn, w_ref[...], preferred_element_type=F32)
    o_ref[...] = y if scale == 1.0 else y * scale


def _proj(x, g, w, layer=None, scale=1.0, tm=512):
    n = x.shape[0]
    tm = min(tm, n)
    nout = w.shape[-1]
    return pl.pallas_call(
        functools.partial(_proj_kernel, scale=scale),
        grid=(n // tm,),
        in_specs=[pl.BlockSpec((tm, D_MODEL), lambda i: (i, 0)),
                  _const_spec((1, D_MODEL)),
                  _layer_spec((D_MODEL, nout), layer)],
        out_specs=pl.BlockSpec((tm, nout), lambda i: (i, 0)),
        out_shape=jax.ShapeDtypeStruct((n, nout), F32),
        compiler_params=_params("parallel"),
        name="proj",
    )(x, g.reshape(1, D_MODEL), w)


def _kv_tails(t_len, tm):
    tails = []
    for w in DIL_WINDOWS:
        n_tail = min(w, t_len)
        if n_tail >= tm:
            assert n_tail % tm == 0
            tails.append(((t_len - n_tail) // tm, 0, tm))
        else:
            tails.append((t_len // tm - 1, tm - n_tail, n_tail))
    return tails


def _kv_kernel(x_ref, g_ref, w_ref, kv_ref, *tail_refs, tails):
    j = pl.program_id(1)
    xn = _rms(x_ref[...], g_ref[...]).astype(BF16)
    kv = jnp.dot(xn, w_ref[...], preferred_element_type=F32)
    kv_ref[...] = kv
    for g, (first_tile, row0, n_tok) in enumerate(tails):
        def emit(g=g, row0=row0, n_tok=n_tok):
            tail_refs[g][...] = kv[row0:row0 + n_tok, g * 2 * ATTN_WIDTH:(g + 1) * 2 * ATTN_WIDTH].T

        if first_tile == 0:
            emit()
        else:
            pl.when(j >= first_tile)(emit)


def _kv_proj(x, g, w, tm=512):
    bn, t_len, _ = x.shape
    tm = min(tm, t_len)
    assert t_len % tm == 0
    nout = w.shape[-1]
    tails = _kv_tails(t_len, tm)
    tail_specs, tail_shapes = [], []
    for (first_tile, _, n_tok), win in zip(tails, DIL_WINDOWS):
        tail_specs.append(pl.BlockSpec((None, 2 * ATTN_WIDTH, n_tok),
                                       lambda b, j, f=first_tile: (b, 0, jnp.maximum(j - f, 0))))
        tail_shapes.append(jax.ShapeDtypeStruct((bn, 2 * ATTN_WIDTH, min(win, t_len)), F32))
    return pl.pallas_call(
        functools.partial(_kv_kernel, tails=tails),
        grid=(bn, t_len // tm),
        in_specs=[pl.BlockSpec((None, tm, D_MODEL), lambda b, j: (b, j, 0)),
                  _const_spec((1, D_MODEL)),
                  _const_spec((D_MODEL, nout))],
        out_specs=[pl.BlockSpec((None, tm, nout), lambda b, j: (b, j, 0))] + tail_specs,
        out_shape=[jax.ShapeDtypeStruct((bn, t_len, nout), F32)] + tail_shapes,
        compiler_params=_params("parallel", "arbitrary"),
        name="kv_proj",
    )(x, g.reshape(1, D_MODEL), w)


def _oproj_kernel(a_ref, w_ref, h_ref, o_ref):
    o_ref[...] = h_ref[...] + jnp.dot(a_ref[...].astype(BF16), w_ref[...], preferred_element_type=F32)


def _oproj(a, w, layer, h, tm=512):
    n = h.shape[0]
    tm = min(tm, n)
    return pl.pallas_call(
        _oproj_kernel,
        grid=(n // tm,),
        in_specs=[pl.BlockSpec((tm, ATTN_WIDTH), lambda i: (i, 0)),
                  _layer_spec((ATTN_WIDTH, D_MODEL), layer),
                  pl.BlockSpec((tm, D_MODEL), lambda i: (i, 0))],
        out_specs=pl.BlockSpec((tm, D_MODEL), lambda i: (i, 0)),
        out_shape=jax.ShapeDtypeStruct((n, D_MODEL), F32),
        compiler_params=_params("parallel"),
        name="oproj",
    )(a, w, h)


def _alibi_slopes():
    n = N_DIL * HEADS_PER_GROUP
    s = jnp.exp2(-8.0 * jnp.arange(1, n + 1, dtype=F32) / n)
    return s.reshape(N_DIL, HEADS_PER_GROUP)


def _prompt_bias():
    qi = jnp.arange(R_LEN)[:, None]
    kj = jnp.arange(2 * R_LEN)[None, :]
    diff = R_LEN + qi - kj
    valid = (diff >= 0) & (diff <= R_LEN)
    dil = jnp.asarray(DIL_RATES, jnp.int32)[:, None, None]
    dist = (dil * diff[None]).astype(F32)
    bias = -_alibi_slopes()[:, :, None, None] * dist[:, None]
    return jnp.where(valid[None, None], bias * LOG2_E, NEG)


def _sample_bias():
    kk = R_LEN - jnp.arange(R_LEN)
    dil = jnp.asarray(DIL_RATES, jnp.int32)[:, None]
    dist = (dil * kk[None]).astype(F32)
    return -_alibi_slopes()[:, :, None] * dist[:, None]


MERGE_ROWS = 256
INTERLEAVE = 16


def _attn_kernel(q0, q1, q2, k0, k1, k2, v0, v1, v2, bias_ref, o_ref, acc_s, m_s, l_s, *, t_len):
    qs, ks, vs = (q0, q1, q2), (k0, k1, k2), (v0, v1, v2)
    lo = lax.broadcasted_iota(jnp.int32, (R_LEN, LANES), 1) < HEAD_DIM

    def block(g, q_idx, k_idx, first):
        q = qs[g][q_idx, :]
        qb = jnp.concatenate([jnp.where(lo, q, 0.0), jnp.where(lo, 0.0, q)], axis=0).astype(BF16)
        kb = ks[g][k_idx, :].astype(BF16)
        vb = vs[g][k_idx, :].astype(BF16)
        s = lax.dot_general(qb, kb, (((1,), (1,)), ((), ())), preferred_element_type=F32)
        bias = bias_ref[g, :, :, R_LEN:] if first else bias_ref[g]
        bias = bias.reshape(2 * R_LEN, bias.shape[-1])
        s = jnp.where(bias > 0.5 * NEG, s + bias, NEG)
        m = jnp.max(s, axis=-1, keepdims=True)
        p = jnp.exp2(s - m)
        l = jnp.sum(p, axis=-1, keepdims=True)
        a = jnp.dot(p.astype(BF16), vb, preferred_element_type=F32)
        acc_s[g, q_idx, :] = jnp.where(lo, a[:R_LEN], a[R_LEN:])
        m_s[g, q_idx, :] = jnp.where(lo, m[:R_LEN], m[R_LEN:])
        l_s[g, q_idx, :] = jnp.where(lo, l[:R_LEN], l[R_LEN:])

    for g in range(N_DIL):
        dil = DIL_RATES[g]
        span = R_LEN * dil
        n_blocks = t_len // span

        def rows(start, n, dil=dil):
            if dil > 1:
                return pl.ds(start, n, stride=dil)
            return pl.ds(start if isinstance(start, int) else pl.multiple_of(start, R_LEN), n)

        if dil <= INTERLEAVE:
            for r in range(dil):
                block(g, rows(r, R_LEN), rows(r, R_LEN), True)
        else:
            def first_blocks(i, c, g=g, rows=rows):
                for j in range(INTERLEAVE):
                    r = i * INTERLEAVE + j
                    block(g, rows(r, R_LEN), rows(r, R_LEN), True)
                return c

            lax.fori_loop(0, dil // INTERLEAVE, first_blocks, 0)

        later = n_blocks - 1
        if later == 0:
            continue
        per_trip = max(w for w in range(1, max(INTERLEAVE // dil, 1) + 1) if later % w == 0)

        def later_blocks(i, c, g=g, dil=dil, span=span, rows=rows, per_trip=per_trip):
            for j in range(per_trip):
                nb = 1 + i * per_trip + j
                for r in range(dil):
                    start = nb * span + r
                    block(g, rows(start, R_LEN), rows(start - span, 2 * R_LEN), False)
            return c

        if later == per_trip:
            later_blocks(0, 0)
        else:
            lax.fori_loop(0, later // per_trip, later_blocks, 0)

    def merge(i, c):
        rows = pl.ds(pl.multiple_of(i * MERGE_ROWS, MERGE_ROWS), MERGE_ROWS)
        ms = [m_s[g, rows, :] for g in range(N_DIL)]
        mx = jnp.maximum(jnp.maximum(ms[0], ms[1]), ms[2])
        ws = [jnp.exp2(m - mx) for m in ms]
        den = ws[0] * l_s[0, rows, :] + ws[1] * l_s[1, rows, :] + ws[2] * l_s[2, rows, :]
        num = ws[0] * acc_s[0, rows, :] + ws[1] * acc_s[1, rows, :] + ws[2] * acc_s[2, rows, :]
        o_ref[rows, :] = num / den
        return c

    lax.fori_loop(0, t_len // MERGE_ROWS, merge, 0)


def _attn_prompt(q, kv, bias):
    bn, t_len, _ = q.shape
    pairs = ATTN_WIDTH // LANES
    per_group = 2 * ATTN_WIDTH // LANES

    def col(off):
        return pl.BlockSpec((None, t_len, LANES), lambda b, hp: (b, 0, off + hp))

    in_specs = ([col(g * pairs) for g in range(N_DIL)]
                + [col(g * per_group) for g in range(N_DIL)]
                + [col(g * per_group + pairs) for g in range(N_DIL)]
                + [pl.BlockSpec((N_DIL, 2, R_LEN, 2 * R_LEN), lambda b, hp: (0, hp, 0, 0))])
    scratch = pltpu.VMEM((N_DIL, t_len, LANES), F32)
    return pl.pallas_call(
        functools.partial(_attn_kernel, t_len=t_len),
        grid=(bn, pairs),
        in_specs=in_specs,
        out_specs=pl.BlockSpec((None, t_len, LANES), lambda b, hp: (b, 0, hp)),
        out_shape=jax.ShapeDtypeStruct((bn, t_len, ATTN_WIDTH), F32),
        scratch_shapes=[scratch, scratch, scratch],
        compiler_params=_params("parallel", "parallel"),
        name="attn_prompt",
    )(q, q, q, kv, kv, kv, kv, kv, kv, bias)


SAMPLE_BLOCK = 8


def _window_kernel(c_ref, o_ref, *, dil):
    w = c_ref.shape[-1]
    pos = lax.broadcasted_iota(jnp.int32, (w, R_LEN), 0)
    col = lax.broadcasted_iota(jnp.int32, (w, R_LEN), 1)
    pick = jnp.where(pos == col * dil, 1.0, 0.0).astype(BF16)
    o_ref[...] = jnp.dot(c_ref[...].astype(BF16), pick, preferred_element_type=F32).astype(BF16)


def _strided_window(cache, dil):
    bd, w = cache.shape[:2]
    assert w == R_LEN * dil
    slab = jnp.transpose(cache, (0, 2, 3, 4, 1)).reshape(bd, 2 * ATTN_WIDTH, w)
    if dil == 1:
        return slab
    return pl.pallas_call(
        functools.partial(_window_kernel, dil=dil),
        grid=(bd,),
        in_specs=[pl.BlockSpec((None, 2 * ATTN_WIDTH, w), lambda i: (i, 0, 0))],
        out_specs=pl.BlockSpec((None, 2 * ATTN_WIDTH, R_LEN), lambda i: (i, 0, 0)),
        out_shape=jax.ShapeDtypeStruct((bd, 2 * ATTN_WIDTH, R_LEN), BF16),
        compiler_params=_params("parallel"),
        name="cache_window",
    )(slab)


def _attn_sample_kernel(q_ref, kvn_ref, c0, c1, c2, bias_ref, o_ref):
    windows = (c0, c1, c2)
    sub = lax.broadcasted_iota(jnp.int32, (HEADS_PER_GROUP, ATTN_WIDTH), 0)
    lane = lax.broadcasted_iota(jnp.int32, (HEADS_PER_GROUP, ATTN_WIDTH), 1)
    own = (lane // HEAD_DIM) == sub
    for b in range(SAMPLE_BLOCK):
        os_, ms, ls = [], [], []
        for g in range(N_DIL):
            base = g * 2 * ATTN_WIDTH
            qbd = jnp.where(own, q_ref[b:b + 1, g * ATTN_WIDTH:(g + 1) * ATTN_WIDTH], 0.0)
            kt = windows[g][b, :ATTN_WIDTH, :].astype(BF16)
            vt = windows[g][b, ATTN_WIDTH:, :].astype(BF16)
            kn = kvn_ref[b:b + 1, base:base + ATTN_WIDTH]
            vn = kvn_ref[b:b + 1, base + ATTN_WIDTH:base + 2 * ATTN_WIDTH]
            s = jnp.dot(qbd.astype(BF16), kt, preferred_element_type=F32) + bias_ref[g]
            s_new = jnp.sum(qbd * kn, axis=-1, keepdims=True)
            m = jnp.maximum(jnp.max(s, axis=-1, keepdims=True), s_new)
            p = jnp.exp(s - m)
            p_new = jnp.exp(s_new - m)
            ls.append(jnp.sum(p, axis=-1, keepdims=True) + p_new)
            os_.append(lax.dot_general(p.astype(BF16), vt, (((1,), (1,)), ((), ())),
                                       preferred_element_type=F32) + p_new * vn)
            ms.append(m)
        mx = jnp.maximum(jnp.maximum(ms[0], ms[1]), ms[2])
        ws = [jnp.exp(m - mx) for m in ms]
        den = ws[0] * ls[0] + ws[1] * ls[1] + ws[2] * ls[2]
        o = (ws[0] * os_[0] + ws[1] * os_[1] + ws[2] * os_[2]) / den
        o_ref[b:b + 1, :] = jnp.sum(jnp.where(own, o, 0.0), axis=0, keepdims=True)


def _attn_sample(q, kv_new, windows, bias):
    bd = q.shape[0]
    assert bd % SAMPLE_BLOCK == 0
    window_spec = pl.BlockSpec((SAMPLE_BLOCK, 2 * ATTN_WIDTH, R_LEN), lambda i: (i, 0, 0))
    return pl.pallas_call(
        _attn_sample_kernel,
        grid=(bd // SAMPLE_BLOCK,),
        in_specs=[pl.BlockSpec((SAMPLE_BLOCK, N_DIL * ATTN_WIDTH), lambda i: (i, 0)),
                  pl.BlockSpec((SAMPLE_BLOCK, N_DIL * 2 * ATTN_WIDTH), lambda i: (i, 0))]
                 + [window_spec] * N_DIL + [_const_spec((N_DIL, HEADS_PER_GROUP, R_LEN))],
        out_specs=pl.BlockSpec((SAMPLE_BLOCK, ATTN_WIDTH), lambda i: (i, 0)),
        out_shape=jax.ShapeDtypeStruct((bd, ATTN_WIDTH), F32),
        compiler_params=_params("parallel"),
        name="attn_sample",
    )(q, kv_new, *windows, bias)


def kernel(x_prompt, x_sample, state_ssm_re, state_ssm_im, cache_kv_w128, cache_kv_w512, cache_kv_w2048, norm_ffn1, ffn1_w_in, ffn1_w_out, norm_mix, norm_ffn2, ffn2_w_in, ffn2_w_out, norm_kv, norm_final, ssm_log_dt, ssm_lambda_re, ssm_lambda_im, ssm_b_re, ssm_b_im, ssm_c_re, ssm_c_im, ssm_d, glu_w, glu_b, attn_w_q, attn_w_kv, attn_w_o):
    bn, t_len, _ = x_prompt.shape
    bd = x_sample.shape[0]
    assert x_sample.shape[1] == 1

    glu_wb = glu_w.astype(BF16)
    w_q, w_kv, w_o = attn_w_q.astype(BF16), attn_w_kv.astype(BF16), attn_w_o.astype(BF16)
    q_scale = HEAD_DIM ** -0.5

    ssm, ssm_pairs = [], []
    for layer in range(N_A_LAYERS):
        ar, ai, wb, wcr, wci, *pair_w = _ssm_weights(
            ssm_log_dt[layer], ssm_lambda_re[layer], ssm_lambda_im[layer],
            ssm_b_re[layer], ssm_b_im[layer], ssm_c_re[layer], ssm_c_im[layer])
        ar, ai = ar.reshape(1, STATE_W), ai.reshape(1, STATE_W)
        tail = (ssm_d[layer].reshape(D_MODEL), glu_wb, glu_b[layer])
        ssm.append((ar, ai, wb, wcr, wci) + tail)
        ssm_pairs.append((ar, ai, *pair_w) + tail)

    def states(parts, n_seq):
        return jnp.stack([p.reshape(n_seq, N_SSM_GROUPS, SSM_STATE) for p in parts])

    s0r = state_ssm_re.reshape(N_A_LAYERS, bd, STATE_W)
    s0i = state_ssm_im.reshape(N_A_LAYERS, bd, STATE_W)
    windows = [_strided_window(cache, dil)
               for cache, dil in zip((cache_kv_w128, cache_kv_w512, cache_kv_w2048), DIL_RATES)]
    sample_bias, prompt_bias = _sample_bias(), _prompt_bias()
    fin_r, fin_i, fin_rs, fin_is = [], [], [], []
    h = x_prompt.reshape(bn * t_len, D_MODEL)
    hs = x_sample.reshape(bd, D_MODEL)
    hs, w1 = _ffn_cast(hs, norm_ffn1[0], ffn1_w_in, ffn1_w_out, 0)
    for layer in range(DEPTH):
        last = layer == DEPTH - 1
        if layer == N_A_LAYERS:
            kv_p, *kv_tails = _kv_proj(h.reshape(bn, t_len, D_MODEL), norm_kv, w_kv)
            kv_s = _proj(hs, norm_kv, w_kv)
        ffn1 = _ffn(h, norm_ffn1[layer], w1, cast_next=(ffn2_w_in, ffn2_w_out, layer),
                    q_proj=((norm_mix[layer], w_q, layer - N_A_LAYERS, q_scale * LOG2_E)
                            if layer >= N_A_LAYERS else None),
                    extra=hs if layer > 0 else None)
        if layer > 0:
            hs = ffn1[-1]
        h, w2 = ffn1[0], ffn1[2 if layer >= N_A_LAYERS else 1]
        if layer < N_A_LAYERS:
            h, sr, si = _s5_pairs(h.reshape(bn, t_len, D_MODEL), norm_mix[layer],
                                  *ssm_pairs[layer], layer=layer, tt=64)
            h = h.reshape(bn * t_len, D_MODEL)
            fin_r.append(sr)
            fin_i.append(si)
            hs, sr, si = _s5(hs.reshape(bd, 1, D_MODEL), norm_mix[layer], s0r[layer], s0i[layer],
                             *ssm[layer], layer=layer, tt=1)
            hs = hs.reshape(bd, D_MODEL)
            fin_rs.append(sr)
            fin_is.append(si)
            attn = None
        else:
            b = layer - N_A_LAYERS
            merged = _attn_prompt(ffn1[1].reshape(bn, t_len, -1), kv_p, prompt_bias)
            attn = (merged.reshape(bn * t_len, ATTN_WIDTH), w_o, b)
            qs = _proj(hs, norm_mix[layer], w_q, layer=b, scale=q_scale)
            hs = _oproj(_attn_sample(qs, kv_s, windows, sample_bias), w_o, b, hs)
        if last:
            h, hs = _ffn(h, norm_ffn2[layer], w2, attn=attn, final_g=norm_final, extra=hs)
        else:
            h, w1, hs = _ffn(h, norm_ffn2[layer], w2, attn=attn, extra=hs,
                             cast_next=(ffn1_w_in, ffn1_w_out, layer + 1))
    y_prompt = h.reshape(bn, t_len, D_MODEL)
    y_sample = hs.reshape(bd, 1, D_MODEL)
    ssm_re_p, ssm_im_p = states(fin_r, bn), states(fin_i, bn)
    ssm_re_s, ssm_im_s = states(fin_rs, bd), states(fin_is, bd)
    kv_s = kv_s.reshape(bd, 1, N_DIL, 2, HEADS_PER_GROUP, HEAD_DIM)
    kv_out_p = [jnp.transpose(tail.reshape(bn, 2, HEADS_PER_GROUP, HEAD_DIM, -1), (0, 4, 1, 2, 3))
                for tail in kv_tails]

    return (y_prompt, y_sample, ssm_re_p, ssm_im_p, kv_out_p[0], kv_out_p[1], kv_out_p[2],
            ssm_re_s, ssm_im_s, kv_s[:, :, 0], kv_s[:, :, 1], kv_s[:, :, 2])
```

```python
import functools

import jax
import jax.numpy as jnp
from jax import lax
from jax.experimental import pallas as pl
from jax.experimental.pallas import tpu as pltpu

F32 = jnp.float32
BF16 = jnp.bfloat16

D_MODEL = 1024
DEPTH = 4
N_A_LAYERS = DEPTH // 2
SSM_GROUP = 16
N_SSM_GROUPS = D_MODEL // SSM_GROUP
SSM_STATE = 64
STATE_W = N_SSM_GROUPS * SSM_STATE
DIL_WINDOWS = (128, 512, 2048)
DIL_RATES = (1, 4, 16)
N_DIL = 3
HEADS_PER_GROUP = 8
HEAD_DIM = 64
ATTN_WIDTH = HEADS_PER_GROUP * HEAD_DIM
R_LEN = DIL_WINDOWS[0] // DIL_RATES[0]
D_FF = 2816
EPS = 1e-6
NEG = -1e30
LOG2_E = 1.4426950408889634

LANES = 128
MXU_TILE = 256
BF16_SUBLANES = 16
CH_BLOCKS = D_MODEL // LANES
GROUPS_PER_BLOCK = LANES // SSM_GROUP
STATE_BLOCK = GROUPS_PER_BLOCK * SSM_STATE
VMEM_LIMIT_BYTES = 56 * 1024 * 1024

assert all(w // d == R_LEN for w, d in zip(DIL_WINDOWS, DIL_RATES))


def _params(*semantics):
    return pltpu.CompilerParams(dimension_semantics=semantics, vmem_limit_bytes=VMEM_LIMIT_BYTES)


def _const_spec(shape):
    zeros = (0,) * len(shape)
    return pl.BlockSpec(shape, lambda *_: zeros, pipeline_mode=pl.Buffered(1))


def _layer_spec(shape, layer):
    if layer is None:
        return _const_spec(shape)
    idx = (layer,) + (0,) * len(shape)
    return pl.BlockSpec((None,) + tuple(shape), lambda *_: idx, pipeline_mode=pl.Buffered(1))


def _rms(x, g):
    return x * lax.rsqrt(jnp.mean(x * x, axis=-1, keepdims=True) + EPS) * g


FFN_SUB_ROWS = 512


def _ff_chunks(n_chunks):
    tiles = D_FF // MXU_TILE
    assert tiles * MXU_TILE == D_FF and n_chunks <= tiles
    cuts = [MXU_TILE * ((tiles * c + n_chunks - 1) // n_chunks) for c in range(n_chunks + 1)]
    return list(zip(cuts[:-1], cuts[1:]))


def _swiglu_residual(x, g, wa_ref, wb_ref, wout_ref, n_chunks):
    xn = _rms(x, g).astype(BF16)
    acc = None
    for lo, hi in _ff_chunks(n_chunks):
        a = jnp.dot(xn, wa_ref[:, lo:hi], preferred_element_type=F32)
        b = jnp.dot(xn, wb_ref[:, lo:hi], preferred_element_type=F32)
        act = (a * jax.nn.sigmoid(a) * b).astype(BF16)
        d = jnp.dot(act, wout_ref[lo:hi, :], preferred_element_type=F32)
        acc = d if acc is None else acc + d
    return x + 0.5 * acc


def _ffn_kernel(*refs, n_chunks, attn_in, final, q_scale, cast_next, extra_steps):
    refs = list(refs)
    x_ref = refs.pop(0)
    if attn_in:
        a_ref, wo_ref = refs.pop(0), refs.pop(0)
    g_ref, wa_ref, wb_ref, wout_ref = (refs.pop(0) for _ in range(4))
    if final:
        gf_ref = refs.pop(0)
    if q_scale is not None:
        gq_ref, wq_ref = refs.pop(0), refs.pop(0)
    if cast_next:
        next_f32 = [refs.pop(0) for _ in range(3)]
    if extra_steps:
        xs_ref = refs.pop(0)
    o_ref = refs.pop(0)
    if q_scale is not None:
        q_ref = refs.pop(0)
    if cast_next:
        for src in next_f32:
            refs.pop(0)[...] = src[...].astype(BF16)

    def main():
        tm = x_ref.shape[0]
        for r0 in range(0, tm, min(tm, FFN_SUB_ROWS)):
            rows = slice(r0, r0 + min(tm, FFN_SUB_ROWS))
            x = x_ref[rows, :]
            if attn_in:
                x = x + jnp.dot(a_ref[rows, :].astype(BF16), wo_ref[...], preferred_element_type=F32)
            y = _swiglu_residual(x, g_ref[...], wa_ref, wb_ref, wout_ref, n_chunks)
            if final:
                y = _rms(y, gf_ref[...])
            o_ref[rows, :] = y
            if q_scale is not None:
                q = jnp.dot(_rms(y, gq_ref[...]).astype(BF16), wq_ref[...], preferred_element_type=F32)
                q_ref[rows, :] = q * q_scale

    if not extra_steps:
        main()
        return
    ys_ref = refs.pop(0)
    pl.when(pl.program_id(0) < extra_steps)(main)

    @pl.when(pl.program_id(0) == extra_steps)
    def _():
        ys = _swiglu_residual(xs_ref[...], g_ref[...], wa_ref, wb_ref, wout_ref, n_chunks)
        ys_ref[...] = _rms(ys, gf_ref[...]) if final else ys


def _ffn(x, g, weights, attn=None, final_g=None, q_proj=None, cast_next=None, extra=None, n_chunks=2):
    n = x.shape[0]
    tm = min(FFN_SUB_ROWS * (1 if q_proj is not None else 2), n)
    assert n % tm == 0
    steps = n // tm
    step = (lambda i: jnp.minimum(i, steps - 1)) if extra is not None else (lambda i: i)
    rows = lambda width: pl.BlockSpec((tm, width), lambda i: (step(i), 0))
    in_specs, args = [rows(D_MODEL)], [x]
    if attn is not None:
        merged, w_o, idx = attn
        in_specs += [rows(ATTN_WIDTH), _layer_spec((ATTN_WIDTH, D_MODEL), idx)]
        args += [merged, w_o]
    in_specs += [_const_spec((1, D_MODEL)), _const_spec((D_MODEL, D_FF)), _const_spec((D_MODEL, D_FF)),
                 _const_spec((D_FF, D_MODEL))]
    args += [g.reshape(1, D_MODEL), *weights]
    if final_g is not None:
        in_specs.append(_const_spec((1, D_MODEL)))
        args.append(final_g.reshape(1, D_MODEL))
    out_specs, out_shape = [rows(D_MODEL)], [jax.ShapeDtypeStruct((n, D_MODEL), F32)]
    q_scale = None
    if q_proj is not None:
        gq, w_q, idx, q_scale = q_proj
        nq = w_q.shape[-1]
        in_specs += [_const_spec((1, D_MODEL)), _layer_spec((D_MODEL, nq), idx)]
        args += [gq.reshape(1, D_MODEL), w_q]
        out_specs.append(rows(nq))
        out_shape.append(jax.ShapeDtypeStruct((n, nq), F32))
    if cast_next is not None:
        w_in, w_out, idx = cast_next
        r_in = D_MODEL // steps
        out_blocks = [b for b in range(1, steps + 1) if steps % b == 0 and D_FF % b == 0
                      and (D_FF // b) % BF16_SUBLANES == 0][-1]
        r_out, hold = D_FF // out_blocks, steps // out_blocks
        assert r_in * steps == D_MODEL and r_in % BF16_SUBLANES == 0
        in_specs += [pl.BlockSpec((None, r_in, D_FF), lambda i: (idx, step(i), 0)),
                     pl.BlockSpec((None, r_in, D_FF), lambda i: (idx, step(i), 1)),
                     pl.BlockSpec((None, r_out, D_MODEL), lambda i: (idx, step(i) // hold, 0))]
        args += [w_in, w_in, w_out]
        out_specs += [pl.BlockSpec((r_in, D_FF), lambda i: (step(i), 0)),
                      pl.BlockSpec((r_in, D_FF), lambda i: (step(i), 0)),
                      pl.BlockSpec((r_out, D_MODEL), lambda i: (step(i) // hold, 0))]
        out_shape += [jax.ShapeDtypeStruct((D_MODEL, D_FF), BF16), jax.ShapeDtypeStruct((D_MODEL, D_FF), BF16),
                      jax.ShapeDtypeStruct((D_FF, D_MODEL), BF16)]
    if extra is not None:
        in_specs.append(_const_spec(extra.shape))
        args.append(extra)
        out_specs.append(pl.BlockSpec(extra.shape, lambda i: (0, 0)))
        out_shape.append(jax.ShapeDtypeStruct(extra.shape, F32))
    out = pl.pallas_call(
        functools.partial(_ffn_kernel, n_chunks=n_chunks, attn_in=attn is not None,
                          final=final_g is not None, q_scale=q_scale, cast_next=cast_next is not None,
                          extra_steps=steps if extra is not None else 0),
        grid=(steps + (extra is not None),),
        in_specs=in_specs,
        out_specs=out_specs,
        out_shape=out_shape,
        compiler_params=_params("arbitrary"),
        name="ffn",
    )(*args)
    if len(out) == 1:
        return out[0]
    parts = [out[0]]
    k = 1
    if q_proj is not None:
        parts.append(out[k])
        k += 1
    if cast_next is not None:
        parts.append(tuple(out[k:k + 3]))
        k += 3
    if extra is not None:
        parts.append(out[k])
    return tuple(parts)


def _ffn_cast_kernel(x_ref, g_ref, wa_ref, wb_ref, wout_ref, o_ref, wa_o, wb_o, wout_o, acc_s):
    c = pl.program_id(0)
    wa = wa_ref[...].astype(BF16)
    wb = wb_ref[...].astype(BF16)
    wout = wout_ref[...].astype(BF16)
    wa_o[...] = wa
    wb_o[...] = wb
    wout_o[...] = wout
    x = x_ref[...]
    xn = _rms(x, g_ref[...]).astype(BF16)
    a = jnp.dot(xn, wa, preferred_element_type=F32)
    b = jnp.dot(xn, wb, preferred_element_type=F32)
    d = jnp.dot((a * jax.nn.sigmoid(a) * b).astype(BF16), wout, preferred_element_type=F32)

    @pl.when(c == 0)
    def _():
        acc_s[...] = d

    @pl.when(c > 0)
    def _():
        acc_s[...] += d

    @pl.when(c == pl.num_programs(0) - 1)
    def _():
        o_ref[...] = x + 0.5 * acc_s[...]


def _ffn_cast(x, g, w_in, w_out, layer):
    n = x.shape[0]
    n_chunks = D_FF // MXU_TILE
    y, wa, wb, wo = pl.pallas_call(
        _ffn_cast_kernel,
        grid=(n_chunks,),
        in_specs=[_const_spec((n, D_MODEL)), _const_spec((1, D_MODEL)),
                  pl.BlockSpec((None, D_MODEL, MXU_TILE), lambda c: (layer, 0, c)),
                  pl.BlockSpec((None, D_MODEL, MXU_TILE), lambda c: (layer, 0, n_chunks + c)),
                  pl.BlockSpec((None, MXU_TILE, D_MODEL), lambda c: (layer, c, 0))],
        out_specs=[pl.BlockSpec((n, D_MODEL), lambda c: (0, 0)),
                   pl.BlockSpec((D_MODEL, MXU_TILE), lambda c: (0, c)),
                   pl.BlockSpec((D_MODEL, MXU_TILE), lambda c: (0, c)),
                   pl.BlockSpec((MXU_TILE, D_MODEL), lambda c: (c, 0))],
        out_shape=[jax.ShapeDtypeStruct((n, D_MODEL), F32),
                   jax.ShapeDtypeStruct((D_MODEL, D_FF), BF16),
                   jax.ShapeDtypeStruct((D_MODEL, D_FF), BF16),
                   jax.ShapeDtypeStruct((D_FF, D_MODEL), BF16)],
        scratch_shapes=[pltpu.VMEM((n, D_MODEL), F32)],
        compiler_params=_params("arbitrary"),
        name="ffn_cast",
    )(x, g.reshape(1, D_MODEL), w_in, w_in, w_out)
    return y, (wa, wb, wo)


def _ssm_weights_kernel(ldt_ref, lr_ref, li_ref, br_ref, bi_ref, cr_ref, ci_ref,
                        ar_ref, ai_ref, wb_ref, wcr_ref, wci_ref, win_ref, wor_ref, woi_ref, wl_ref):
    dt = jnp.exp(ldt_ref[...])
    lr = lr_ref[...]
    li = li_ref[...]
    mag = jnp.exp(lr * dt)
    ar = mag * jnp.cos(li * dt)
    ai = mag * jnp.sin(li * dt)
    den = lr * lr + li * li
    zr = ((ar - 1.0) * lr + ai * li) / den
    zi = (ai * lr - (ar - 1.0) * li) / den
    ar_ref[...] = ar
    ai_ref[...] = ai
    br = br_ref[...]
    bi = bi_ref[...]
    bbr = zr[:, None, :] * br - zi[:, None, :] * bi
    bbi = zr[:, None, :] * bi + zi[:, None, :] * br
    a1r, a1i = ar[:, None, :], ai[:, None, :]
    a2r, a2i = (ar * ar - ai * ai)[:, None, :], (2.0 * ar * ai)[:, None, :]
    abr = a1r * bbr - a1i * bbi
    abi = a1r * bbi + a1i * bbr
    cr = cr_ref[...]
    ci = ci_ref[...]
    car = cr * a1r - ci * a1i
    cai = cr * a1i + ci * a1r
    ca2r = cr * a2r - ci * a2i
    ca2i = cr * a2i + ci * a2r

    def re_dot(xr, xi):
        dims = (((2,), (2,)), ((0,), (0,)))
        return (lax.dot_general(xr, bbr, dims, preferred_element_type=F32)
                - lax.dot_general(xi, bbi, dims, preferred_element_type=F32))

    k0 = re_dot(cr, ci)
    k1 = re_dot(car, cai)

    def same_group(n_rows, row_span, n_cols, col_span):
        r = lax.broadcasted_iota(jnp.int32, (n_rows, n_cols), 0) // row_span
        c = lax.broadcasted_iota(jnp.int32, (n_rows, n_cols), 1) // col_span
        return r == c

    def replicate(n, copies, transpose):
        shape = (copies * n, n) if transpose else (n, copies * n)
        r = lax.broadcasted_iota(jnp.int32, shape, 0)
        c = lax.broadcasted_iota(jnp.int32, shape, 1)
        return jnp.where(r % n == c % n, 1.0, 0.0).astype(BF16)

    gpb = GROUPS_PER_BLOCK
    rep_p = replicate(SSM_STATE, gpb, False)
    rep_pt = replicate(SSM_STATE, gpb, True)
    rep_ht = replicate(SSM_GROUP, gpb, True)
    m_in = same_group(LANES, SSM_GROUP, STATE_BLOCK, SSM_STATE)
    m_out = same_group(STATE_BLOCK, SSM_STATE, LANES, SSM_GROUP)
    m_mix = same_group(LANES, SSM_GROUP, LANES, SSM_GROUP)
    nt = (((1,), (1,)), ((), ()))

    def rows_of(x, k):
        return x[k * gpb:(k + 1) * gpb].reshape(LANES, x.shape[-1]).astype(BF16)

    def bd_in(x, k):
        return jnp.where(m_in, jnp.dot(rows_of(x, k), rep_p, preferred_element_type=F32), 0.0)

    def bd_out(x, k):
        return jnp.where(m_out, lax.dot_general(rep_pt, rows_of(x, k), nt, preferred_element_type=F32), 0.0)

    def bd_mix(x, k):
        return jnp.where(m_mix, lax.dot_general(rep_ht, rows_of(x, k), nt, preferred_element_type=F32), 0.0)

    for k in range(CH_BLOCKS):
        one_step = jnp.concatenate([bd_in(bbr, k), bd_in(bbi, k)], axis=1)
        wb_ref[k] = one_step.astype(BF16)
        wcr_ref[k] = bd_out(cr, k).astype(BF16)
        wci_ref[k] = (-bd_out(ci, k)).astype(BF16)
        win_ref[k] = jnp.concatenate([jnp.concatenate([bd_in(abr, k), bd_in(abi, k)], axis=1), one_step],
                                     axis=0).astype(BF16)
        wor_ref[k] = jnp.concatenate([bd_out(car, k), bd_out(ca2r, k)], axis=1).astype(BF16)
        woi_ref[k] = (-jnp.concatenate([bd_out(cai, k), bd_out(ca2i, k)], axis=1)).astype(BF16)
        m0, m1 = bd_mix(k0, k), bd_mix(k1, k)
        wl_ref[k] = jnp.concatenate([jnp.concatenate([m0, m1], axis=1),
                                     jnp.concatenate([jnp.zeros_like(m0), m0], axis=1)], axis=0).astype(BF16)


def _ssm_weights(log_dt, lam_re, lam_im, b_re, b_im, c_re, c_im):
    g, p = N_SSM_GROUPS, SSM_STATE
    gp = jax.ShapeDtypeStruct((g, p), F32)
    stack = lambda rows, cols: jax.ShapeDtypeStruct((CH_BLOCKS, rows, cols), BF16)
    return pl.pallas_call(
        _ssm_weights_kernel,
        out_shape=(gp, gp,
                   stack(LANES, 2 * STATE_BLOCK), stack(STATE_BLOCK, LANES), stack(STATE_BLOCK, LANES),
                   stack(2 * LANES, 2 * STATE_BLOCK), stack(STATE_BLOCK, 2 * LANES), stack(STATE_BLOCK, 2 * LANES),
                   stack(2 * LANES, 2 * LANES)),
        compiler_params=pltpu.CompilerParams(vmem_limit_bytes=VMEM_LIMIT_BYTES),
        name="s5_weights",
    )(log_dt.reshape(g, 1), lam_re, lam_im, jnp.swapaxes(b_re, 1, 2), jnp.swapaxes(b_im, 1, 2), c_re, c_im)


SCAN_COLS = 512


def _s5_kernel(h_ref, g_ref, s0r_ref, s0i_ref, ar_ref, ai_ref, wb_ref, wcr_ref, wci_ref, d_ref,
               gw_ref, gb_ref, o_ref, sr_ref, si_ref, u_s, xr_s, xi_s, *, rb, tt):
    n_slab, per_slab = h_ref.shape[:2]
    assert n_slab * per_slab == rb * tt and (n_slab == rb or tt == 1)

    def seq_rows(s):
        return pl.ds(s, per_slab, stride=n_slab) if n_slab > 1 else pl.ds(0, per_slab)

    @pl.when(pl.program_id(0) == 0)
    def _():
        sr_ref[...] = s0r_ref[...]
        si_ref[...] = s0i_ref[...]

    for s in range(n_slab):
        un = _rms(h_ref[s], g_ref[...])
        for k in range(CH_BLOCKS):
            u_s[k, seq_rows(s), :] = un[:, k * LANES:(k + 1) * LANES]
    for k in range(CH_BLOCKS):
        bu = jnp.dot(u_s[k].astype(BF16), wb_ref[k], preferred_element_type=F32)
        xr_s[:, k * STATE_BLOCK:(k + 1) * STATE_BLOCK] = bu[:, :STATE_BLOCK]
        xi_s[:, k * STATE_BLOCK:(k + 1) * STATE_BLOCK] = bu[:, STATE_BLOCK:]

    for c in range(STATE_W // SCAN_COLS):
        cs = slice(c * SCAN_COLS, (c + 1) * SCAN_COLS)
        a_r = jnp.broadcast_to(ar_ref[:, cs], (rb, SCAN_COLS))
        a_i = jnp.broadcast_to(ai_ref[:, cs], (rb, SCAN_COLS))

        def step(t, carry, cs=cs, a_r=a_r, a_i=a_i):
            s_r, s_i = carry
            rows = pl.ds(pl.multiple_of(t * rb, rb), rb)
            n_r = a_r * s_r - a_i * s_i + xr_s[rows, cs]
            n_i = a_r * s_i + a_i * s_r + xi_s[rows, cs]
            xr_s[rows, cs] = n_r
            xi_s[rows, cs] = n_i
            return n_r, n_i

        s_r, s_i = lax.fori_loop(0, tt, step, (sr_ref[:, cs], si_ref[:, cs]), unroll=True)
        sr_ref[:, cs] = s_r
        si_ref[:, cs] = s_i

    ys = []
    for k in range(CH_BLOCKS):
        ks = slice(k * STATE_BLOCK, (k + 1) * STATE_BLOCK)
        ys.append(jnp.dot(xr_s[:, ks].astype(BF16), wcr_ref[k], preferred_element_type=F32)
                  + jnp.dot(xi_s[:, ks].astype(BF16), wci_ref[k], preferred_element_type=F32)
                  + d_ref[:, k * LANES:(k + 1) * LANES] * u_s[k])
    act = jax.nn.gelu(jnp.concatenate(ys, axis=1)).astype(BF16)
    z = jnp.dot(act, gw_ref[...], preferred_element_type=F32) + gb_ref[...]
    mix = z[:, :D_MODEL] * jax.nn.sigmoid(z[:, D_MODEL:])
    for k in range(CH_BLOCKS):
        u_s[k] = mix[:, k * LANES:(k + 1) * LANES]
    for s in range(n_slab):
        o_ref[s] = h_ref[s] + jnp.concatenate([u_s[k, seq_rows(s), :] for k in range(CH_BLOCKS)], axis=1)


def _s5(h, g, s0r, s0i, ar, ai, wb, wcr, wci, d_skip, glu_w, glu_b, *, layer, tt):
    rb, t_len, _ = h.shape
    assert t_len % tt == 0 and rb % 8 == 0
    rows = rb * tt
    block = (1, rb, D_MODEL) if t_len == 1 else (rb, tt, D_MODEL)
    h_in = h.reshape(1, rb, D_MODEL) if t_len == 1 else h
    state = jax.ShapeDtypeStruct((rb, STATE_W), F32)
    out, sr, si = pl.pallas_call(
        functools.partial(_s5_kernel, rb=rb, tt=tt),
        grid=(t_len // tt,),
        in_specs=[pl.BlockSpec(block, lambda i: (0, i, 0)),
                  _const_spec((1, D_MODEL)),
                  _const_spec((rb, STATE_W)), _const_spec((rb, STATE_W)),
                  _const_spec((1, STATE_W)), _const_spec((1, STATE_W)),
                  _const_spec((CH_BLOCKS, LANES, 2 * STATE_BLOCK)),
                  _const_spec((CH_BLOCKS, STATE_BLOCK, LANES)),
                  _const_spec((CH_BLOCKS, STATE_BLOCK, LANES)),
                  _const_spec((1, D_MODEL)),
                  _layer_spec((D_MODEL, 2 * D_MODEL), layer),
                  _const_spec((1, 2 * D_MODEL))],
        out_specs=(pl.BlockSpec(block, lambda i: (0, i, 0)),
                   pl.BlockSpec((rb, STATE_W), lambda i: (0, 0)),
                   pl.BlockSpec((rb, STATE_W), lambda i: (0, 0))),
        out_shape=(jax.ShapeDtypeStruct(h_in.shape, F32), state, state),
        scratch_shapes=[pltpu.VMEM((CH_BLOCKS, rows, LANES), F32),
                        pltpu.VMEM((rows, STATE_W), F32), pltpu.VMEM((rows, STATE_W), F32)],
        compiler_params=_params("arbitrary"),
        name="s5",
    )(h_in, g.reshape(1, D_MODEL), s0r, s0i, ar, ai, wb, wcr, wci, d_skip.reshape(1, D_MODEL),
      glu_w, glu_b.reshape(1, 2 * D_MODEL))
    return out.reshape(h.shape), sr, si


def _s5_pair_kernel(h_ref, g_ref, ar_ref, ai_ref, win_ref, wor_ref, woi_ref, wl_ref, d_ref,
                    gw_ref, gb_ref, o_ref, sr_ref, si_ref, u_s, xr_s, xi_s, *, rb, tt):
    pairs = tt // 2
    prow = pairs * rb

    def seq_rows(s):
        return pl.ds(s, tt, stride=rb)

    def split(x):
        x3 = x.reshape(pairs, 2 * rb, x.shape[-1])
        return x3[:, :rb].reshape(prow, x.shape[-1]), x3[:, rb:].reshape(prow, x.shape[-1])

    def interleave(even, odd):
        n = even.shape[-1]
        return jnp.concatenate([even.reshape(pairs, rb, n), odd.reshape(pairs, rb, n)], axis=1).reshape(tt * rb, n)

    @pl.when(pl.program_id(0) == 0)
    def _():
        sr_ref[...] = jnp.zeros_like(sr_ref)
        si_ref[...] = jnp.zeros_like(si_ref)

    for s in range(rb):
        un = _rms(h_ref[s], g_ref[...])
        for k in range(CH_BLOCKS):
            u_s[k, seq_rows(s), :] = un[:, k * LANES:(k + 1) * LANES]

    lhs = []
    for k in range(CH_BLOCKS):
        ks = slice(k * STATE_BLOCK, (k + 1) * STATE_BLOCK)
        lhs.append(jnp.concatenate(split(u_s[k]), axis=1).astype(BF16))
        w = jnp.dot(lhs[k], win_ref[k], preferred_element_type=F32)
        xr_s[:, ks] = w[:, :STATE_BLOCK]
        xi_s[:, ks] = w[:, STATE_BLOCK:]

    for c in range(STATE_W // SCAN_COLS):
        cs = slice(c * SCAN_COLS, (c + 1) * SCAN_COLS)
        a_r, a_i = ar_ref[:, cs], ai_ref[:, cs]
        a2_r = jnp.broadcast_to(a_r * a_r - a_i * a_i, (rb, SCAN_COLS))
        a2_i = jnp.broadcast_to(2.0 * a_r * a_i, (rb, SCAN_COLS))

        def step(j, carry, cs=cs, a2_r=a2_r, a2_i=a2_i):
            s_r, s_i = carry
            rows = pl.ds(pl.multiple_of(j * rb, rb), rb)
            n_r = a2_r * s_r - a2_i * s_i + xr_s[rows, cs]
            n_i = a2_r * s_i + a2_i * s_r + xi_s[rows, cs]
            xr_s[rows, cs] = s_r
            xi_s[rows, cs] = s_i
            return n_r, n_i

        s_r, s_i = lax.fori_loop(0, pairs, step, (sr_ref[:, cs], si_ref[:, cs]), unroll=True)
        sr_ref[:, cs] = s_r
        si_ref[:, cs] = s_i

    ys = []
    for k in range(CH_BLOCKS):
        ks = slice(k * STATE_BLOCK, (k + 1) * STATE_BLOCK)
        y2 = (jnp.dot(xr_s[:, ks].astype(BF16), wor_ref[k], preferred_element_type=F32)
              + jnp.dot(xi_s[:, ks].astype(BF16), woi_ref[k], preferred_element_type=F32)
              + jnp.dot(lhs[k], wl_ref[k], preferred_element_type=F32))
        ys.append(interleave(y2[:, :LANES], y2[:, LANES:]) + d_ref[:, k * LANES:(k + 1) * LANES] * u_s[k])
    act = jax.nn.gelu(jnp.concatenate(ys, axis=1)).astype(BF16)
    z = jnp.dot(act, gw_ref[...], preferred_element_type=F32) + gb_ref[...]
    mix = z[:, :D_MODEL] * jax.nn.sigmoid(z[:, D_MODEL:])
    for k in range(CH_BLOCKS):
        u_s[k] = mix[:, k * LANES:(k + 1) * LANES]
    for s in range(rb):
        o_ref[s] = h_ref[s] + jnp.concatenate([u_s[k, seq_rows(s), :] for k in range(CH_BLOCKS)], axis=1)


def _s5_pairs(h, g, ar, ai, w_in, w_out_r, w_out_i, w_loc, d_skip, glu_w, glu_b, *, layer, tt):
    rb, t_len, _ = h.shape
    assert t_len % tt == 0 and tt % 2 == 0 and rb % 8 == 0
    rows, prow = rb * tt, rb * tt // 2
    block = (rb, tt, D_MODEL)
    state = jax.ShapeDtypeStruct((rb, STATE_W), F32)
    return pl.pallas_call(
        functools.partial(_s5_pair_kernel, rb=rb, tt=tt),
        grid=(t_len // tt,),
        in_specs=[pl.BlockSpec(block, lambda i: (0, i, 0)),
                  _const_spec((1, D_MODEL)),
                  _const_spec((1, STATE_W)), _const_spec((1, STATE_W)),
                  _const_spec((CH_BLOCKS, 2 * LANES, 2 * STATE_BLOCK)),
                  _const_spec((CH_BLOCKS, STATE_BLOCK, 2 * LANES)),
                  _const_spec((CH_BLOCKS, STATE_BLOCK, 2 * LANES)),
                  _const_spec((CH_BLOCKS, 2 * LANES, 2 * LANES)),
                  _const_spec((1, D_MODEL)),
                  _layer_spec((D_MODEL, 2 * D_MODEL), layer),
                  _const_spec((1, 2 * D_MODEL))],
        out_specs=(pl.BlockSpec(block, lambda i: (0, i, 0)),
                   pl.BlockSpec((rb, STATE_W), lambda i: (0, 0)),
                   pl.BlockSpec((rb, STATE_W), lambda i: (0, 0))),
        out_shape=(jax.ShapeDtypeStruct(h.shape, F32), state, state),
        scratch_shapes=[pltpu.VMEM((CH_BLOCKS, rows, LANES), F32),
                        pltpu.VMEM((prow, STATE_W), F32), pltpu.VMEM((prow, STATE_W), F32)],
        compiler_params=_params("arbitrary"),
        name="s5_pairs",
    )(h, g.reshape(1, D_MODEL), ar, ai, w_in, w_out_r, w_out_i, w_loc, d_skip.reshape(1, D_MODEL),
      glu_w, glu_b.reshape(1, 2 * D_MODEL))


def _proj_kernel(x_ref, g_ref, w_ref, o_ref, *, scale):
    xn = _rms(x_ref[...], g_ref[...]).astype(BF16)
    y = jnp.dot(xn, w_ref[...], preferred_element_type=F32)
    o_ref[...] = y if scale == 1.0 else y * scale


def _proj(x, g, w, layer=None, scale=1.0, tm=512):
    n = x.shape[0]
    tm = min(tm, n)
    nout = w.shape[-1]
    return pl.pallas_call(
        functools.partial(_proj_kernel, scale=scale),
        grid=(n // tm,),
        in_specs=[pl.BlockSpec((tm, D_MODEL), lambda i: (i, 0)),
                  _const_spec((1, D_MODEL)),
                  _layer_spec((D_MODEL, nout), layer)],
        out_specs=pl.BlockSpec((tm, nout), lambda i: (i, 0)),
        out_shape=jax.ShapeDtypeStruct((n, nout), F32),
        compiler_params=_params("parallel"),
        name="proj",
    )(x, g.reshape(1, D_MODEL), w)


def _kv_tails(t_len, tm):
    tails = []
    for w in DIL_WINDOWS:
        n_tail = min(w, t_len)
        if n_tail >= tm:
            assert n_tail % tm == 0
            tails.append(((t_len - n_tail) // tm, 0, tm))
        else:
            tails.append((t_len // tm - 1, tm - n_tail, n_tail))
    return tails


def _kv_kernel(x_ref, g_ref, w_ref, kv_ref, *tail_refs, tails):
    j = pl.program_id(1)
    xn = _rms(x_ref[...], g_ref[...]).astype(BF16)
    kv = jnp.dot(xn, w_ref[...], preferred_element_type=F32)
    kv_ref[...] = kv
    for g, (first_tile, row0, n_tok) in enumerate(tails):
        def emit(g=g, row0=row0, n_tok=n_tok):
            tail_refs[g][...] = kv[row0:row0 + n_tok, g * 2 * ATTN_WIDTH:(g + 1) * 2 * ATTN_WIDTH].T

        if first_tile == 0:
            emit()
        else:
            pl.when(j >= first_tile)(emit)


def _kv_proj(x, g, w, tm=512):
    bn, t_len, _ = x.shape
    tm = min(tm, t_len)
    assert t_len % tm == 0
    nout = w.shape[-1]
    tails = _kv_tails(t_len, tm)
    tail_specs, tail_shapes = [], []
    for (first_tile, _, n_tok), win in zip(tails, DIL_WINDOWS):
        tail_specs.append(pl.BlockSpec((None, 2 * ATTN_WIDTH, n_tok),
                                       lambda b, j, f=first_tile: (b, 0, jnp.maximum(j - f, 0))))
        tail_shapes.append(jax.ShapeDtypeStruct((bn, 2 * ATTN_WIDTH, min(win, t_len)), F32))
    return pl.pallas_call(
        functools.partial(_kv_kernel, tails=tails),
        grid=(bn, t_len // tm),
        in_specs=[pl.BlockSpec((None, tm, D_MODEL), lambda b, j: (b, j, 0)),
                  _const_spec((1, D_MODEL)),
                  _const_spec((D_MODEL, nout))],
        out_specs=[pl.BlockSpec((None, tm, nout), lambda b, j: (b, j, 0))] + tail_specs,
        out_shape=[jax.ShapeDtypeStruct((bn, t_len, nout), F32)] + tail_shapes,
        compiler_params=_params("parallel", "arbitrary"),
        name="kv_proj",
    )(x, g.reshape(1, D_MODEL), w)


def _oproj_kernel(a_ref, w_ref, h_ref, o_ref):
    o_ref[...] = h_ref[...] + jnp.dot(a_ref[...].astype(BF16), w_ref[...], preferred_element_type=F32)


def _oproj(a, w, layer, h, tm=512):
    n = h.shape[0]
    tm = min(tm, n)
    return pl.pallas_call(
        _oproj_kernel,
        grid=(n // tm,),
        in_specs=[pl.BlockSpec((tm, ATTN_WIDTH), lambda i: (i, 0)),
                  _layer_spec((ATTN_WIDTH, D_MODEL), layer),
                  pl.BlockSpec((tm, D_MODEL), lambda i: (i, 0))],
        out_specs=pl.BlockSpec((tm, D_MODEL), lambda i: (i, 0)),
        out_shape=jax.ShapeDtypeStruct((n, D_MODEL), F32),
        compiler_params=_params("parallel"),
        name="oproj",
    )(a, w, h)


def _alibi_slopes():
    n = N_DIL * HEADS_PER_GROUP
    s = jnp.exp2(-8.0 * jnp.arange(1, n + 1, dtype=F32) / n)
    return s.reshape(N_DIL, HEADS_PER_GROUP)


def _prompt_bias():
    qi = jnp.arange(R_LEN)[:, None]
    kj = jnp.arange(2 * R_LEN)[None, :]
    diff = R_LEN + qi - kj
    valid = (diff >= 0) & (diff <= R_LEN)
    dil = jnp.asarray(DIL_RATES, jnp.int32)[:, None, None]
    dist = (dil * diff[None]).astype(F32)
    bias = -_alibi_slopes()[:, :, None, None] * dist[:, None]
    return jnp.where(valid[None, None], bias * LOG2_E, NEG)


def _sample_bias():
    kk = R_LEN - jnp.arange(R_LEN)
    dil = jnp.asarray(DIL_RATES, jnp.int32)[:, None]
    dist = (dil * kk[None]).astype(F32)
    return -_alibi_slopes()[:, :, None] * dist[:, None]


MERGE_ROWS = 256
INTERLEAVE = 16


def _attn_kernel(q0, q1, q2, k0, k1, k2, v0, v1, v2, bias_ref, o_ref, acc_s, m_s, l_s, *, t_len):
    qs, ks, vs = (q0, q1, q2), (k0, k1, k2), (v0, v1, v2)
    lo = lax.broadcasted_iota(jnp.int32, (R_LEN, LANES), 1) < HEAD_DIM

    def block(g, q_idx, k_idx, first):
        q = qs[g][q_idx, :]
        qb = jnp.concatenate([jnp.where(lo, q, 0.0), jnp.where(lo, 0.0, q)], axis=0).astype(BF16)
        kb = ks[g][k_idx, :].astype(BF16)
        vb = vs[g][k_idx, :].astype(BF16)
        s = lax.dot_general(qb, kb, (((1,), (1,)), ((), ())), preferred_element_type=F32)
        bias = bias_ref[g, :, :, R_LEN:] if first else bias_ref[g]
        bias = bias.reshape(2 * R_LEN, bias.shape[-1])
        s = jnp.where(bias > 0.5 * NEG, s + bias, NEG)
        m = jnp.max(s, axis=-1, keepdims=True)
        p = jnp.exp2(s - m)
        l = jnp.sum(p, axis=-1, keepdims=True)
        a = jnp.dot(p.astype(BF16), vb, preferred_element_type=F32)
        acc_s[g, q_idx, :] = jnp.where(lo, a[:R_LEN], a[R_LEN:])
        m_s[g, q_idx, :] = jnp.where(lo, m[:R_LEN], m[R_LEN:])
        l_s[g, q_idx, :] = jnp.where(lo, l[:R_LEN], l[R_LEN:])

    for g in range(N_DIL):
        dil = DIL_RATES[g]
        span = R_LEN * dil
        n_blocks = t_len // span

        def rows(start, n, dil=dil):
            if dil > 1:
                return pl.ds(start, n, stride=dil)
            return pl.ds(start if isinstance(start, int) else pl.multiple_of(start, R_LEN), n)

        if dil <= INTERLEAVE:
            for r in range(dil):
                block(g, rows(r, R_LEN), rows(r, R_LEN), True)
        else:
            def first_blocks(i, c, g=g, rows=rows):
                for j in range(INTERLEAVE):
                    r = i * INTERLEAVE + j
                    block(g, rows(r, R_LEN), rows(r, R_LEN), True)
                return c

            lax.fori_loop(0, dil // INTERLEAVE, first_blocks, 0)

        later = n_blocks - 1
        if later == 0:
            continue
        per_trip = max(w for w in range(1, max(INTERLEAVE // dil, 1) + 1) if later % w == 0)

        def later_blocks(i, c, g=g, dil=dil, span=span, rows=rows, per_trip=per_trip):
            for j in range(per_trip):
                nb = 1 + i * per_trip + j
                for r in range(dil):
                    start = nb * span + r
                    block(g, rows(start, R_LEN), rows(start - span, 2 * R_LEN), False)
            return c

        if later == per_trip:
            later_blocks(0, 0)
        else:
            lax.fori_loop(0, later // per_trip, later_blocks, 0)

    def merge(i, c):
        rows = pl.ds(pl.multiple_of(i * MERGE_ROWS, MERGE_ROWS), MERGE_ROWS)
        ms = [m_s[g, rows, :] for g in range(N_DIL)]
        mx = jnp.maximum(jnp.maximum(ms[0], ms[1]), ms[2])
        ws = [jnp.exp2(m - mx) for m in ms]
        den = ws[0] * l_s[0, rows, :] + ws[1] * l_s[1, rows, :] + ws[2] * l_s[2, rows, :]
        num = ws[0] * acc_s[0, rows, :] + ws[1] * acc_s[1, rows, :] + ws[2] * acc_s[2, rows, :]
        o_ref[rows, :] = num / den
        return c

    lax.fori_loop(0, t_len // MERGE_ROWS, merge, 0)


def _attn_prompt(q, kv, bias):
    bn, t_len, _ = q.shape
    pairs = ATTN_WIDTH // LANES
    per_group = 2 * ATTN_WIDTH // LANES

    def col(off):
        return pl.BlockSpec((None, t_len, LANES), lambda b, hp: (b, 0, off + hp))

    in_specs = ([col(g * pairs) for g in range(N_DIL)]
                + [col(g * per_group) for g in range(N_DIL)]
                + [col(g * per_group + pairs) for g in range(N_DIL)]
                + [pl.BlockSpec((N_DIL, 2, R_LEN, 2 * R_LEN), lambda b, hp: (0, hp, 0, 0))])
    scratch = pltpu.VMEM((N_DIL, t_len, LANES), F32)
    return pl.pallas_call(
        functools.partial(_attn_kernel, t_len=t_len),
        grid=(bn, pairs),
        in_specs=in_specs,
        out_specs=pl.BlockSpec((None, t_len, LANES), lambda b, hp: (b, 0, hp)),
        out_shape=jax.ShapeDtypeStruct((bn, t_len, ATTN_WIDTH), F32),
        scratch_shapes=[scratch, scratch, scratch],
        compiler_params=_params("parallel", "parallel"),
        name="attn_prompt",
    )(q, q, q, kv, kv, kv, kv, kv, kv, bias)


SAMPLE_BLOCK = 16


def _window_kernel(c_ref, o_ref, *, dil):
    w = c_ref.shape[-1]
    pos = lax.broadcasted_iota(jnp.int32, (w, R_LEN), 0)
    col = lax.broadcasted_iota(jnp.int32, (w, R_LEN), 1)
    pick = jnp.where(pos == col * dil, 1.0, 0.0).astype(BF16)
    o_ref[...] = jnp.dot(c_ref[...].astype(BF16), pick, preferred_element_type=F32).astype(BF16)


def _strided_window(cache, dil):
    bd, w = cache.shape[:2]
    assert w == R_LEN * dil
    slab = jnp.transpose(cache, (0, 2, 3, 4, 1)).reshape(bd, 2 * ATTN_WIDTH, w)
    if dil == 1:
        return slab
    return pl.pallas_call(
        functools.partial(_window_kernel, dil=dil),
        grid=(bd,),
        in_specs=[pl.BlockSpec((None, 2 * ATTN_WIDTH, w), lambda i: (i, 0, 0))],
        out_specs=pl.BlockSpec((None, 2 * ATTN_WIDTH, R_LEN), lambda i: (i, 0, 0)),
        out_shape=jax.ShapeDtypeStruct((bd, 2 * ATTN_WIDTH, R_LEN), BF16),
        compiler_params=_params("parallel"),
        name="cache_window",
    )(slab)


def _attn_sample_kernel(q_ref, kvn_ref, c0, c1, c2, bias_ref, o_ref):
    windows = (c0, c1, c2)
    sub = lax.broadcasted_iota(jnp.int32, (HEADS_PER_GROUP, ATTN_WIDTH), 0)
    lane = lax.broadcasted_iota(jnp.int32, (HEADS_PER_GROUP, ATTN_WIDTH), 1)
    own = (lane // HEAD_DIM) == sub
    for b in range(SAMPLE_BLOCK):
        os_, ms, ls = [], [], []
        for g in range(N_DIL):
            base = g * 2 * ATTN_WIDTH
            qbd = jnp.where(own, q_ref[b:b + 1, g * ATTN_WIDTH:(g + 1) * ATTN_WIDTH], 0.0)
            kt = windows[g][b, :ATTN_WIDTH, :].astype(BF16)
            vt = windows[g][b, ATTN_WIDTH:, :].astype(BF16)
            kn = kvn_ref[b:b + 1, base:base + ATTN_WIDTH]
            vn = kvn_ref[b:b + 1, base + ATTN_WIDTH:base + 2 * ATTN_WIDTH]
            s = jnp.dot(qbd.astype(BF16), kt, preferred_element_type=F32) + bias_ref[g]
            s_new = jnp.sum(qbd * kn, axis=-1, keepdims=True)
            m = jnp.maximum(jnp.max(s, axis=-1, keepdims=True), s_new)
            p = jnp.exp(s - m)
            p_new = jnp.exp(s_new - m)
            ls.append(jnp.sum(p, axis=-1, keepdims=True) + p_new)
            os_.append(lax.dot_general(p.astype(BF16), vt, (((1,), (1,)), ((), ())),
                                       preferred_element_type=F32) + p_new * vn)
            ms.append(m)
        mx = jnp.maximum(jnp.maximum(ms[0], ms[1]), ms[2])
        ws = [jnp.exp(m - mx) for m in ms]
        den = ws[0] * ls[0] + ws[1] * ls[1] + ws[2] * ls[2]
        o = (ws[0] * os_[0] + ws[1] * os_[1] + ws[2] * os_[2]) / den
        o_ref[b:b + 1, :] = jnp.sum(jnp.where(own, o, 0.0), axis=0, keepdims=True)


def _attn_sample(q, kv_new, windows, bias):
    bd = q.shape[0]
    assert bd % SAMPLE_BLOCK == 0
    window_spec = pl.BlockSpec((SAMPLE_BLOCK, 2 * ATTN_WIDTH, R_LEN), lambda i: (i, 0, 0))
    return pl.pallas_call(
        _attn_sample_kernel,
        grid=(bd // SAMPLE_BLOCK,),
        in_specs=[pl.BlockSpec((SAMPLE_BLOCK, N_DIL * ATTN_WIDTH), lambda i: (i, 0)),
                  pl.BlockSpec((SAMPLE_BLOCK, N_DIL * 2 * ATTN_WIDTH), lambda i: (i, 0))]
                 + [window_spec] * N_DIL + [_const_spec((N_DIL, HEADS_PER_GROUP, R_LEN))],
        out_specs=pl.BlockSpec((SAMPLE_BLOCK, ATTN_WIDTH), lambda i: (i, 0)),
        out_shape=jax.ShapeDtypeStruct((bd, ATTN_WIDTH), F32),
        compiler_params=_params("parallel"),
        name="attn_sample",
    )(q, kv_new, *windows, bias)


def kernel(x_prompt, x_sample, state_ssm_re, state_ssm_im, cache_kv_w128, cache_kv_w512, cache_kv_w2048, norm_ffn1, ffn1_w_in, ffn1_w_out, norm_mix, norm_ffn2, ffn2_w_in, ffn2_w_out, norm_kv, norm_final, ssm_log_dt, ssm_lambda_re, ssm_lambda_im, ssm_b_re, ssm_b_im, ssm_c_re, ssm_c_im, ssm_d, glu_w, glu_b, attn_w_q, attn_w_kv, attn_w_o):
    bn, t_len, _ = x_prompt.shape
    bd = x_sample.shape[0]
    assert x_sample.shape[1] == 1

    glu_wb = glu_w.astype(BF16)
    w_q, w_kv, w_o = attn_w_q.astype(BF16), attn_w_kv.astype(BF16), attn_w_o.astype(BF16)
    q_scale = HEAD_DIM ** -0.5

    ssm, ssm_pairs = [], []
    for layer in range(N_A_LAYERS):
        ar, ai, wb, wcr, wci, *pair_w = _ssm_weights(
            ssm_log_dt[layer], ssm_lambda_re[layer], ssm_lambda_im[layer],
            ssm_b_re[layer], ssm_b_im[layer], ssm_c_re[layer], ssm_c_im[layer])
        ar, ai = ar.reshape(1, STATE_W), ai.reshape(1, STATE_W)
        tail = (ssm_d[layer].reshape(D_MODEL), glu_wb, glu_b[layer])
        ssm.append((ar, ai, wb, wcr, wci) + tail)
        ssm_pairs.append((ar, ai, *pair_w) + tail)

    def states(parts, n_seq):
        return jnp.stack([p.reshape(n_seq, N_SSM_GROUPS, SSM_STATE) for p in parts])

    s0r = state_ssm_re.reshape(N_A_LAYERS, bd, STATE_W)
    s0i = state_ssm_im.reshape(N_A_LAYERS, bd, STATE_W)
    windows = [_strided_window(cache, dil)
               for cache, dil in zip((cache_kv_w128, cache_kv_w512, cache_kv_w2048), DIL_RATES)]
    sample_bias, prompt_bias = _sample_bias(), _prompt_bias()
    fin_r, fin_i, fin_rs, fin_is = [], [], [], []
    h = x_prompt.reshape(bn * t_len, D_MODEL)
    hs = x_sample.reshape(bd, D_MODEL)
    hs, w1 = _ffn_cast(hs, norm_ffn1[0], ffn1_w_in, ffn1_w_out, 0)
    for layer in range(DEPTH):
        last = layer == DEPTH - 1
        if layer == N_A_LAYERS:
            kv_p, *kv_tails = _kv_proj(h.reshape(bn, t_len, D_MODEL), norm_kv, w_kv)
            kv_s = _proj(hs, norm_kv, w_kv)
        ffn1 = _ffn(h, norm_ffn1[layer], w1, cast_next=(ffn2_w_in, ffn2_w_out, layer),
                    q_proj=((norm_mix[layer], w_q, layer - N_A_LAYERS, q_scale * LOG2_E)
                            if layer >= N_A_LAYERS else None),
                    extra=hs if layer > 0 else None)
        if layer > 0:
            hs = ffn1[-1]
        h, w2 = ffn1[0], ffn1[2 if layer >= N_A_LAYERS else 1]
        if layer < N_A_LAYERS:
            h, sr, si = _s5_pairs(h.reshape(bn, t_len, D_MODEL), norm_mix[layer],
                                  *ssm_pairs[layer], layer=layer, tt=64)
            h = h.reshape(bn * t_len, D_MODEL)
            fin_r.append(sr)
            fin_i.append(si)
            hs, sr, si = _s5(hs.reshape(bd, 1, D_MODEL), norm_mix[layer], s0r[layer], s0i[layer],
                             *ssm[layer], layer=layer, tt=1)
            hs = hs.reshape(bd, D_MODEL)
            fin_rs.append(sr)
            fin_is.append(si)
            attn = None
        else:
            b = layer - N_A_LAYERS
            merged = _attn_prompt(ffn1[1].reshape(bn, t_len, -1), kv_p, prompt_bias)
            attn = (merged.reshape(bn * t_len, ATTN_WIDTH), w_o, b)
            qs = _proj(hs, norm_mix[layer], w_q, layer=b, scale=q_scale)
            hs = _oproj(_attn_sample(qs, kv_s, windows, sample_bias), w_o, b, hs)
        if last:
            h, hs = _ffn(h, norm_ffn2[layer], w2, attn=attn, final_g=norm_final, extra=hs)
        else:
            h, w1, hs = _ffn(h, norm_ffn2[layer], w2, attn=attn, extra=hs,
                             cast_next=(ffn1_w_in, ffn1_w_out, layer + 1))
    y_prompt = h.reshape(bn, t_len, D_MODEL)
    y_sample = hs.reshape(bd, 1, D_MODEL)
    ssm_re_p, ssm_im_p = states(fin_r, bn), states(fin_i, bn)
    ssm_re_s, ssm_im_s = states(fin_rs, bd), states(fin_is, bd)
    kv_s = kv_s.reshape(bd, 1, N_DIL, 2, HEADS_PER_GROUP, HEAD_DIM)
    kv_out_p = [jnp.transpose(tail.reshape(bn, 2, HEADS_PER_GROUP, HEAD_DIM, -1), (0, 4, 1, 2, 3))
                for tail in kv_tails]

    return (y_prompt, y_sample, ssm_re_p, ssm_im_p, kv_out_p[0], kv_out_p[1], kv_out_p[2],
            ssm_re_s, ssm_im_s, kv_s[:, :, 0], kv_s[:, :, 1], kv_s[:, :, 2])
```

```python
import functools

import jax
import jax.numpy as jnp
from jax import lax
from jax.experimental import pallas as pl
from jax.experimental.pallas import tpu as pltpu

F32 = jnp.float32
BF16 = jnp.bfloat16

D_MODEL = 1024
DEPTH = 4
N_A_LAYERS = DEPTH // 2
SSM_GROUP = 16
N_SSM_GROUPS = D_MODEL // SSM_GROUP
SSM_STATE = 64
STATE_W = N_SSM_GROUPS * SSM_STATE
DIL_WINDOWS = (128, 512, 2048)
DIL_RATES = (1, 4, 16)
N_DIL = 3
HEADS_PER_GROUP = 8
HEAD_DIM = 64
ATTN_WIDTH = HEADS_PER_GROUP * HEAD_DIM
R_LEN = DIL_WINDOWS[0] // DIL_RATES[0]
D_FF = 2816
EPS = 1e-6
NEG = -1e30
LOG2_E = 1.4426950408889634

LANES = 128
MXU_TILE = 256
BF16_SUBLANES = 16
CH_BLOCKS = D_MODEL // LANES
GROUPS_PER_BLOCK = LANES // SSM_GROUP
STATE_BLOCK = GROUPS_PER_BLOCK * SSM_STATE
VMEM_LIMIT_BYTES = 56 * 1024 * 1024

assert all(w // d == R_LEN for w, d in zip(DIL_WINDOWS, DIL_RATES))


def _params(*semantics):
    return pltpu.CompilerParams(dimension_semantics=semantics, vmem_limit_bytes=VMEM_LIMIT_BYTES)


def _const_spec(shape):
    zeros = (0,) * len(shape)
    return pl.BlockSpec(shape, lambda *_: zeros, pipeline_mode=pl.Buffered(1))


def _layer_spec(shape, layer):
    if layer is None:
        return _const_spec(shape)
    idx = (layer,) + (0,) * len(shape)
    return pl.BlockSpec((None,) + tuple(shape), lambda *_: idx, pipeline_mode=pl.Buffered(1))


def _rms(x, g):
    return x * lax.rsqrt(jnp.mean(x * x, axis=-1, keepdims=True) + EPS) * g


FFN_SUB_ROWS = 512


def _ff_chunks(n_chunks):
    tiles = D_FF // MXU_TILE
    assert tiles * MXU_TILE == D_FF and n_chunks <= tiles
    cuts = [MXU_TILE * ((tiles * c + n_chunks - 1) // n_chunks) for c in range(n_chunks + 1)]
    return list(zip(cuts[:-1], cuts[1:]))


def _swiglu_residual(x, g, wa_ref, wb_ref, wout_ref, n_chunks):
    xn = _rms(x, g).astype(BF16)
    acc = None
    for lo, hi in _ff_chunks(n_chunks):
        a = jnp.dot(xn, wa_ref[:, lo:hi], preferred_element_type=F32)
        b = jnp.dot(xn, wb_ref[:, lo:hi], preferred_element_type=F32)
        act = (a * jax.nn.sigmoid(a) * b).astype(BF16)
        d = jnp.dot(act, wout_ref[lo:hi, :], preferred_element_type=F32)
        acc = d if acc is None else acc + d
    return x + 0.5 * acc


def _ffn_kernel(*refs, n_chunks, attn_in, final, q_scale, cast_next, extra_steps):
    refs = list(refs)
    x_ref = refs.pop(0)
    if attn_in:
        a_ref, wo_ref = refs.pop(0), refs.pop(0)
    g_ref, wa_ref, wb_ref, wout_ref = (refs.pop(0) for _ in range(4))
    if final:
        gf_ref = refs.pop(0)
    if q_scale is not None:
        gq_ref, wq_ref = refs.pop(0), refs.pop(0)
    if cast_next:
        next_f32 = [refs.pop(0) for _ in range(3)]
    if extra_steps:
        xs_ref = refs.pop(0)
    o_ref = refs.pop(0)
    if q_scale is not None:
        q_ref = refs.pop(0)
    if cast_next:
        for src in next_f32:
            refs.pop(0)[...] = src[...].astype(BF16)

    def main():
        tm = x_ref.shape[0]
        for r0 in range(0, tm, min(tm, FFN_SUB_ROWS)):
            rows = slice(r0, r0 + min(tm, FFN_SUB_ROWS))
            x = x_ref[rows, :]
            if attn_in:
                x = x + jnp.dot(a_ref[rows, :].astype(BF16), wo_ref[...], preferred_element_type=F32)
            y = _swiglu_residual(x, g_ref[...], wa_ref, wb_ref, wout_ref, n_chunks)
            if final:
                y = _rms(y, gf_ref[...])
            o_ref[rows, :] = y
            if q_scale is not None:
                q = jnp.dot(_rms(y, gq_ref[...]).astype(BF16), wq_ref[...], preferred_element_type=F32)
                q_ref[rows, :] = q * q_scale

    if not extra_steps:
        main()
        return
    ys_ref = refs.pop(0)
    pl.when(pl.program_id(0) < extra_steps)(main)

    @pl.when(pl.program_id(0) == extra_steps)
    def _():
        ys = _swiglu_residual(xs_ref[...], g_ref[...], wa_ref, wb_ref, wout_ref, n_chunks)
        ys_ref[...] = _rms(ys, gf_ref[...]) if final else ys


def _ffn(x, g, weights, attn=None, final_g=None, q_proj=None, cast_next=None, extra=None, n_chunks=2):
    n = x.shape[0]
    tm = min(FFN_SUB_ROWS * (1 if q_proj is not None else 2), n)
    assert n % tm == 0
    steps = n // tm
    step = (lambda i: jnp.minimum(i, steps - 1)) if extra is not None else (lambda i: i)
    rows = lambda width: pl.BlockSpec((tm, width), lambda i: (step(i), 0))
    in_specs, args = [rows(D_MODEL)], [x]
    if attn is not None:
        merged, w_o, idx = attn
        in_specs += [rows(ATTN_WIDTH), _layer_spec((ATTN_WIDTH, D_MODEL), idx)]
        args += [merged, w_o]
    in_specs += [_const_spec((1, D_MODEL)), _const_spec((D_MODEL, D_FF)), _const_spec((D_MODEL, D_FF)),
                 _const_spec((D_FF, D_MODEL))]
    args += [g.reshape(1, D_MODEL), *weights]
    if final_g is not None:
        in_specs.append(_const_spec((1, D_MODEL)))
        args.append(final_g.reshape(1, D_MODEL))
    out_specs, out_shape = [rows(D_MODEL)], [jax.ShapeDtypeStruct((n, D_MODEL), F32)]
    q_scale = None
    if q_proj is not None:
        gq, w_q, idx, q_scale = q_proj
        nq = w_q.shape[-1]
        in_specs += [_const_spec((1, D_MODEL)), _layer_spec((D_MODEL, nq), idx)]
        args += [gq.reshape(1, D_MODEL), w_q]
        out_specs.append(rows(nq))
        out_shape.append(jax.ShapeDtypeStruct((n, nq), F32))
    if cast_next is not None:
        w_in, w_out, idx = cast_next
        r_in = D_MODEL // steps
        out_blocks = [b for b in range(1, steps + 1) if steps % b == 0 and D_FF % b == 0
                      and (D_FF // b) % BF16_SUBLANES == 0][-1]
        r_out, hold = D_FF // out_blocks, steps // out_blocks
        assert r_in * steps == D_MODEL and r_in % BF16_SUBLANES == 0
        in_specs += [pl.BlockSpec((None, r_in, D_FF), lambda i: (idx, step(i), 0)),
                     pl.BlockSpec((None, r_in, D_FF), lambda i: (idx, step(i), 1)),
                     pl.BlockSpec((None, r_out, D_MODEL), lambda i: (idx, step(i) // hold, 0))]
        args += [w_in, w_in, w_out]
        out_specs += [pl.BlockSpec((r_in, D_FF), lambda i: (step(i), 0)),
                      pl.BlockSpec((r_in, D_FF), lambda i: (step(i), 0)),
                      pl.BlockSpec((r_out, D_MODEL), lambda i: (step(i) // hold, 0))]
        out_shape += [jax.ShapeDtypeStruct((D_MODEL, D_FF), BF16), jax.ShapeDtypeStruct((D_MODEL, D_FF), BF16),
                      jax.ShapeDtypeStruct((D_FF, D_MODEL), BF16)]
    if extra is not None:
        in_specs.append(_const_spec(extra.shape))
        args.append(extra)
        out_specs.append(pl.BlockSpec(extra.shape, lambda i: (0, 0)))
        out_shape.append(jax.ShapeDtypeStruct(extra.shape, F32))
    out = pl.pallas_call(
        functools.partial(_ffn_kernel, n_chunks=n_chunks, attn_in=attn is not None,
                          final=final_g is not None, q_scale=q_scale, cast_next=cast_next is not None,
                          extra_steps=steps if extra is not None else 0),
        grid=(steps + (extra is not None),),
        in_specs=in_specs,
        out_specs=out_specs,
        out_shape=out_shape,
        compiler_params=_params("arbitrary"),
        name="ffn",
    )(*args)
    if len(out) == 1:
        return out[0]
    parts = [out[0]]
    k = 1
    if q_proj is not None:
        parts.append(out[k])
        k += 1
    if cast_next is not None:
        parts.append(tuple(out[k:k + 3]))
        k += 3
    if extra is not None:
        parts.append(out[k])
    return tuple(parts)


def _ffn_cast_kernel(x_ref, g_ref, wa_ref, wb_ref, wout_ref, o_ref, wa_o, wb_o, wout_o, acc_s):
    c = pl.program_id(0)
    wa = wa_ref[...].astype(BF16)
    wb = wb_ref[...].astype(BF16)
    wout = wout_ref[...].astype(BF16)
    wa_o[...] = wa
    wb_o[...] = wb
    wout_o[...] = wout
    x = x_ref[...]
    xn = _rms(x, g_ref[...]).astype(BF16)
    a = jnp.dot(xn, wa, preferred_element_type=F32)
    b = jnp.dot(xn, wb, preferred_element_type=F32)
    d = jnp.dot((a * jax.nn.sigmoid(a) * b).astype(BF16), wout, preferred_element_type=F32)

    @pl.when(c == 0)
    def _():
        acc_s[...] = d

    @pl.when(c > 0)
    def _():
        acc_s[...] += d

    @pl.when(c == pl.num_programs(0) - 1)
    def _():
        o_ref[...] = x + 0.5 * acc_s[...]


def _ffn_cast(x, g, w_in, w_out, layer):
    n = x.shape[0]
    n_chunks = D_FF // MXU_TILE
    y, wa, wb, wo = pl.pallas_call(
        _ffn_cast_kernel,
        grid=(n_chunks,),
        in_specs=[_const_spec((n, D_MODEL)), _const_spec((1, D_MODEL)),
                  pl.BlockSpec((None, D_MODEL, MXU_TILE), lambda c: (layer, 0, c)),
                  pl.BlockSpec((None, D_MODEL, MXU_TILE), lambda c: (layer, 0, n_chunks + c)),
                  pl.BlockSpec((None, MXU_TILE, D_MODEL), lambda c: (layer, c, 0))],
        out_specs=[pl.BlockSpec((n, D_MODEL), lambda c: (0, 0)),
                   pl.BlockSpec((D_MODEL, MXU_TILE), lambda c: (0, c)),
                   pl.BlockSpec((D_MODEL, MXU_TILE), lambda c: (0, c)),
                   pl.BlockSpec((MXU_TILE, D_MODEL), lambda c: (c, 0))],
        out_shape=[jax.ShapeDtypeStruct((n, D_MODEL), F32),
                   jax.ShapeDtypeStruct((D_MODEL, D_FF), BF16),
                   jax.ShapeDtypeStruct((D_MODEL, D_FF), BF16),
                   jax.ShapeDtypeStruct((D_FF, D_MODEL), BF16)],
        scratch_shapes=[pltpu.VMEM((n, D_MODEL), F32)],
        compiler_params=_params("arbitrary"),
        name="ffn_cast",
    )(x, g.reshape(1, D_MODEL), w_in, w_in, w_out)
    return y, (wa, wb, wo)


def _ssm_weights_kernel(ldt_ref, lr_ref, li_ref, br_ref, bi_ref, cr_ref, ci_ref,
                        ar_ref, ai_ref, wb_ref, wcr_ref, wci_ref, win_ref, wor_ref, woi_ref, wl_ref):
    dt = jnp.exp(ldt_ref[...])
    lr = lr_ref[...]
    li = li_ref[...]
    mag = jnp.exp(lr * dt)
    ar = mag * jnp.cos(li * dt)
    ai = mag * jnp.sin(li * dt)
    den = lr * lr + li * li
    zr = ((ar - 1.0) * lr + ai * li) / den
    zi = (ai * lr - (ar - 1.0) * li) / den
    ar_ref[...] = ar
    ai_ref[...] = ai
    br = br_ref[...]
    bi = bi_ref[...]
    bbr = zr[:, None, :] * br - zi[:, None, :] * bi
    bbi = zr[:, None, :] * bi + zi[:, None, :] * br
    a1r, a1i = ar[:, None, :], ai[:, None, :]
    a2r, a2i = (ar * ar - ai * ai)[:, None, :], (2.0 * ar * ai)[:, None, :]
    abr = a1r * bbr - a1i * bbi
    abi = a1r * bbi + a1i * bbr
    cr = cr_ref[...]
    ci = ci_ref[...]
    car = cr * a1r - ci * a1i
    cai = cr * a1i + ci * a1r
    ca2r = cr * a2r - ci * a2i
    ca2i = cr * a2i + ci * a2r

    def re_dot(xr, xi):
        dims = (((2,), (2,)), ((0,), (0,)))
        return (lax.dot_general(xr, bbr, dims, preferred_element_type=F32)
                - lax.dot_general(xi, bbi, dims, preferred_element_type=F32))

    k0 = re_dot(cr, ci)
    k1 = re_dot(car, cai)

    def same_group(n_rows, row_span, n_cols, col_span):
        r = lax.broadcasted_iota(jnp.int32, (n_rows, n_cols), 0) // row_span
        c = lax.broadcasted_iota(jnp.int32, (n_rows, n_cols), 1) // col_span
        return r == c

    def replicate(n, copies, transpose):
        shape = (copies * n, n) if transpose else (n, copies * n)
        r = lax.broadcasted_iota(jnp.int32, shape, 0)
        c = lax.broadcasted_iota(jnp.int32, shape, 1)
        return jnp.where(r % n == c % n, 1.0, 0.0).astype(BF16)

    gpb = GROUPS_PER_BLOCK
    rep_p = replicate(SSM_STATE, gpb, False)
    rep_pt = replicate(SSM_STATE, gpb, True)
    rep_ht = replicate(SSM_GROUP, gpb, True)
    m_in = same_group(LANES, SSM_GROUP, STATE_BLOCK, SSM_STATE)
    m_out = same_group(STATE_BLOCK, SSM_STATE, LANES, SSM_GROUP)
    m_mix = same_group(LANES, SSM_GROUP, LANES, SSM_GROUP)
    nt = (((1,), (1,)), ((), ()))

    def rows_of(x, k):
        return x[k * gpb:(k + 1) * gpb].reshape(LANES, x.shape[-1]).astype(BF16)

    def bd_in(x, k):
        return jnp.where(m_in, jnp.dot(rows_of(x, k), rep_p, preferred_element_type=F32), 0.0)

    def bd_out(x, k):
        return jnp.where(m_out, lax.dot_general(rep_pt, rows_of(x, k), nt, preferred_element_type=F32), 0.0)

    def bd_mix(x, k):
        return jnp.where(m_mix, lax.dot_general(rep_ht, rows_of(x, k), nt, preferred_element_type=F32), 0.0)

    for k in range(CH_BLOCKS):
        one_step = jnp.concatenate([bd_in(bbr, k), bd_in(bbi, k)], axis=1)
        wb_ref[k] = one_step.astype(BF16)
        wcr_ref[k] = bd_out(cr, k).astype(BF16)
        wci_ref[k] = (-bd_out(ci, k)).astype(BF16)
        win_ref[k] = jnp.concatenate([jnp.concatenate([bd_in(abr, k), bd_in(abi, k)], axis=1), one_step],
                                     axis=0).astype(BF16)
        wor_ref[k] = jnp.concatenate([bd_out(car, k), bd_out(ca2r, k)], axis=1).astype(BF16)
        woi_ref[k] = (-jnp.concatenate([bd_out(cai, k), bd_out(ca2i, k)], axis=1)).astype(BF16)
        m0, m1 = bd_mix(k0, k), bd_mix(k1, k)
        wl_ref[k] = jnp.concatenate([jnp.concatenate([m0, m1], axis=1),
                                     jnp.concatenate([jnp.zeros_like(m0), m0], axis=1)], axis=0).astype(BF16)


def _ssm_weights(log_dt, lam_re, lam_im, b_re, b_im, c_re, c_im):
    g, p = N_SSM_GROUPS, SSM_STATE
    gp = jax.ShapeDtypeStruct((g, p), F32)
    stack = lambda rows, cols: jax.ShapeDtypeStruct((CH_BLOCKS, rows, cols), BF16)
    return pl.pallas_call(
        _ssm_weights_kernel,
        out_shape=(gp, gp,
                   stack(LANES, 2 * STATE_BLOCK), stack(STATE_BLOCK, LANES), stack(STATE_BLOCK, LANES),
                   stack(2 * LANES, 2 * STATE_BLOCK), stack(STATE_BLOCK, 2 * LANES), stack(STATE_BLOCK, 2 * LANES),
                   stack(2 * LANES, 2 * LANES)),
        compiler_params=pltpu.CompilerParams(vmem_limit_bytes=VMEM_LIMIT_BYTES),
        name="s5_weights",
    )(log_dt.reshape(g, 1), lam_re, lam_im, jnp.swapaxes(b_re, 1, 2), jnp.swapaxes(b_im, 1, 2), c_re, c_im)


SCAN_COLS = 512


def _s5_kernel(h_ref, g_ref, s0r_ref, s0i_ref, ar_ref, ai_ref, wb_ref, wcr_ref, wci_ref, d_ref,
               gw_ref, gb_ref, o_ref, sr_ref, si_ref, u_s, xr_s, xi_s, *, rb, tt):
    n_slab, per_slab = h_ref.shape[:2]
    assert n_slab * per_slab == rb * tt and (n_slab == rb or tt == 1)

    def seq_rows(s):
        return pl.ds(s, per_slab, stride=n_slab) if n_slab > 1 else pl.ds(0, per_slab)

    @pl.when(pl.program_id(0) == 0)
    def _():
        sr_ref[...] = s0r_ref[...]
        si_ref[...] = s0i_ref[...]

    for s in range(n_slab):
        un = _rms(h_ref[s], g_ref[...])
        for k in range(CH_BLOCKS):
            u_s[k, seq_rows(s), :] = un[:, k * LANES:(k + 1) * LANES]
    for k in range(CH_BLOCKS):
        bu = jnp.dot(u_s[k].astype(BF16), wb_ref[k], preferred_element_type=F32)
        xr_s[:, k * STATE_BLOCK:(k + 1) * STATE_BLOCK] = bu[:, :STATE_BLOCK]
        xi_s[:, k * STATE_BLOCK:(k + 1) * STATE_BLOCK] = bu[:, STATE_BLOCK:]

    for c in range(STATE_W // SCAN_COLS):
        cs = slice(c * SCAN_COLS, (c + 1) * SCAN_COLS)
        a_r = jnp.broadcast_to(ar_ref[:, cs], (rb, SCAN_COLS))
        a_i = jnp.broadcast_to(ai_ref[:, cs], (rb, SCAN_COLS))

        def step(t, carry, cs=cs, a_r=a_r, a_i=a_i):
            s_r, s_i = carry
            rows = pl.ds(pl.multiple_of(t * rb, rb), rb)
            n_r = a_r * s_r - a_i * s_i + xr_s[rows, cs]
            n_i = a_r * s_i + a_i * s_r + xi_s[rows, cs]
            xr_s[rows, cs] = n_r
            xi_s[rows, cs] = n_i
            return n_r, n_i

        s_r, s_i = lax.fori_loop(0, tt, step, (sr_ref[:, cs], si_ref[:, cs]), unroll=True)
        sr_ref[:, cs] = s_r
        si_ref[:, cs] = s_i

    ys = []
    for k in range(CH_BLOCKS):
        ks = slice(k * STATE_BLOCK, (k + 1) * STATE_BLOCK)
        ys.append(jnp.dot(xr_s[:, ks].astype(BF16), wcr_ref[k], preferred_element_type=F32)
                  + jnp.dot(xi_s[:, ks].astype(BF16), wci_ref[k], preferred_element_type=F32)
                  + d_ref[:, k * LANES:(k + 1) * LANES] * u_s[k])
    act = jax.nn.gelu(jnp.concatenate(ys, axis=1)).astype(BF16)
    z = jnp.dot(act, gw_ref[...], preferred_element_type=F32) + gb_ref[...]
    mix = z[:, :D_MODEL] * jax.nn.sigmoid(z[:, D_MODEL:])
    for k in range(CH_BLOCKS):
        u_s[k] = mix[:, k * LANES:(k + 1) * LANES]
    for s in range(n_slab):
        o_ref[s] = h_ref[s] + jnp.concatenate([u_s[k, seq_rows(s), :] for k in range(CH_BLOCKS)], axis=1)


def _s5(h, g, s0r, s0i, ar, ai, wb, wcr, wci, d_skip, glu_w, glu_b, *, layer, tt):
    rb, t_len, _ = h.shape
    assert t_len % tt == 0 and rb % 8 == 0
    rows = rb * tt
    block = (1, rb, D_MODEL) if t_len == 1 else (rb, tt, D_MODEL)
    h_in = h.reshape(1, rb, D_MODEL) if t_len == 1 else h
    state = jax.ShapeDtypeStruct((rb, STATE_W), F32)
    out, sr, si = pl.pallas_call(
        functools.partial(_s5_kernel, rb=rb, tt=tt),
        grid=(t_len // tt,),
        in_specs=[pl.BlockSpec(block, lambda i: (0, i, 0)),
                  _const_spec((1, D_MODEL)),
                  _const_spec((rb, STATE_W)), _const_spec((rb, STATE_W)),
                  _const_spec((1, STATE_W)), _const_spec((1, STATE_W)),
                  _const_spec((CH_BLOCKS, LANES, 2 * STATE_BLOCK)),
                  _const_spec((CH_BLOCKS, STATE_BLOCK, LANES)),
                  _const_spec((CH_BLOCKS, STATE_BLOCK, LANES)),
                  _const_spec((1, D_MODEL)),
                  _layer_spec((D_MODEL, 2 * D_MODEL), layer),
                  _const_spec((1, 2 * D_MODEL))],
        out_specs=(pl.BlockSpec(block, lambda i: (0, i, 0)),
                   pl.BlockSpec((rb, STATE_W), lambda i: (0, 0)),
                   pl.BlockSpec((rb, STATE_W), lambda i: (0, 0))),
        out_shape=(jax.ShapeDtypeStruct(h_in.shape, F32), state, state),
        scratch_shapes=[pltpu.VMEM((CH_BLOCKS, rows, LANES), F32),
                        pltpu.VMEM((rows, STATE_W), F32), pltpu.VMEM((rows, STATE_W), F32)],
        compiler_params=_params("arbitrary"),
        name="s5",
    )(h_in, g.reshape(1, D_MODEL), s0r, s0i, ar, ai, wb, wcr, wci, d_skip.reshape(1, D_MODEL),
      glu_w, glu_b.reshape(1, 2 * D_MODEL))
    return out.reshape(h.shape), sr, si


def _s5_pair_kernel(h_ref, g_ref, ar_ref, ai_ref, win_ref, wor_ref, woi_ref, wl_ref, d_ref,
                    gw_ref, gb_ref, o_ref, sr_ref, si_ref, u_s, xr_s, xi_s, *, rb, tt):
    pairs = tt // 2
    prow = pairs * rb

    def seq_rows(s):
        return pl.ds(s, tt, stride=rb)

    def split(x):
        x3 = x.reshape(pairs, 2 * rb, x.shape[-1])
        return x3[:, :rb].reshape(prow, x.shape[-1]), x3[:, rb:].reshape(prow, x.shape[-1])

    def interleave(even, odd):
        n = even.shape[-1]
        return jnp.concatenate([even.reshape(pairs, rb, n), odd.reshape(pairs, rb, n)], axis=1).reshape(tt * rb, n)

    @pl.when(pl.program_id(0) == 0)
    def _():
        sr_ref[...] = jnp.zeros_like(sr_ref)
        si_ref[...] = jnp.zeros_like(si_ref)

    for s in range(rb):
        un = _rms(h_ref[s], g_ref[...])
        for k in range(CH_BLOCKS):
            u_s[k, seq_rows(s), :] = un[:, k * LANES:(k + 1) * LANES]

    lhs = []
    for k in range(CH_BLOCKS):
        ks = slice(k * STATE_BLOCK, (k + 1) * STATE_BLOCK)
        lhs.append(jnp.concatenate(split(u_s[k]), axis=1).astype(BF16))
        w = jnp.dot(lhs[k], win_ref[k], preferred_element_type=F32)
        xr_s[:, ks] = w[:, :STATE_BLOCK]
        xi_s[:, ks] = w[:, STATE_BLOCK:]

    for c in range(STATE_W // SCAN_COLS):
        cs = slice(c * SCAN_COLS, (c + 1) * SCAN_COLS)
        a_r, a_i = ar_ref[:, cs], ai_ref[:, cs]
        a2_r = jnp.broadcast_to(a_r * a_r - a_i * a_i, (rb, SCAN_COLS))
        a2_i = jnp.broadcast_to(2.0 * a_r * a_i, (rb, SCAN_COLS))

        def step(j, carry, cs=cs, a2_r=a2_r, a2_i=a2_i):
            s_r, s_i = carry
            rows = pl.ds(pl.multiple_of(j * rb, rb), rb)
            n_r = a2_r * s_r - a2_i * s_i + xr_s[rows, cs]
            n_i = a2_r * s_i + a2_i * s_r + xi_s[rows, cs]
            xr_s[rows, cs] = s_r
            xi_s[rows, cs] = s_i
            return n_r, n_i

        s_r, s_i = lax.fori_loop(0, pairs, step, (sr_ref[:, cs], si_ref[:, cs]), unroll=True)
        sr_ref[:, cs] = s_r
        si_ref[:, cs] = s_i

    ys = []
    for k in range(CH_BLOCKS):
        ks = slice(k * STATE_BLOCK, (k + 1) * STATE_BLOCK)
        y2 = (jnp.dot(xr_s[:, ks].astype(BF16), wor_ref[k], preferred_element_type=F32)
              + jnp.dot(xi_s[:, ks].astype(BF16), woi_ref[k], preferred_element_type=F32)
              + jnp.dot(lhs[k], wl_ref[k], preferred_element_type=F32))
        ys.append(interleave(y2[:, :LANES], y2[:, LANES:]) + d_ref[:, k * LANES:(k + 1) * LANES] * u_s[k])
    act = jax.nn.gelu(jnp.concatenate(ys, axis=1)).astype(BF16)
    z = jnp.dot(act, gw_ref[...], preferred_element_type=F32) + gb_ref[...]
    mix = z[:, :D_MODEL] * jax.nn.sigmoid(z[:, D_MODEL:])
    for k in range(CH_BLOCKS):
        u_s[k] = mix[:, k * LANES:(k + 1) * LANES]
    for s in range(rb):
        o_ref[s] = h_ref[s] + jnp.concatenate([u_s[k, seq_rows(s), :] for k in range(CH_BLOCKS)], axis=1)


def _s5_pairs(h, g, ar, ai, w_in, w_out_r, w_out_i, w_loc, d_skip, glu_w, glu_b, *, layer, tt):
    rb, t_len, _ = h.shape
    assert t_len % tt == 0 and tt % 2 == 0 and rb % 8 == 0
    rows, prow = rb * tt, rb * tt // 2
    block = (rb, tt, D_MODEL)
    state = jax.ShapeDtypeStruct((rb, STATE_W), F32)
    return pl.pallas_call(
        functools.partial(_s5_pair_kernel, rb=rb, tt=tt),
        grid=(t_len // tt,),
        in_specs=[pl.BlockSpec(block, lambda i: (0, i, 0)),
                  _const_spec((1, D_MODEL)),
                  _const_spec((1, STATE_W)), _const_spec((1, STATE_W)),
                  _const_spec((CH_BLOCKS, 2 * LANES, 2 * STATE_BLOCK)),
                  _const_spec((CH_BLOCKS, STATE_BLOCK, 2 * LANES)),
                  _const_spec((CH_BLOCKS, STATE_BLOCK, 2 * LANES)),
                  _const_spec((CH_BLOCKS, 2 * LANES, 2 * LANES)),
                  _const_spec((1, D_MODEL)),
                  _layer_spec((D_MODEL, 2 * D_MODEL), layer),
                  _const_spec((1, 2 * D_MODEL))],
        out_specs=(pl.BlockSpec(block, lambda i: (0, i, 0)),
                   pl.BlockSpec((rb, STATE_W), lambda i: (0, 0)),
                   pl.BlockSpec((rb, STATE_W), lambda i: (0, 0))),
        out_shape=(jax.ShapeDtypeStruct(h.shape, F32), state, state),
        scratch_shapes=[pltpu.VMEM((CH_BLOCKS, rows, LANES), F32),
                        pltpu.VMEM((prow, STATE_W), F32), pltpu.VMEM((prow, STATE_W), F32)],
        compiler_params=_params("arbitrary"),
        name="s5_pairs",
    )(h, g.reshape(1, D_MODEL), ar, ai, w_in, w_out_r, w_out_i, w_loc, d_skip.reshape(1, D_MODEL),
      glu_w, glu_b.reshape(1, 2 * D_MODEL))


def _proj_kernel(x_ref, g_ref, w_ref, o_ref, *, scale):
    xn = _rms(x_ref[...], g_ref[...]).astype(BF16)
    y = jnp.dot(xn, w_ref[...], preferred_element_type=F32)
    o_ref[...] = y if scale == 1.0 else y * scale


def _proj(x, g, w, layer=None, scale=1.0, tm=512):
    n = x.shape[0]
    tm = min(tm, n)
    nout = w.shape[-1]
    return pl.pallas_call(
        functools.partial(_proj_kernel, scale=scale),
        grid=(n // tm,),
        in_specs=[pl.BlockSpec((tm, D_MODEL), lambda i: (i, 0)),
                  _const_spec((1, D_MODEL)),
                  _layer_spec((D_MODEL, nout), layer)],
        out_specs=pl.BlockSpec((tm, nout), lambda i: (i, 0)),
        out_shape=jax.ShapeDtypeStruct((n, nout), F32),
        compiler_params=_params("parallel"),
        name="proj",
    )(x, g.reshape(1, D_MODEL), w)


def _kv_tails(t_len, tm):
    tails = []
    for w in DIL_WINDOWS:
        n_tail = min(w, t_len)
        if n_tail >= tm:
            assert n_tail % tm == 0
            tails.append(((t_len - n_tail) // tm, 0, tm))
        else:
            tails.append((t_len // tm - 1, tm - n_tail, n_tail))
    return tails


def _kv_kernel(x_ref, g_ref, w_ref, kv_ref, *tail_refs, tails):
    j = pl.program_id(1)
    xn = _rms(x_ref[...], g_ref[...]).astype(BF16)
    kv = jnp.dot(xn, w_ref[...], preferred_element_type=F32)
    kv_ref[...] = kv
    for g, (first_tile, row0, n_tok) in enumerate(tails):
        def emit(g=g, row0=row0, n_tok=n_tok):
            tail_refs[g][...] = kv[row0:row0 + n_tok, g * 2 * ATTN_WIDTH:(g + 1) * 2 * ATTN_WIDTH].T

        if first_tile == 0:
            emit()
        else:
            pl.when(j >= first_tile)(emit)


def _kv_proj(x, g, w, tm=512):
    bn, t_len, _ = x.shape
    tm = min(tm, t_len)
    assert t_len % tm == 0
    nout = w.shape[-1]
    tails = _kv_tails(t_len, tm)
    tail_specs, tail_shapes = [], []
    for (first_tile, _, n_tok), win in zip(tails, DIL_WINDOWS):
        tail_specs.append(pl.BlockSpec((None, 2 * ATTN_WIDTH, n_tok),
                                       lambda b, j, f=first_tile: (b, 0, jnp.maximum(j - f, 0))))
        tail_shapes.append(jax.ShapeDtypeStruct((bn, 2 * ATTN_WIDTH, min(win, t_len)), F32))
    return pl.pallas_call(
        functools.partial(_kv_kernel, tails=tails),
        grid=(bn, t_len // tm),
        in_specs=[pl.BlockSpec((None, tm, D_MODEL), lambda b, j: (b, j, 0)),
                  _const_spec((1, D_MODEL)),
                  _const_spec((D_MODEL, nout))],
        out_specs=[pl.BlockSpec((None, tm, nout), lambda b, j: (b, j, 0))] + tail_specs,
        out_shape=[jax.ShapeDtypeStruct((bn, t_len, nout), F32)] + tail_shapes,
        compiler_params=_params("parallel", "arbitrary"),
        name="kv_proj",
    )(x, g.reshape(1, D_MODEL), w)


def _oproj_kernel(a_ref, w_ref, h_ref, o_ref):
    o_ref[...] = h_ref[...] + jnp.dot(a_ref[...].astype(BF16), w_ref[...], preferred_element_type=F32)


def _oproj(a, w, layer, h, tm=512):
    n = h.shape[0]
    tm = min(tm, n)
    return pl.pallas_call(
        _oproj_kernel,
        grid=(n // tm,),
        in_specs=[pl.BlockSpec((tm, ATTN_WIDTH), lambda i: (i, 0)),
                  _layer_spec((ATTN_WIDTH, D_MODEL), layer),
                  pl.BlockSpec((tm, D_MODEL), lambda i: (i, 0))],
        out_specs=pl.BlockSpec((tm, D_MODEL), lambda i: (i, 0)),
        out_shape=jax.ShapeDtypeStruct((n, D_MODEL), F32),
        compiler_params=_params("parallel"),
        name="oproj",
    )(a, w, h)


def _alibi_slopes():
    n = N_DIL * HEADS_PER_GROUP
    s = jnp.exp2(-8.0 * jnp.arange(1, n + 1, dtype=F32) / n)
    return s.reshape(N_DIL, HEADS_PER_GROUP)


def _prompt_bias():
    qi = jnp.arange(R_LEN)[:, None]
    kj = jnp.arange(2 * R_LEN)[None, :]
    diff = R_LEN + qi - kj
    valid = (diff >= 0) & (diff <= R_LEN)
    dil = jnp.asarray(DIL_RATES, jnp.int32)[:, None, None]
    dist = (dil * diff[None]).astype(F32)
    bias = -_alibi_slopes()[:, :, None, None] * dist[:, None]
    return jnp.where(valid[None, None], bias * LOG2_E, NEG)


def _sample_bias():
    kk = R_LEN - jnp.arange(R_LEN)
    dil = jnp.asarray(DIL_RATES, jnp.int32)[:, None]
    dist = (dil * kk[None]).astype(F32)
    return -_alibi_slopes()[:, :, None] * dist[:, None]


MERGE_GROUP = DIL_RATES.index(1)
INTERLEAVE = 16


def _attn_kernel(q0, q1, q2, k0, k1, k2, v0, v1, v2, bias_ref, o_ref, acc_s, m_s, l_s, *, t_len):
    qs, ks, vs = (q0, q1, q2), (k0, k1, k2), (v0, v1, v2)
    lo = lax.broadcasted_iota(jnp.int32, (R_LEN, LANES), 1) < HEAD_DIM

    def block(g, q_idx, k_idx, first):
        q = qs[g][q_idx, :]
        qb = jnp.concatenate([jnp.where(lo, q, 0.0), jnp.where(lo, 0.0, q)], axis=0).astype(BF16)
        kb = ks[g][k_idx, :].astype(BF16)
        vb = vs[g][k_idx, :].astype(BF16)
        s = lax.dot_general(qb, kb, (((1,), (1,)), ((), ())), preferred_element_type=F32)
        bias = bias_ref[g, :, :, R_LEN:] if first else bias_ref[g]
        bias = bias.reshape(2 * R_LEN, bias.shape[-1])
        s = jnp.where(bias > 0.5 * NEG, s + bias, NEG)
        m = jnp.max(s, axis=-1, keepdims=True)
        p = jnp.exp2(s - m)
        l = jnp.sum(p, axis=-1, keepdims=True)
        a = jnp.dot(p.astype(BF16), vb, preferred_element_type=F32)
        acc = jnp.where(lo, a[:R_LEN], a[R_LEN:])
        mm = jnp.where(lo, m[:R_LEN], m[R_LEN:])
        ll = jnp.where(lo, l[:R_LEN], l[R_LEN:])
        if g != MERGE_GROUP:
            acc_s[g, q_idx, :] = acc
            m_s[g, q_idx, :] = mm
            l_s[g, q_idx, :] = ll
            return
        others = [h for h in range(N_DIL) if h != MERGE_GROUP]
        ms = [mm] + [m_s[h, q_idx, :] for h in others]
        mx = functools.reduce(jnp.maximum, ms)
        ws = [jnp.exp2(x - mx) for x in ms]
        ls = [ll] + [l_s[h, q_idx, :] for h in others]
        accs = [acc] + [acc_s[h, q_idx, :] for h in others]
        den = sum(w * x for w, x in zip(ws, ls))
        o_ref[q_idx, :] = sum(w * x for w, x in zip(ws, accs)) / den

    for g in [h for h in range(N_DIL) if h != MERGE_GROUP] + [MERGE_GROUP]:
        dil = DIL_RATES[g]
        span = R_LEN * dil
        n_blocks = t_len // span

        def rows(start, n, dil=dil):
            if dil > 1:
                return pl.ds(start, n, stride=dil)
            return pl.ds(start if isinstance(start, int) else pl.multiple_of(start, R_LEN), n)

        if dil <= INTERLEAVE:
            for r in range(dil):
                block(g, rows(r, R_LEN), rows(r, R_LEN), True)
        else:
            def first_blocks(i, c, g=g, rows=rows):
                for j in range(INTERLEAVE):
                    r = i * INTERLEAVE + j
                    block(g, rows(r, R_LEN), rows(r, R_LEN), True)
                return c

            lax.fori_loop(0, dil // INTERLEAVE, first_blocks, 0)

        later = n_blocks - 1
        if later == 0:
            continue
        per_trip = max(w for w in range(1, max(INTERLEAVE // dil, 1) + 1) if later % w == 0)

        def later_blocks(i, c, g=g, dil=dil, span=span, rows=rows, per_trip=per_trip):
            for j in range(per_trip):
                nb = 1 + i * per_trip + j
                for r in range(dil):
                    start = nb * span + r
                    block(g, rows(start, R_LEN), rows(start - span, 2 * R_LEN), False)
            return c

        if later == per_trip:
            later_blocks(0, 0)
        else:
            lax.fori_loop(0, later // per_trip, later_blocks, 0)


def _attn_prompt(q, kv, bias):
    bn, t_len, _ = q.shape
    pairs = ATTN_WIDTH // LANES
    per_group = 2 * ATTN_WIDTH // LANES

    def col(off):
        return pl.BlockSpec((None, t_len, LANES), lambda b, hp: (b, 0, off + hp))

    in_specs = ([col(g * pairs) for g in range(N_DIL)]
                + [col(g * per_group) for g in range(N_DIL)]
                + [col(g * per_group + pairs) for g in range(N_DIL)]
                + [pl.BlockSpec((N_DIL, 2, R_LEN, 2 * R_LEN), lambda b, hp: (0, hp, 0, 0))])
    scratch = pltpu.VMEM((N_DIL, t_len, LANES), F32)
    return pl.pallas_call(
        functools.partial(_attn_kernel, t_len=t_len),
        grid=(bn, pairs),
        in_specs=in_specs,
        out_specs=pl.BlockSpec((None, t_len, LANES), lambda b, hp: (b, 0, hp)),
        out_shape=jax.ShapeDtypeStruct((bn, t_len, ATTN_WIDTH), F32),
        scratch_shapes=[scratch, scratch, scratch],
        compiler_params=_params("parallel", "parallel"),
        name="attn_prompt",
    )(q, q, q, kv, kv, kv, kv, kv, kv, bias)


SAMPLE_BLOCK = 16


def _window_kernel(c_ref, o_ref, *, dil):
    w = c_ref.shape[-1]
    pos = lax.broadcasted_iota(jnp.int32, (w, R_LEN), 0)
    col = lax.broadcasted_iota(jnp.int32, (w, R_LEN), 1)
    pick = jnp.where(pos == col * dil, 1.0, 0.0).astype(BF16)
    o_ref[...] = jnp.dot(c_ref[...].astype(BF16), pick, preferred_element_type=F32).astype(BF16)


def _strided_window(cache, dil):
    bd, w = cache.shape[:2]
    assert w == R_LEN * dil
    slab = jnp.transpose(cache, (0, 2, 3, 4, 1)).reshape(bd, 2 * ATTN_WIDTH, w)
    if dil == 1:
        return slab
    return pl.pallas_call(
        functools.partial(_window_kernel, dil=dil),
        grid=(bd,),
        in_specs=[pl.BlockSpec((None, 2 * ATTN_WIDTH, w), lambda i: (i, 0, 0))],
        out_specs=pl.BlockSpec((None, 2 * ATTN_WIDTH, R_LEN), lambda i: (i, 0, 0)),
        out_shape=jax.ShapeDtypeStruct((bd, 2 * ATTN_WIDTH, R_LEN), BF16),
        compiler_params=_params("parallel"),
        name="cache_window",
    )(slab)


def _attn_sample_kernel(q_ref, kvn_ref, c0, c1, c2, bias_ref, o_ref):
    windows = (c0, c1, c2)
    sub = lax.broadcasted_iota(jnp.int32, (HEADS_PER_GROUP, ATTN_WIDTH), 0)
    lane = lax.broadcasted_iota(jnp.int32, (HEADS_PER_GROUP, ATTN_WIDTH), 1)
    own = (lane // HEAD_DIM) == sub
    for b in range(SAMPLE_BLOCK):
        os_, ms, ls = [], [], []
        for g in range(N_DIL):
            base = g * 2 * ATTN_WIDTH
            qbd = jnp.where(own, q_ref[b:b + 1, g * ATTN_WIDTH:(g + 1) * ATTN_WIDTH], 0.0)
            kt = windows[g][b, :ATTN_WIDTH, :].astype(BF16)
            vt = windows[g][b, ATTN_WIDTH:, :].astype(BF16)
            kn = kvn_ref[b:b + 1, base:base + ATTN_WIDTH]
            vn = kvn_ref[b:b + 1, base + ATTN_WIDTH:base + 2 * ATTN_WIDTH]
            s = jnp.dot(qbd.astype(BF16), kt, preferred_element_type=F32) + bias_ref[g]
            s_new = jnp.sum(qbd * kn, axis=-1, keepdims=True)
            m = jnp.maximum(jnp.max(s, axis=-1, keepdims=True), s_new)
            p = jnp.exp(s - m)
            p_new = jnp.exp(s_new - m)
            ls.append(jnp.sum(p, axis=-1, keepdims=True) + p_new)
            os_.append(lax.dot_general(p.astype(BF16), vt, (((1,), (1,)), ((), ())),
                                       preferred_element_type=F32) + p_new * vn)
            ms.append(m)
        mx = jnp.maximum(jnp.maximum(ms[0], ms[1]), ms[2])
        ws = [jnp.exp(m - mx) for m in ms]
        den = ws[0] * ls[0] + ws[1] * ls[1] + ws[2] * ls[2]
        o = (ws[0] * os_[0] + ws[1] * os_[1] + ws[2] * os_[2]) / den
        o_ref[b:b + 1, :] = jnp.sum(jnp.where(own, o, 0.0), axis=0, keepdims=True)


def _attn_sample(q, kv_new, windows, bias):
    bd = q.shape[0]
    assert bd % SAMPLE_BLOCK == 0
    window_spec = pl.BlockSpec((SAMPLE_BLOCK, 2 * ATTN_WIDTH, R_LEN), lambda i: (i, 0, 0))
    return pl.pallas_call(
        _attn_sample_kernel,
        grid=(bd // SAMPLE_BLOCK,),
        in_specs=[pl.BlockSpec((SAMPLE_BLOCK, N_DIL * ATTN_WIDTH), lambda i: (i, 0)),
                  pl.BlockSpec((SAMPLE_BLOCK, N_DIL * 2 * ATTN_WIDTH), lambda i: (i, 0))]
                 + [window_spec] * N_DIL + [_const_spec((N_DIL, HEADS_PER_GROUP, R_LEN))],
        out_specs=pl.BlockSpec((SAMPLE_BLOCK, ATTN_WIDTH), lambda i: (i, 0)),
        out_shape=jax.ShapeDtypeStruct((bd, ATTN_WIDTH), F32),
        compiler_params=_params("parallel"),
        name="attn_sample",
    )(q, kv_new, *windows, bias)


def kernel(x_prompt, x_sample, state_ssm_re, state_ssm_im, cache_kv_w128, cache_kv_w512, cache_kv_w2048, norm_ffn1, ffn1_w_in, ffn1_w_out, norm_mix, norm_ffn2, ffn2_w_in, ffn2_w_out, norm_kv, norm_final, ssm_log_dt, ssm_lambda_re, ssm_lambda_im, ssm_b_re, ssm_b_im, ssm_c_re, ssm_c_im, ssm_d, glu_w, glu_b, attn_w_q, attn_w_kv, attn_w_o):
    bn, t_len, _ = x_prompt.shape
    bd = x_sample.shape[0]
    assert x_sample.shape[1] == 1

    glu_wb = glu_w.astype(BF16)
    w_q, w_kv, w_o = attn_w_q.astype(BF16), attn_w_kv.astype(BF16), attn_w_o.astype(BF16)
    q_scale = HEAD_DIM ** -0.5

    ssm, ssm_pairs = [], []
    for layer in range(N_A_LAYERS):
        ar, ai, wb, wcr, wci, *pair_w = _ssm_weights(
            ssm_log_dt[layer], ssm_lambda_re[layer], ssm_lambda_im[layer],
            ssm_b_re[layer], ssm_b_im[layer], ssm_c_re[layer], ssm_c_im[layer])
        ar, ai = ar.reshape(1, STATE_W), ai.reshape(1, STATE_W)
        tail = (ssm_d[layer].reshape(D_MODEL), glu_wb, glu_b[layer])
        ssm.append((ar, ai, wb, wcr, wci) + tail)
        ssm_pairs.append((ar, ai, *pair_w) + tail)

    def states(parts, n_seq):
        return jnp.stack([p.reshape(n_seq, N_SSM_GROUPS, SSM_STATE) for p in parts])

    s0r = state_ssm_re.reshape(N_A_LAYERS, bd, STATE_W)
    s0i = state_ssm_im.reshape(N_A_LAYERS, bd, STATE_W)
    windows = [_strided_window(cache, dil)
               for cache, dil in zip((cache_kv_w128, cache_kv_w512, cache_kv_w2048), DIL_RATES)]
    sample_bias, prompt_bias = _sample_bias(), _prompt_bias()
    fin_r, fin_i, fin_rs, fin_is = [], [], [], []
    h = x_prompt.reshape(bn * t_len, D_MODEL)
    hs = x_sample.reshape(bd, D_MODEL)
    hs, w1 = _ffn_cast(hs, norm_ffn1[0], ffn1_w_in, ffn1_w_out, 0)
    for layer in range(DEPTH):
        last = layer == DEPTH - 1
        if layer == N_A_LAYERS:
            kv_p, *kv_tails = _kv_proj(h.reshape(bn, t_len, D_MODEL), norm_kv, w_kv)
            kv_s = _proj(hs, norm_kv, w_kv)
        ffn1 = _ffn(h, norm_ffn1[layer], w1, cast_next=(ffn2_w_in, ffn2_w_out, layer),
                    q_proj=((norm_mix[layer], w_q, layer - N_A_LAYERS, q_scale * LOG2_E)
                            if layer >= N_A_LAYERS else None),
                    extra=hs if layer > 0 else None)
        if layer > 0:
            hs = ffn1[-1]
        h, w2 = ffn1[0], ffn1[2 if layer >= N_A_LAYERS else 1]
        if layer < N_A_LAYERS:
            h, sr, si = _s5_pairs(h.reshape(bn, t_len, D_MODEL), norm_mix[layer],
                                  *ssm_pairs[layer], layer=layer, tt=64)
            h = h.reshape(bn * t_len, D_MODEL)
            fin_r.append(sr)
            fin_i.append(si)
            hs, sr, si = _s5(hs.reshape(bd, 1, D_MODEL), norm_mix[layer], s0r[layer], s0i[layer],
                             *ssm[layer], layer=layer, tt=1)
            hs = hs.reshape(bd, D_MODEL)
            fin_rs.append(sr)
            fin_is.append(si)
            attn = None
        else:
            b = layer - N_A_LAYERS
            merged = _attn_prompt(ffn1[1].reshape(bn, t_len, -1), kv_p, prompt_bias)
            attn = (merged.reshape(bn * t_len, ATTN_WIDTH), w_o, b)
            qs = _proj(hs, norm_mix[layer], w_q, layer=b, scale=q_scale)
            hs = _oproj(_attn_sample(qs, kv_s, windows, sample_bias), w_o, b, hs)
        if last:
            h, hs = _ffn(h, norm_ffn2[layer], w2, attn=attn, final_g=norm_final, extra=hs)
        else:
            h, w1, hs = _ffn(h, norm_ffn2[layer], w2, attn=attn, extra=hs,
                             cast_next=(ffn1_w_in, ffn1_w_out, layer + 1))
    y_prompt = h.reshape(bn, t_len, D_MODEL)
    y_sample = hs.reshape(bd, 1, D_MODEL)
    ssm_re_p, ssm_im_p = states(fin_r, bn), states(fin_i, bn)
    ssm_re_s, ssm_im_s = states(fin_rs, bd), states(fin_is, bd)
    kv_s = kv_s.reshape(bd, 1, N_DIL, 2, HEADS_PER_GROUP, HEAD_DIM)
    kv_out_p = [jnp.transpose(tail.reshape(bn, 2, HEADS_PER_GROUP, HEAD_DIM, -1), (0, 4, 1, 2, 3))
                for tail in kv_tails]

    return (y_prompt, y_sample, ssm_re_p, ssm_im_p, kv_out_p[0], kv_out_p[1], kv_out_p[2],
            ssm_re_s, ssm_im_s, kv_s[:, :, 0], kv_s[:, :, 1], kv_s[:, :, 2])
```
